```python
import math
import jax
import jax.numpy as jnp
from jax import lax
import numpy as np

D_MODEL = 2048
BATCH = 8
SEQ = 2048
DEPTH = 2

GRID_W = 64
CTX_LEN = 256
EPS = 1e-6
N_MOD = 9

FFN_HIDDEN = 5632

MLA_HEADS = D_MODEL // 256
MLA_NOPE = 128
MLA_ROPE = 64
MLA_V = 128
MLA_Q_LORA = D_MODEL // 4
MLA_KV_LORA = D_MODEL // 8
MLA_WIDTH = MLA_HEADS * MLA_V
ROPE_BASE = 10000.0
ATTN_BLOCK = 128

HY_WIDTH = D_MODEL // 4
HY_ORDER = 2
HY_SHORT = 3
HY_EMB = 33
HY_FFN = 64
HY_SIN_FREQ = 1.0
HY_MIN_DECAY = math.log(1e-2) / 1.5
HY_MAX_DECAY = math.log(1e-2) / 0.3
HY_WINDOW_SHIFT = 0.05

RET_HEADS = D_MODEL // 512
RET_DK = 64
RET_DV = 128
RET_WIDTH = RET_HEADS * RET_DV
RET_CHUNK = 128

COL_SIZES = (MLA_KV_LORA, MLA_ROPE, RET_HEADS * RET_DK, RET_WIDTH,
             MLA_Q_LORA, RET_HEADS * RET_DK, RET_WIDTH, (HY_ORDER + 1) * HY_WIDTH)
N_KV_COLS = sum(COL_SIZES[:4])
N_IN = sum(COL_SIZES)

kernel_name = 'hybrid_mla_hyena_retention_dit_block'


def split_cols(z, sizes):
    return jnp.split(z, [int(s) for s in np.cumsum(sizes)[:-1]], axis=-1)


def rms_norm(x, gain=None):
    xf = x.astype(jnp.float32)
    y = xf * lax.rsqrt(jnp.mean(xf * xf, axis=-1, keepdims=True) + EPS)
    if gain is not None:
        y = y * gain.astype(jnp.float32)
    return y.astype(x.dtype)


def modulate(x, shift, scale):
    return rms_norm(x) * (1 + scale) + shift


def modulation(cond, w, b):
    m = jax.nn.silu(cond) @ w + b
    m = m.reshape((-1, 1, m.shape[-1]))
    return jnp.split(m, N_MOD, axis=-1)


def swiglu(h, w_gate, w_up, w_down):
    return (jax.nn.silu(h @ w_gate) * (h @ w_up)) @ w_down


def axial_rope_tables(rows, dtype):
    n_freq = MLA_ROPE // 4
    inv_freq = ROPE_BASE ** (-jnp.arange(n_freq, dtype=jnp.float32) / n_freq)
    row = jnp.repeat(jnp.arange(rows), GRID_W).astype(jnp.float32)
    col = jnp.tile(jnp.arange(GRID_W), rows).astype(jnp.float32)
    tabs = []
    for pos in (row, col):
        ang = pos[:, None] * inv_freq[None, :]
        tabs += [jnp.cos(ang)[None, :, None, :].astype(dtype), jnp.sin(ang)[None, :, None, :].astype(dtype)]
    return tuple(tabs)


def rotate(x, cos, sin):
    x1, x2 = jnp.split(x, 2, axis=-1)
    return jnp.concatenate([x1 * cos - x2 * sin, x1 * sin + x2 * cos], axis=-1)


def apply_axial_rope(x, rope):
    cos_r, sin_r, cos_c, sin_c = rope
    x_row, x_col = jnp.split(x, 2, axis=-1)
    return jnp.concatenate([rotate(x_row, cos_r, sin_r), rotate(x_col, cos_c, sin_c)], axis=-1)


def mla_queries(q_lat, p, rope):
    b, n, _ = q_lat.shape
    q = (rms_norm(q_lat, p['mla_q_norm']) @ p['mla_wuq']).reshape(b, n, MLA_HEADS, MLA_NOPE + MLA_ROPE)
    q_nope = rms_norm(q[..., :MLA_NOPE], p['mla_qn_nope'])
    q_rope = rms_norm(q[..., MLA_NOPE:], p['mla_qn_rope'])
    if rope is not None:
        q_rope = apply_axial_rope(q_rope, rope)
    return jnp.concatenate([q_nope, q_rope], axis=-1)


def mla_keys_values(kv_lat, k_rope, p, rope):
    b, n, _ = kv_lat.shape
    kv = (rms_norm(kv_lat, p['mla_kv_norm']) @ p['mla_wukv']).reshape(b, n, MLA_HEADS, MLA_NOPE + MLA_V)
    k_nope = rms_norm(kv[..., :MLA_NOPE], p['mla_kn_nope'])
    k_r = rms_norm(k_rope, p['mla_kn_rope'])[:, :, None, :]
    if rope is not None:
        k_r = apply_axial_rope(k_r, rope)
    k = jnp.concatenate([k_nope, jnp.broadcast_to(k_r, (b, n, MLA_HEADS, MLA_ROPE))], axis=-1)
    return k, kv[..., MLA_NOPE:]


def block_attention(q, k, v):
    b, lq, h, dk = q.shape
    scale = dk ** -0.5
    qb = q.reshape(b, lq // ATTN_BLOCK, ATTN_BLOCK, h, dk).transpose(1, 0, 2, 3, 4)

    def one_block(qi):
        s = jnp.einsum('bqhd,bkhd->bhqk', qi, k).astype(jnp.float32) * scale
        pr = jax.nn.softmax(s, axis=-1).astype(v.dtype)
        return jnp.einsum('bhqk,bkhd->bqhd', pr, v)

    o = lax.map(one_block, qb)
    return o.transpose(1, 0, 2, 3, 4).reshape(b, lq, h, v.shape[-1])


def hyena_filter_spectra(n, p):
    f32 = jnp.float32
    t = jnp.linspace(0.0, 1.0, n, dtype=f32)[:, None]
    bands = (HY_EMB - 1) // 2
    w = 2 * math.pi * jnp.arange(n, dtype=f32)[:, None] / n
    f = jnp.linspace(1e-4, bands - 1, bands, dtype=f32)[None, :]
    feats = jnp.concatenate([t, jnp.cos(f * w), -jnp.sin(f * w)], axis=-1)
    h = jnp.sin(HY_SIN_FREQ * (feats @ p['hy_ffn_w1'].astype(f32) + p['hy_ffn_b1'].astype(f32)))
    h = jnp.sin(HY_SIN_FREQ * (h @ p['hy_ffn_w2'].astype(f32) + p['hy_ffn_b2'].astype(f32)))
    h = (h @ p['hy_ffn_w3'].astype(f32)).reshape(n, 2, HY_ORDER, HY_WIDTH)
    deltas = jnp.abs(jnp.linspace(HY_MIN_DECAY, HY_MAX_DECAY, HY_WIDTH, dtype=f32))
    window = jnp.exp(-t * deltas[None, :]) + HY_WINDOW_SHIFT
    h = h * window[:, None, None, :]
    h_fwd, h_bwd = h[:, 0], h[:, 1]
    two_sided = jnp.concatenate([h_fwd[:1] + h_bwd[:1], h_fwd[1:], jnp.zeros_like(h_fwd[:1]), h_bwd[:0:-1]], axis=0)
    return jnp.fft.rfft(two_sided, axis=0)


def long_conv(u, h_spec, skip):
    n = u.shape[1]
    uf = u.astype(jnp.float32)
    y = jnp.fft.irfft(jnp.fft.rfft(uf, n=2 * n, axis=1) * h_spec[None], n=2 * n, axis=1)[:, :n]
    return (y + uf * skip.astype(jnp.float32)).astype(u.dtype)


def short_conv(u, w, b):
    n = u.shape[1]
    pad = (HY_SHORT - 1) // 2
    up = jnp.pad(u, ((0, 0), (pad, pad), (0, 0)))
    return sum(up[:, j:j + n] * w[j] for j in range(HY_SHORT)) + b


def hyena_mix(u, p):
    n = u.shape[1]
    x1, x2, v = jnp.split(short_conv(u, p['hy_conv_w'], p['hy_conv_b']), HY_ORDER + 1, axis=-1)
    spec = hyena_filter_spectra(n, p)
    y = v
    for o, gate in enumerate((x1, x2)):
        y = gate * long_conv(y, spec[:, o], p['hy_skip'][o])
    return y


def to_heads(z, d):
    b, n, _ = z.shape
    return z.reshape(b, n, -1, d).transpose(0, 2, 1, 3).astype(jnp.float32)


def retention_scan(q, k, v, log_gamma, s0):
    b, h, n, dk = q.shape
    dv = v.shape[-1]
    nc = n // RET_CHUNK
    idx = jnp.arange(RET_CHUNK, dtype=jnp.float32)
    diff = idx[:, None] - idx[None, :]
    lg = log_gamma[:, None, None]
    intra_decay = jnp.where(diff >= 0, jnp.exp(lg * jnp.maximum(diff, 0.0)), 0.0)
    q_decay = jnp.exp(log_gamma[:, None] * (idx + 1.0))
    k_decay = jnp.exp(log_gamma[:, None] * (RET_CHUNK - 1.0 - idx))
    chunk_decay = jnp.exp(log_gamma * RET_CHUNK)
    qc = q.reshape(b, h, nc, RET_CHUNK, dk)
    kc = k.reshape(b, h, nc, RET_CHUNK, dk)
    vc = v.reshape(b, h, nc, RET_CHUNK, dv)
    scores = jnp.einsum('bhcid,bhcjd->bhcij', qc, kc) * intra_decay[None, :, None]
    o_intra = jnp.einsum('bhcij,bhcjv->bhciv', scores, vc)
    kv_chunk = jnp.einsum('bhcjd,bhcjv->cbhdv', kc * k_decay[None, :, None, :, None], vc)

    def step(s, kv):
        return chunk_decay[None, :, None, None] * s + kv, s

    _, s_prev = lax.scan(step, s0, kv_chunk)
    o_cross = jnp.einsum('bhcid,cbhdv->bhciv', qc, s_prev) * q_decay[None, :, None, :, None]
    return (o_intra + o_cross).reshape(b, h, n, dv)


def retention_context_state(k, v, log_gamma, backward):
    n = k.shape[2]
    pos = jnp.arange(n, dtype=jnp.float32)
    dist = pos if backward else (n - 1.0) - pos
    w = jnp.exp(log_gamma[:, None] * dist[None, :])
    return jnp.einsum('hn,bhnd,bhne->bhde', w, k, v)


def retention_mix(q, k, v, log_gamma, s0):
    fwd = retention_scan(q, k, v, log_gamma[0], s0[0])
    bwd = retention_scan(q[:, :, ::-1], k[:, :, ::-1], v[:, :, ::-1], log_gamma[1], s0[1])[:, :, ::-1]
    return fwd + bwd


def retention_output(o, gate, gn_w, gn_b):
    b, h, n, dv = o.shape
    mu = jnp.mean(o, axis=-1, keepdims=True)
    var = jnp.mean(jnp.square(o - mu), axis=-1, keepdims=True)
    o = ((o - mu) * lax.rsqrt(var + EPS)).transpose(0, 2, 1, 3).reshape(b, n, h * dv)
    o = o * gn_w.astype(jnp.float32) + gn_b.astype(jnp.float32)
    return (jax.nn.silu(gate.astype(jnp.float32)) * o).astype(gate.dtype)


def token_mixers(q_lat, keys, values, rq, rk, rv, s0, r_gate, hy_in, log_gamma, p, rope):
    b, n, _ = q_lat.shape
    q = mla_queries(q_lat, p, rope)
    attn = block_attention(q, keys, values).reshape(b, n, MLA_WIDTH)
    hy = hyena_mix(hy_in, p)
    ret = retention_mix(rq, rk, rv, log_gamma, s0)
    merged = jnp.concatenate([rms_norm(attn, p['mla_out_norm']),
                              rms_norm(hy, p['hy_out_norm']),
                              retention_output(ret, r_gate, p['ret_gn_w'], p['ret_gn_b'])], axis=-1)
    return merged @ p['w_out']


def trunk_layer(x, ctx, mod_x, mod_c, p, rope, need_ctx_out):
    sx1, cx1, gx1, sx2, cx2, gx2, sx3, cx3, gx3 = mod_x
    sc1, cc1, gc1, sc2, cc2, gc2, sc3, cc3, gc3 = mod_c
    ffn1 = (p['ffn1_gate'], p['ffn1_up'], p['ffn1_down'])
    ffn2 = (p['ffn2_gate'], p['ffn2_up'], p['ffn2_down'])

    x = x + 0.5 * gx1 * swiglu(modulate(x, sx1, cx1), *ffn1)
    ctx = ctx + 0.5 * gc1 * swiglu(modulate(ctx, sc1, cc1), *ffn1)

    log_gamma = jax.nn.log_sigmoid(p['ret_decay'].astype(jnp.float32))

    hc = modulate(ctx, sc2, cc2)
    if need_ctx_out:
        zc = split_cols(hc @ p['w_in'], COL_SIZES)
    else:
        zc = split_cols(hc @ p['w_in'][:, :N_KV_COLS], COL_SIZES[:4])
    k_c, v_c = mla_keys_values(zc[0], zc[1], p, None)
    rk_c = to_heads(zc[2], RET_DK) * RET_DK ** -0.5
    rv_c = to_heads(zc[3], RET_DV)
    s_ctx = (retention_context_state(rk_c, rv_c, log_gamma[0], False),
             retention_context_state(rk_c, rv_c, log_gamma[1], True))

    kv_lat, k_rope, r_k, r_v, q_lat, r_q, r_gate, hy_in = split_cols(modulate(x, sx2, cx2) @ p['w_in'], COL_SIZES)
    k_x, v_x = mla_keys_values(kv_lat, k_rope, p, rope)
    mixed = token_mixers(q_lat, jnp.concatenate([k_c, k_x], axis=1), jnp.concatenate([v_c, v_x], axis=1),
                         to_heads(r_q, RET_DK), to_heads(r_k, RET_DK) * RET_DK ** -0.5, to_heads(r_v, RET_DV),
                         s_ctx, r_gate, hy_in, log_gamma, p, rope)
    x = x + gx2 * mixed

    x = x + 0.5 * gx3 * swiglu(modulate(x, sx3, cx3), *ffn2)

    if need_ctx_out:
        zero = jnp.zeros_like(s_ctx[0])
        mixed_c = token_mixers(zc[4], k_c, v_c, to_heads(zc[5], RET_DK), rk_c, rv_c, (zero, zero),
                               zc[6], zc[7], log_gamma, p, None)
        ctx = ctx + gc2 * mixed_c
        ctx = ctx + 0.5 * gc3 * swiglu(modulate(ctx, sc3, cc3), *ffn2)
    return x, ctx


def setup_inputs(seed: int = 0) -> dict:
    key = jax.random.key(seed)
    keys = iter(jax.random.split(key, 48))
    f32 = jnp.float32

    def normal(shape, scale):
        return jax.random.normal(next(keys), shape, f32) * scale

    def gain(shape):
        return 1.0 + normal(shape, 0.02)

    D, L, F = D_MODEL, DEPTH, FFN_HIDDEN
    gam = 1.0 - 2.0 ** (-5.0 - np.arange(RET_HEADS, dtype=np.float32))
    decay_logit = jnp.asarray(np.log(gam / (1.0 - gam)), f32)
    return {
        'x': normal((BATCH, SEQ, D), 1.0),
        'c': normal((BATCH, D), 1.0),
        'ctx': normal((BATCH, CTX_LEN, D), 1.0),
        'c_ctx': normal((D,), 1.0),
        'ada_w': normal((L, D, N_MOD * D), 0.5 * D ** -0.5),
        'ada_b': normal((L, N_MOD * D), 0.02),
        'ffn1_gate': normal((L, D, F), D ** -0.5),
        'ffn1_up': normal((L, D, F), D ** -0.5),
        'ffn1_down': normal((L, F, D), F ** -0.5),
        'w_in': normal((L, D, N_IN), D ** -0.5),
        'mla_q_norm': gain((L, MLA_Q_LORA)),
        'mla_wuq': normal((L, MLA_Q_LORA, MLA_HEADS * (MLA_NOPE + MLA_ROPE)), MLA_Q_LORA ** -0.5),
        'mla_kv_norm': gain((L, MLA_KV_LORA)),
        'mla_wukv': normal((L, MLA_KV_LORA, MLA_HEADS * (MLA_NOPE + MLA_V)), MLA_KV_LORA ** -0.5),
        'mla_qn_nope': gain((L, MLA_NOPE)),
        'mla_qn_rope': gain((L, MLA_ROPE)),
        'mla_kn_nope': gain((L, MLA_NOPE)),
        'mla_kn_rope': gain((L, MLA_ROPE)),
        'mla_out_norm': gain((L, MLA_WIDTH)),
        'hy_conv_w': normal((L, HY_SHORT, (HY_ORDER + 1) * HY_WIDTH), HY_SHORT ** -0.5),
        'hy_conv_b': normal((L, (HY_ORDER + 1) * HY_WIDTH), 0.02),
        'hy_ffn_w1': normal((L, HY_EMB, HY_FFN), HY_EMB ** -0.5),
        'hy_ffn_b1': normal((L, HY_FFN), 0.02),
        'hy_ffn_w2': normal((L, HY_FFN, HY_FFN), HY_FFN ** -0.5),
        'hy_ffn_b2': normal((L, HY_FFN), 0.02),
        'hy_ffn_w3': normal((L, HY_FFN, 2 * HY_ORDER * HY_WIDTH), HY_FFN ** -0.5),
        'hy_skip': normal((L, HY_ORDER, HY_WIDTH), 1.0),
        'hy_out_norm': gain((L, HY_WIDTH)),
        'ret_decay': decay_logit[None, None, :] + normal((L, 2, RET_HEADS), 0.05),
        'ret_gn_w': gain((L, RET_WIDTH)),
        'ret_gn_b': normal((L, RET_WIDTH), 0.02),
        'w_out': normal((L, D, D), D ** -0.5),
        'ffn2_gate': normal((L, D, F), D ** -0.5),
        'ffn2_up': normal((L, D, F), D ** -0.5),
        'ffn2_down': normal((L, F, D), F ** -0.5),
    }


def reference(x, c, ctx, c_ctx, ada_w, ada_b, ffn1_gate, ffn1_up, ffn1_down, w_in,
              mla_q_norm, mla_wuq, mla_kv_norm, mla_wukv, mla_qn_nope, mla_qn_rope,
              mla_kn_nope, mla_kn_rope, mla_out_norm, hy_conv_w, hy_conv_b, hy_ffn_w1,
              hy_ffn_b1, hy_ffn_w2, hy_ffn_b2, hy_ffn_w3, hy_skip, hy_out_norm, ret_decay,
              ret_gn_w, ret_gn_b, w_out, ffn2_gate, ffn2_up, ffn2_down):
    ROWS = x.shape[1] // GRID_W
    rope = axial_rope_tables(ROWS, x.dtype)
    for l in range(DEPTH):
        p = {
            'ffn1_gate': ffn1_gate[l], 'ffn1_up': ffn1_up[l], 'ffn1_down': ffn1_down[l],
            'w_in': w_in[l],
            'mla_q_norm': mla_q_norm[l], 'mla_wuq': mla_wuq[l],
            'mla_kv_norm': mla_kv_norm[l], 'mla_wukv': mla_wukv[l],
            'mla_qn_nope': mla_qn_nope[l], 'mla_qn_rope': mla_qn_rope[l],
            'mla_kn_nope': mla_kn_nope[l], 'mla_kn_rope': mla_kn_rope[l],
            'mla_out_norm': mla_out_norm[l],
            'hy_conv_w': hy_conv_w[l], 'hy_conv_b': hy_conv_b[l],
            'hy_ffn_w1': hy_ffn_w1[l], 'hy_ffn_b1': hy_ffn_b1[l],
            'hy_ffn_w2': hy_ffn_w2[l], 'hy_ffn_b2': hy_ffn_b2[l], 'hy_ffn_w3': hy_ffn_w3[l],
            'hy_skip': hy_skip[l], 'hy_out_norm': hy_out_norm[l],
            'ret_decay': ret_decay[l], 'ret_gn_w': ret_gn_w[l], 'ret_gn_b': ret_gn_b[l],
            'w_out': w_out[l],
            'ffn2_gate': ffn2_gate[l], 'ffn2_up': ffn2_up[l], 'ffn2_down': ffn2_down[l],
        }
        mod_x = modulation(c, ada_w[l], ada_b[l])
        mod_c = modulation(c_ctx, ada_w[l], ada_b[l])
        x, ctx = trunk_layer(x, ctx, mod_x, mod_c, p, rope, l < DEPTH - 1)
    return x
```

```python
import functools
import math

import jax
import jax.numpy as jnp
import numpy as np
from jax import lax
from jax.experimental import pallas as pl
from jax.experimental.pallas import tpu as pltpu

F32 = jnp.float32
BF16 = jnp.bfloat16

D_MODEL = 2048
BATCH = 8
SEQ = 2048
DEPTH = 2
GRID_W = 64
CTX_LEN = 256
EPS = 1e-6
N_MOD = 9
FFN_HIDDEN = 5632

MLA_HEADS = 8
MLA_NOPE = 128
MLA_ROPE = 64
MLA_V = 128
MLA_Q_LORA = 512
MLA_KV_LORA = 256
MLA_WIDTH = MLA_HEADS * MLA_V
ROPE_BASE = 10000.0

HY_WIDTH = 512
HY_ORDER = 2
HY_EMB = 33
HY_FFN = 64
HY_SIN_FREQ = 1.0
HY_MIN_DECAY = math.log(1e-2) / 1.5
HY_MAX_DECAY = math.log(1e-2) / 0.3
HY_WINDOW_SHIFT = 0.05

RET_HEADS = 4
RET_DK = 64
RET_DV = 128
RET_WIDTH = RET_HEADS * RET_DV
RET_CHUNK = 128

VMEM_LIMIT_BYTES = 60 * 1024 * 1024
LANES = 128
HEAD_SLOT = 256

Z_HY = 0
Z_RV = 1536
Z_QLAT = 2048
Z_RGATE = 2560
Z_RQ = 3072
Z_RK = 3584
Z_KVLAT = 4096
Z_KR = 4352
Z_COLS = 4608

TM = 1024
TN = 512


def _cparams(*sem):
    return pltpu.CompilerParams(dimension_semantics=sem, vmem_limit_bytes=VMEM_LIMIT_BYTES)


def _rms(x, width=None):
    width = x.shape[-1] if width is None else width
    return x * lax.rsqrt(jnp.sum(x * x, axis=-1, keepdims=True) / width + EPS)


def _silu(x):
    return x * jax.nn.sigmoid(x)


def _mod_kernel(c_ref, w_ref, b_ref, o_ref):
    a = _silu(c_ref[...])
    o_ref[...] = jnp.dot(a, w_ref[...], preferred_element_type=F32,
                         precision=lax.Precision.HIGHEST) + b_ref[...]


def _modulation(cond, ada_w, ada_b):
    nl, d, nout = ada_w.shape
    bn = 1024
    return pl.pallas_call(
        _mod_kernel,
        grid=(nl, nout // bn),
        in_specs=[pl.BlockSpec((16, d), lambda l, j: (0, 0)),
                  pl.BlockSpec((None, d, bn), lambda l, j: (l, 0, j)),
                  pl.BlockSpec((None, 1, bn), lambda l, j: (l, 0, j))],
        out_specs=pl.BlockSpec((None, 16, bn), lambda l, j: (l, 0, j)),
        out_shape=jax.ShapeDtypeStruct((nl, 16, nout), F32),
        compiler_params=_cparams("arbitrary", "arbitrary"),
        name="modulation",
    )(cond, ada_w, ada_b.reshape(nl, 1, nout))


def _normmod_kernel(x_ref, sh_ref, sc_ref, o_ref):
    y = _rms(x_ref[...])
    o_ref[...] = (y * (1.0 + sc_ref[...]) + sh_ref[...]).astype(BF16)


def _normmod(s, shift, scale, n):
    r, d = s.shape
    tm = min(512, n)
    per_batch = shift.shape[0] > 1
    mod_spec = pl.BlockSpec((None, 1, d), lambda i: ((i * tm) // n if per_batch else 0, 0, 0))
    return pl.pallas_call(
        _normmod_kernel,
        grid=(r // tm,),
        in_specs=[pl.BlockSpec((tm, d), lambda i: (i, 0)), mod_spec, mod_spec],
        out_specs=pl.BlockSpec((tm, d), lambda i: (i, 0)),
        out_shape=jax.ShapeDtypeStruct((r, d), BF16),
        compiler_params=_cparams("arbitrary"),
        name="normmod",
    )(s, shift, scale)


def _ffn_up_kernel(h_ref, wg_ref, wu_ref, o_ref):
    h = h_ref[...]
    g = jnp.dot(h, wg_ref[...], preferred_element_type=F32)
    u = jnp.dot(h, wu_ref[...], preferred_element_type=F32)
    o_ref[...] = (_silu(g) * u).astype(BF16)


def _ffn_up(h, wg, wu, l):
    r, d = h.shape
    f = wg.shape[-1]
    w_spec = pl.BlockSpec((None, d, TN), lambda i, j: (l, 0, j))
    return pl.pallas_call(
        _ffn_up_kernel,
        grid=(r // TM, f // TN),
        in_specs=[pl.BlockSpec((TM, d), lambda i, j: (i, 0)), w_spec, w_spec],
        out_specs=pl.BlockSpec((TM, TN), lambda i, j: (i, j)),
        out_shape=jax.ShapeDtypeStruct((r, f), BF16),
        compiler_params=_cparams("arbitrary", "arbitrary"),
        name="ffn_up",
    )(h, wg, wu)


def _mm_kernel(a_ref, w_ref, o_ref):
    o_ref[...] = jnp.dot(a_ref[...], w_ref[...], preferred_element_type=F32).astype(o_ref.dtype)


def _matmul(a, w, l):
    r, k = a.shape
    nout = w.shape[-1]
    return pl.pallas_call(
        _mm_kernel,
        grid=(r // TM, nout // TN),
        in_specs=[pl.BlockSpec((TM, k), lambda i, j: (i, 0)),
                  pl.BlockSpec((None, k, TN), lambda i, j: (l, 0, j))],
        out_specs=pl.BlockSpec((TM, TN), lambda i, j: (i, j)),
        out_shape=jax.ShapeDtypeStruct((r, nout), BF16),
        compiler_params=_cparams("arbitrary", "arbitrary"),
        name="matmul",
    )(a, w)


def _mm_res_kernel(*refs, n_in, coeff):
    a_refs, w_refs = refs[:n_in], refs[n_in:2 * n_in]
    s_ref, g_ref, o_ref = refs[2 * n_in:]
    acc = None
    for a_ref, w_ref in zip(a_refs, w_refs):
        part = jnp.dot(a_ref[...], w_ref[...], preferred_element_type=F32)
        acc = part if acc is None else acc + part
    o_ref[...] = s_ref[...] + (coeff * g_ref[...]) * acc


def _matmul_residual(parts, w, l, s, gate, n, coeff):
    r, d = s.shape
    in_specs, w_specs, row = [], [], 0
    for p in parts:
        kp = p.shape[1]
        assert row % kp == 0
        in_specs.append(pl.BlockSpec((TM, kp), lambda i, j: (i, 0)))
        w_specs.append(pl.BlockSpec((None, kp, TN), functools.partial(
            lambda i, j, rb: (l, rb, j), rb=row // kp)))
        row += kp
    assert row == w.shape[1]
    per_batch = gate.shape[0] > 1
    return pl.pallas_call(
        functools.partial(_mm_res_kernel, n_in=len(parts), coeff=coeff),
        grid=(r // TM, d // TN),
        in_specs=in_specs + w_specs + [
            pl.BlockSpec((TM, TN), lambda i, j: (i, j)),
            pl.BlockSpec((None, 1, TN), lambda i, j: ((i * TM) // n if per_batch else 0, 0, j))],
        out_specs=pl.BlockSpec((TM, TN), lambda i, j: (i, j)),
        out_shape=jax.ShapeDtypeStruct((r, d), F32),
        compiler_params=_cparams("arbitrary", "arbitrary"),
        name="matmul_residual",
    )(*parts, *([w] * len(parts)), s, gate)


def _rope_lanes(x):
    lane = lax.broadcasted_iota(jnp.int32, x.shape, 1)
    ms = jnp.sum(jnp.where(lane < MLA_ROPE, x * x, 0.0), axis=-1, keepdims=True) / MLA_ROPE
    return x * lax.rsqrt(ms + EPS)


def _kv_kernel(lat_ref, kr_ref, gkv_ref, w_ref, gkn_ref, gkr_ref, tab_ref, k_ref, v_ref):
    a = _rms(lat_ref[...].astype(F32)) * gkv_ref[...]
    kv = jnp.dot(a.astype(BF16), w_ref[...], preferred_element_type=F32)
    t = _rope_lanes(kr_ref[...].astype(F32)) * gkr_ref[...] * tab_ref[...]
    krot = (t + pltpu.roll(t, MLA_ROPE, 1)).astype(BF16)
    for h in range(MLA_HEADS):
        c0 = h * HEAD_SLOT
        kn = _rms(kv[:, c0:c0 + MLA_NOPE]) * gkn_ref[...]
        k_ref[:, c0:c0 + MLA_NOPE] = kn.astype(BF16)
        k_ref[:, c0 + MLA_NOPE:c0 + HEAD_SLOT] = krot
        v_ref[:, h * MLA_V:(h + 1) * MLA_V] = kv[:, c0 + MLA_NOPE:c0 + HEAD_SLOT].astype(BF16)


def _kv_proj(z, l, n, wukv, g_kv, g_kn, g_kr, tab):
    r = z.shape[0]
    tm = min(512, n)
    nt = tab.shape[0] // tm
    vec = lambda w: pl.BlockSpec((None, 1, w), lambda i: (l, 0, 0))
    return pl.pallas_call(
        _kv_kernel,
        grid=(r // tm,),
        in_specs=[pl.BlockSpec((tm, MLA_KV_LORA), lambda i: (i, Z_KVLAT // MLA_KV_LORA)),
                  pl.BlockSpec((tm, LANES), lambda i: (i, Z_KR // LANES)),
                  vec(MLA_KV_LORA),
                  pl.BlockSpec((None, MLA_KV_LORA, MLA_HEADS * HEAD_SLOT), lambda i: (l, 0, 0)),
                  vec(MLA_NOPE), vec(LANES),
                  pl.BlockSpec((tm, LANES), lambda i: (i % nt, 0))],
        out_specs=[pl.BlockSpec((tm, MLA_HEADS * HEAD_SLOT), lambda i: (i, 0)),
                   pl.BlockSpec((tm, MLA_WIDTH), lambda i: (i, 0))],
        out_shape=[jax.ShapeDtypeStruct((r, MLA_HEADS * HEAD_SLOT), BF16),
                   jax.ShapeDtypeStruct((r, MLA_WIDTH), BF16)],
        compiler_params=_cparams("arbitrary"),
        name="kv_proj",
    )(z, z, g_kv, wukv, g_kn, g_kr, tab)


def _q_kernel(lat_ref, gq_ref, w_ref, gn_ref, gr_ref, tab_ref, q_ref, *, scale):
    a = _rms(lat_ref[...].astype(F32)) * gq_ref[...]
    q = jnp.dot(a.astype(BF16), w_ref[...], preferred_element_type=F32)
    rope_scale = gr_ref[...] * tab_ref[...] * scale
    for h in range(MLA_HEADS):
        c0 = h * HEAD_SLOT
        qn = _rms(q[:, c0:c0 + MLA_NOPE]) * (gn_ref[...] * scale)
        q_ref[:, c0:c0 + MLA_NOPE] = qn.astype(BF16)
        qr = _rope_lanes(q[:, c0 + MLA_NOPE:c0 + HEAD_SLOT]) * rope_scale
        q_ref[:, c0 + MLA_NOPE:c0 + HEAD_SLOT] = qr.astype(BF16)


def _q_proj(z, l, n, wuq, g_q, g_qn, g_qr, tab):
    r = z.shape[0]
    tm = min(512, n)
    nt = tab.shape[0] // tm
    vec = lambda w: pl.BlockSpec((None, 1, w), lambda i: (l, 0, 0))
    return pl.pallas_call(
        functools.partial(_q_kernel, scale=(MLA_NOPE + MLA_ROPE) ** -0.5),
        grid=(r // tm,),
        in_specs=[pl.BlockSpec((tm, MLA_Q_LORA), lambda i: (i, Z_QLAT // MLA_Q_LORA)),
                  vec(MLA_Q_LORA),
                  pl.BlockSpec((None, MLA_Q_LORA, MLA_HEADS * HEAD_SLOT), lambda i: (l, 0, 0)),
                  vec(MLA_NOPE), vec(LANES),
                  pl.BlockSpec((tm, LANES), lambda i: (i % nt, 0))],
        out_specs=pl.BlockSpec((tm, MLA_HEADS * HEAD_SLOT), lambda i: (i, 0)),
        out_shape=jax.ShapeDtypeStruct((r, MLA_HEADS * HEAD_SLOT), BF16),
        compiler_params=_cparams("arbitrary"),
        name="q_proj",
    )(z, g_q, wuq, g_qn, g_qr, tab)


def _attn_kernel(*refs, n_src):
    q_ref = refs[0]
    k_refs, v_refs = refs[1:1 + n_src], refs[1 + n_src:1 + 2 * n_src]
    g_ref, o_ref, acc_ref = refs[1 + 2 * n_src:]
    ssq = None
    for h in range(MLA_HEADS):
        q = q_ref[:, h * HEAD_SLOT:(h + 1) * HEAD_SLOT]
        ss = [lax.dot_general(q, k_ref[:, h * HEAD_SLOT:(h + 1) * HEAD_SLOT],
                              (((1,), (1,)), ((), ())), preferred_element_type=F32)
              for k_ref in k_refs]
        m = functools.reduce(jnp.maximum, [jnp.max(s, axis=-1, keepdims=True) for s in ss])
        ps = [jnp.exp(s - m) for s in ss]
        den = functools.reduce(jnp.add, [jnp.sum(p, axis=-1, keepdims=True) for p in ps])
        o = functools.reduce(jnp.add, [
            jnp.dot(p.astype(BF16), v_ref[:, h * MLA_V:(h + 1) * MLA_V], preferred_element_type=F32)
            for p, v_ref in zip(ps, v_refs)])
        o = o / den
        acc_ref[:, h * MLA_V:(h + 1) * MLA_V] = o
        sq = jnp.sum(o * o, axis=-1, keepdims=True)
        ssq = sq if ssq is None else ssq + sq
    o_ref[...] = (acc_ref[...] * lax.rsqrt(ssq / MLA_WIDTH + EPS) * g_ref[...]).astype(BF16)


def _attention(q, ks, vs, ns, l, g_out):
    r = q.shape[0]
    nq = r // BATCH
    tq = 256
    per_b = nq // tq
    k_specs = [pl.BlockSpec((nk, MLA_HEADS * HEAD_SLOT), lambda b, i: (b, 0)) for nk in ns]
    v_specs = [pl.BlockSpec((nk, MLA_WIDTH), lambda b, i: (b, 0)) for nk in ns]
    return pl.pallas_call(
        functools.partial(_attn_kernel, n_src=len(ks)),
        grid=(BATCH, per_b),
        in_specs=[pl.BlockSpec((tq, MLA_HEADS * HEAD_SLOT), lambda b, i: (b * per_b + i, 0))]
        + k_specs + v_specs
        + [pl.BlockSpec((None, 1, MLA_WIDTH), lambda b, i: (l, 0, 0))],
        out_specs=pl.BlockSpec((tq, MLA_WIDTH), lambda b, i: (b * per_b + i, 0)),
        out_shape=jax.ShapeDtypeStruct((r, MLA_WIDTH), BF16),
        scratch_shapes=[pltpu.VMEM((tq, MLA_WIDTH), F32)],
        compiler_params=_cparams("arbitrary", "arbitrary"),
        name="attention",
    )(q, *ks, *vs, g_out)


def _log_sigmoid(x):
    return jnp.minimum(x, 0.0) - jnp.log(1.0 + jnp.exp(-jnp.abs(x)))


def _ret_kernel(q_ref, k_ref, v_ref, gate_ref, dec_ref, s0_ref, gnw_ref, gnb_ref,
                o_ref, st_ref, acc_ref, *, n):
    nc = n // RET_CHUNK
    c = RET_CHUNK
    ii = lax.broadcasted_iota(jnp.int32, (c, c), 0).astype(F32)
    jj = lax.broadcasted_iota(jnp.int32, (c, c), 1).astype(F32)
    for d in range(2):
        lg = _log_sigmoid(dec_ref[d][0:1, :])
        lgm = jnp.broadcast_to(lg, (c, c))
        if d == 0:
            diff, qpow, kpow = ii - jj, ii + 1.0, (c - 1.0) - ii
        else:
            diff, qpow, kpow = jj - ii, c - ii, ii
        intra = jnp.where(diff >= 0, jnp.exp(lgm * jnp.maximum(diff, 0.0)), 0.0)
        qdec = jnp.exp(lgm * qpow)
        kdec = jnp.exp(lgm * kpow)
        cdec = jnp.exp(lg * float(c))

        def body(ci, state, d=d, intra=intra, qdec=qdec, kdec=kdec, cdec=cdec):
            cc = ci if d == 0 else nc - 1 - ci
            r0 = pl.multiple_of(cc * c, c)
            q = q_ref[pl.ds(r0, c), :]
            k = k_ref[pl.ds(r0, c), :].astype(F32) * (RET_DK ** -0.5)
            v = v_ref[pl.ds(r0, c), :]
            sc = lax.dot_general(q, k.astype(BF16), (((1,), (1,)), ((), ())),
                                 preferred_element_type=F32) * intra
            o = jnp.dot(sc.astype(BF16), v, preferred_element_type=F32)
            o = o + jnp.dot(q, state.astype(BF16), preferred_element_type=F32) * qdec
            kd = (k * kdec).astype(BF16)
            new_state = cdec * state + lax.dot_general(
                kd, v, (((0,), (0,)), ((), ())), preferred_element_type=F32)
            if d == 0:
                acc_ref[pl.ds(r0, c), :] = o
            else:
                acc_ref[pl.ds(r0, c), :] += o
            return new_state

        st_ref[d] = lax.fori_loop(0, nc, body, s0_ref[d])

    o = acc_ref[...]
    mu = jnp.mean(o, axis=-1, keepdims=True)
    var = jnp.mean(jnp.square(o - mu), axis=-1, keepdims=True)
    on = (o - mu) * lax.rsqrt(var + EPS) * gnw_ref[...] + gnb_ref[...]
    o_ref[...] = (_silu(gate_ref[...].astype(F32)) * on).astype(BF16)


def _retention(z, l, n, dec, s0, gn_w, gn_b):
    r = z.shape[0]
    col = lambda base: (lambda b, h: (b, base // LANES + h))
    st_spec = pl.BlockSpec((None, None, 2, LANES, RET_DV), lambda b, h: (b, h, 0, 0, 0))
    gn_spec = pl.BlockSpec((None, 1, RET_DV), lambda b, h: (l, 0, h))
    return pl.pallas_call(
        functools.partial(_ret_kernel, n=n),
        grid=(BATCH, RET_HEADS),
        in_specs=[pl.BlockSpec((n, LANES), col(Z_RQ)),
                  pl.BlockSpec((n, LANES), col(Z_RK)),
                  pl.BlockSpec((n, RET_DV), col(Z_RV)),
                  pl.BlockSpec((n, RET_DV), col(Z_RGATE)),
                  pl.BlockSpec((None, 2, None, 8, LANES), lambda b, h: (l, 0, h, 0, 0)),
                  st_spec, gn_spec, gn_spec],
        out_specs=[pl.BlockSpec((n, RET_DV), lambda b, h: (b, h)), st_spec],
        out_shape=[jax.ShapeDtypeStruct((r, RET_WIDTH), BF16),
                   jax.ShapeDtypeStruct((BATCH, RET_HEADS, 2, LANES, RET_DV), F32)],
        scratch_shapes=[pltpu.VMEM((n, RET_DV), F32)],
        compiler_params=_cparams("arbitrary", "arbitrary"),
        name="retention",
    )(z, z, z, z, dec, s0, gn_w, gn_b)


def _hy_filter_kernel(feat_ref, w1_ref, b1_ref, w2_ref, b2_ref, w3_ref, dl_ref, s_ref, d_ref):
    hi = lax.Precision.HIGHEST
    feats = feat_ref[...]
    h = jnp.sin(HY_SIN_FREQ * (jnp.dot(feats, w1_ref[...], preferred_element_type=F32, precision=hi)
                               + b1_ref[...]))
    h = jnp.sin(HY_SIN_FREQ * (jnp.dot(h, w2_ref[...], preferred_element_type=F32, precision=hi)
                               + b2_ref[...]))
    h = jnp.dot(h, w3_ref[...], preferred_element_type=F32, precision=hi)
    window = jnp.exp(-feats[:, 0:1] * dl_ref[...]) + HY_WINDOW_SHIFT
    window = jnp.concatenate([window] * HY_ORDER, axis=-1)
    half = HY_ORDER * HY_WIDTH
    h_fwd = h[:, :half] * window
    h_bwd = h[:, half:] * window
    s_ref[...] = (h_fwd + h_bwd).astype(BF16)
    d_ref[...] = (h_fwd - h_bwd).astype(BF16)


def _hy_filter(feats, w1, b1, w2, b2, w3, deltas, l):
    n = feats.shape[0]
    tn = 256
    half = HY_ORDER * HY_WIDTH
    full = lambda a, b: pl.BlockSpec((None, a, b), lambda i: (l, 0, 0))
    return pl.pallas_call(
        _hy_filter_kernel,
        grid=(n // tn,),
        in_specs=[pl.BlockSpec((tn, LANES), lambda i: (i, 0)),
                  full(LANES, HY_FFN), full(1, HY_FFN), full(HY_FFN, HY_FFN), full(1, HY_FFN),
                  full(HY_FFN, 2 * half),
                  pl.BlockSpec((1, HY_WIDTH), lambda i: (0, 0))],
        out_specs=[pl.BlockSpec((tn, half), lambda i: (i, 0))] * 2,
        out_shape=[jax.ShapeDtypeStruct((n, half), BF16)] * 2,
        compiler_params=_cparams("arbitrary"),
        name="hyena_filter",
    )(feats, w1, b1, w2, b2, w3, deltas)


def _hy_spec_kernel(fc_ref, fs_ref, alt_ref, s_ref, d_ref, a_ref, b_ref, dd_ref):
    s = s_ref[...]
    hc = jnp.dot(fc_ref[...], s, preferred_element_type=F32)
    hs = jnp.dot(fs_ref[...], d_ref[...], preferred_element_type=F32)
    nyq = jnp.dot(alt_ref[...], s, preferred_element_type=F32)[0:1, :]
    row = lax.broadcasted_iota(jnp.int32, hc.shape, 0) + pl.program_id(0) * hc.shape[0]
    a_ref[...] = hc
    b_ref[...] = jnp.where(row == 0, 0.0, hs)
    dd_ref[...] = jnp.where(row == 0, nyq, hc)


def _hy_spectrum(fc, fs, alt, s, d):
    n = fc.shape[0]
    tk = min(512, n)
    half = s.shape[1]
    out = pl.BlockSpec((tk, half), lambda i: (i, 0))
    return pl.pallas_call(
        _hy_spec_kernel,
        grid=(n // tk,),
        in_specs=[pl.BlockSpec((tk, n), lambda i: (i, 0)),
                  pl.BlockSpec((tk, n), lambda i: (i, 0)),
                  pl.BlockSpec((16, n), lambda i: (0, 0)),
                  pl.BlockSpec((n, half), lambda i: (0, 0)),
                  pl.BlockSpec((n, half), lambda i: (0, 0))],
        out_specs=[out] * 3,
        out_shape=[jax.ShapeDtypeStruct((n, half), F32)] * 3,
        compiler_params=_cparams("arbitrary"),
        name="hyena_spectrum",
    )(fc, fs, alt, s, d)


def _short_conv_kernel(u_ref, w_ref, b_ref, o_ref, pad_ref, *, n):
    zeros = jnp.zeros((8, pad_ref.shape[1]), F32)
    pad_ref[0:8, :] = zeros
    pad_ref[8 + n:16 + n, :] = zeros
    pad_ref[8:8 + n, :] = u_ref[...].astype(F32)
    w = w_ref[...]
    y = (pad_ref[7:7 + n, :] * w[0:1, :] + pad_ref[8:8 + n, :] * w[1:2, :]
         + pad_ref[9:9 + n, :] * w[2:3, :] + b_ref[...])
    o_ref[...] = y.astype(BF16)


def _short_conv(z, l, n, w, b):
    r = z.shape[0]
    wc = HY_WIDTH
    return pl.pallas_call(
        functools.partial(_short_conv_kernel, n=n),
        grid=(BATCH, (HY_ORDER + 1)),
        in_specs=[pl.BlockSpec((n, wc), lambda bi, j: (bi, Z_HY // wc + j)),
                  pl.BlockSpec((None, 3, wc), lambda bi, j: (l, 0, j)),
                  pl.BlockSpec((None, 1, wc), lambda bi, j: (l, 0, j))],
        out_specs=pl.BlockSpec((n, wc), lambda bi, j: (bi, j)),
        out_shape=jax.ShapeDtypeStruct((r, (HY_ORDER + 1) * wc), BF16),
        scratch_shapes=[pltpu.VMEM((n + 16, wc), F32)],
        compiler_params=_cparams("arbitrary", "arbitrary"),
        name="short_conv",
    )(z, w, b)


def _long_conv_kernel(u_ref, gate_ref, fc_ref, fs_ref, gc_ref, gs_ref, a_ref, b_ref, dd_ref,
                      skip_ref, gain_ref, o_ref, acc_ref, *, final):
    k = pl.program_id(1)
    u = u_ref[...]
    xc = jnp.dot(fc_ref[...], u, preferred_element_type=F32)
    xs = jnp.dot(fs_ref[...], u, preferred_element_type=F32)
    hb = b_ref[...]
    yc = (xc * a_ref[...] - xs * hb).astype(BF16)
    ys = (xc * hb + xs * dd_ref[...]).astype(BF16)
    part = (jnp.dot(gc_ref[...], yc, preferred_element_type=F32)
            + jnp.dot(gs_ref[...], ys, preferred_element_type=F32))

    @pl.when(k == 0)
    def _():
        acc_ref[...] = part

    @pl.when(k > 0)
    def _():
        acc_ref[...] += part

    @pl.when(k == pl.num_programs(1) - 1)
    def _():
        y = acc_ref[...] + u.astype(F32) * skip_ref[...]
        y = gate_ref[...].astype(F32) * y
        if final:
            y = _rms(y) * gain_ref[...]
        o_ref[...] = y.astype(BF16)


def _long_conv(cv, u_col, gate_col, mats, spec, order, skip, gain, l, n, final):
    fc, fs, gc, gs = mats
    a, bm, dd = spec
    src = cv if isinstance(cv, tuple) else (cv, cv)
    r = src[0].shape[0]
    wc = HY_WIDTH
    tk = min(512, n)
    f_spec = pl.BlockSpec((tk, n), lambda bi, k: (k, 0))
    g_spec = pl.BlockSpec((n, tk), lambda bi, k: (0, k))
    h_spec = pl.BlockSpec((tk, wc), lambda bi, k: (k, order))
    return pl.pallas_call(
        functools.partial(_long_conv_kernel, final=final),
        grid=(BATCH, n // tk),
        in_specs=[pl.BlockSpec((n, wc), lambda bi, k: (bi, u_col)),
                  pl.BlockSpec((n, wc), lambda bi, k: (bi, gate_col)),
                  f_spec, f_spec, g_spec, g_spec, h_spec, h_spec, h_spec,
                  pl.BlockSpec((None, None, 1, wc), lambda bi, k: (l, order, 0, 0)),
                  pl.BlockSpec((None, 1, wc), lambda bi, k: (l, 0, 0))],
        out_specs=pl.BlockSpec((n, wc), lambda bi, k: (bi, 0)),
        out_shape=jax.ShapeDtypeStruct((r, wc), BF16),
        scratch_shapes=[pltpu.VMEM((n, wc), F32)],
        compiler_params=_cparams("arbitrary", "arbitrary"),
        name="long_conv",
    )(src[0], src[1], fc, fs, gc, gs, a, bm, dd, skip, gain)


def _dft_mats(n):
    big = 2 * n
    k = jnp.arange(n, dtype=jnp.int32)
    ang = ((k[:, None] * k[None, :]) % big).astype(F32) * (2.0 * math.pi / big)
    alt = (1 - 2 * (k % 2)).astype(F32)
    fc = jnp.cos(ang)
    fs = jnp.where(k[:, None] == 0, alt[None, :], jnp.sin(ang))
    wk = jnp.where(k == 0, 1.0, 2.0).astype(F32) / big
    gc = fc.T * wk[None, :]
    gs = fs.T * wk[None, :]
    alt8 = jnp.broadcast_to(alt[None, :], (16, n))
    return tuple(m.astype(BF16) for m in (fc, fs, gc, gs)), alt8.astype(BF16)


def _hy_features(n):
    t = jnp.linspace(0.0, 1.0, n, dtype=F32)[:, None]
    bands = (HY_EMB - 1) // 2
    w = 2 * math.pi * jnp.arange(n, dtype=F32)[:, None] / n
    f = jnp.linspace(1e-4, bands - 1, bands, dtype=F32)[None, :]
    feats = jnp.concatenate([t, jnp.cos(f * w), -jnp.sin(f * w)], axis=-1)
    return jnp.pad(feats, ((0, 0), (0, LANES - HY_EMB)))


def _rope_table():
    n_freq = MLA_ROPE // 4
    inv_freq = ROPE_BASE ** (-jnp.arange(n_freq, dtype=F32) / n_freq)
    rows = SEQ // GRID_W
    row = jnp.repeat(jnp.arange(rows), GRID_W).astype(F32)
    col = jnp.tile(jnp.arange(GRID_W), rows).astype(F32)
    ar = row[:, None] * inv_freq[None, :]
    ac = col[:, None] * inv_freq[None, :]
    cos_part = jnp.concatenate([jnp.cos(ar), jnp.cos(ar), jnp.cos(ac), jnp.cos(ac)], axis=-1)
    sin_part = jnp.concatenate([-jnp.sin(ar), jnp.sin(ar), -jnp.sin(ac), jnp.sin(ac)], axis=-1)
    return jnp.concatenate([cos_part, sin_part], axis=-1)


_ROPE_PARTNER = np.concatenate([np.arange(16, 32), np.arange(0, 16),
                                np.arange(48, 64), np.arange(32, 48)])


def _w_in_columns():
    src = np.full((Z_COLS,), -1, np.int64)
    off = np.cumsum([0, MLA_KV_LORA, MLA_ROPE, RET_HEADS * RET_DK, RET_WIDTH, MLA_Q_LORA,
                     RET_HEADS * RET_DK, RET_WIDTH])
    o_kv, o_kr, o_rk, o_rv, o_q, o_rq, o_gate, o_hy = [int(v) for v in off]
    src[Z_HY:Z_HY + 3 * HY_WIDTH] = o_hy + np.arange(3 * HY_WIDTH)
    src[Z_RV:Z_RV + RET_WIDTH] = o_rv + np.arange(RET_WIDTH)
    src[Z_QLAT:Z_QLAT + MLA_Q_LORA] = o_q + np.arange(MLA_Q_LORA)
    src[Z_RGATE:Z_RGATE + RET_WIDTH] = o_gate + np.arange(RET_WIDTH)
    for h in range(RET_HEADS):
        src[Z_RQ + h * LANES:Z_RQ + h * LANES + RET_DK] = o_rq + h * RET_DK + np.arange(RET_DK)
        src[Z_RK + h * LANES:Z_RK + h * LANES + RET_DK] = o_rk + h * RET_DK + np.arange(RET_DK)
    src[Z_KVLAT:Z_KVLAT + MLA_KV_LORA] = o_kv + np.arange(MLA_KV_LORA)
    src[Z_KR:Z_KR + MLA_ROPE] = o_kr + np.arange(MLA_ROPE)
    src[Z_KR + MLA_ROPE:Z_KR + 2 * MLA_ROPE] = o_kr + _ROPE_PARTNER
    return src


def _take_cols(w, src):
    wz = jnp.concatenate([w, jnp.zeros(w.shape[:-1] + (1,), w.dtype)], axis=-1)
    return jnp.take(wz, jnp.asarray(np.where(src < 0, w.shape[-1], src)), axis=-1)


def _wuq_columns():
    per = MLA_NOPE + MLA_ROPE
    src = np.zeros((MLA_HEADS * HEAD_SLOT,), np.int64)
    for h in range(MLA_HEADS):
        c0 = h * HEAD_SLOT
        src[c0:c0 + MLA_NOPE] = h * per + np.arange(MLA_NOPE)
        src[c0 + MLA_NOPE:c0 + MLA_NOPE + MLA_ROPE] = h * per + MLA_NOPE + np.arange(MLA_ROPE)
        src[c0 + MLA_NOPE + MLA_ROPE:c0 + HEAD_SLOT] = h * per + MLA_NOPE + _ROPE_PARTNER
    return src


def kernel(x, c, ctx, c_ctx, ada_w, ada_b, ffn1_gate, ffn1_up, ffn1_down, w_in, mla_q_norm, mla_wuq, mla_kv_norm, mla_wukv, mla_qn_nope, mla_qn_rope, mla_kn_nope, mla_kn_rope, mla_out_norm, hy_conv_w, hy_conv_b, hy_ffn_w1, hy_ffn_b1, hy_ffn_w2, hy_ffn_b2, hy_ffn_w3, hy_skip, hy_out_norm, ret_decay, ret_gn_w, ret_gn_b, w_out, ffn2_gate, ffn2_up, ffn2_down):
    nl = DEPTH
    d = D_MODEL

    w1g, w1u, w1d = (w.astype(BF16) for w in (ffn1_gate, ffn1_up, ffn1_down))
    w2g, w2u, w2d = (w.astype(BF16) for w in (ffn2_gate, ffn2_up, ffn2_down))
    w_in_p = _take_cols(w_in, _w_in_columns()).astype(BF16)
    wuq_p = _take_cols(mla_wuq, _wuq_columns()).astype(BF16)
    wukv = mla_wukv.astype(BF16)
    w_out_b = w_out.astype(BF16)
    row3 = lambda g: g.reshape(nl, 1, -1)
    partner = jnp.asarray(_ROPE_PARTNER)
    g_qr = row3(jnp.concatenate([mla_qn_rope, mla_qn_rope[:, partner]], axis=-1))
    g_kr = row3(jnp.concatenate([mla_kn_rope, mla_kn_rope[:, partner]], axis=-1))
    g_q, g_kv, g_qn, g_kn = row3(mla_q_norm), row3(mla_kv_norm), row3(mla_qn_nope), row3(mla_kn_nope)
    g_attn, g_hy = row3(mla_out_norm), row3(hy_out_norm)
    gn_w, gn_b = row3(ret_gn_w), row3(ret_gn_b)
    hy_b = row3(hy_conv_b)
    hy_w1 = jnp.pad(hy_ffn_w1, ((0, 0), (0, LANES - HY_EMB), (0, 0)))
    hy_b1, hy_b2 = row3(hy_ffn_b1), row3(hy_ffn_b2)
    hy_skip4 = hy_skip.reshape(nl, HY_ORDER, 1, HY_WIDTH)
    dec = jnp.broadcast_to(ret_decay.astype(F32)[:, :, :, None, None], (nl, 2, RET_HEADS, 8, LANES))

    tab_x = _rope_table()
    tab_c = jnp.concatenate([jnp.ones((CTX_LEN, MLA_ROPE), F32), jnp.zeros((CTX_LEN, MLA_ROPE), F32)], -1)
    deltas = jnp.abs(jnp.linspace(HY_MIN_DECAY, HY_MAX_DECAY, HY_WIDTH, dtype=F32))[None, :]
    mats = {n: _dft_mats(n) for n in (SEQ, CTX_LEN)}
    feats = {n: _hy_features(n) for n in (SEQ, CTX_LEN)}

    cond = jnp.zeros((16, d), F32).at[0].set(c_ctx).at[1:1 + BATCH].set(c)
    mods = _modulation(cond, ada_w, ada_b).reshape(nl, 16, N_MOD, d)

    sx = x.reshape(BATCH * SEQ, d)
    sc = ctx.reshape(BATCH * CTX_LEN, d)

    def ffn(s, n, mod, wg, wu, wd, l):
        h = _normmod(s, mod[0], mod[1], n)
        u = _ffn_up(h, wg, wu, l)
        return _matmul_residual([u], wd, l, s, mod[2], n, 0.5)

    def hyena(z, l, n, spec):
        cv = _short_conv(z, l, n, hy_conv_w, hy_b)
        y1 = _long_conv(cv, 2, 0, mats[n][0], spec, 0, hy_skip4, g_hy, l, n, False)
        return _long_conv((y1, cv), 0, 1, mats[n][0], spec, 1, hy_skip4, g_hy, l, n, True)

    for l in range(nl):
        need_ctx_out = l < nl - 1
        mod_x = [mods[l, 1:1 + BATCH, j][:, None, :] for j in range(N_MOD)]
        mod_c = [mods[l, 0:1, j][:, None, :] for j in range(N_MOD)]

        sx = ffn(sx, SEQ, mod_x[0:3], w1g, w1u, w1d, l)
        sc = ffn(sc, CTX_LEN, mod_c[0:3], w1g, w1u, w1d, l)

        zc = _matmul(_normmod(sc, mod_c[3], mod_c[4], CTX_LEN), w_in_p, l)
        zx = _matmul(_normmod(sx, mod_x[3], mod_x[4], SEQ), w_in_p, l)

        kc, vc = _kv_proj(zc, l, CTX_LEN, wukv, g_kv, g_kn, g_kr, tab_c)
        kx, vx = _kv_proj(zx, l, SEQ, wukv, g_kv, g_kn, g_kr, tab_x)
        zero_state = jnp.zeros((BATCH, RET_HEADS, 2, LANES, RET_DV), F32)
        ret_c, state_c = _retention(zc, l, CTX_LEN, dec, zero_state, gn_w, gn_b)

        filt = {}
        for n in ((SEQ, CTX_LEN) if need_ctx_out else (SEQ,)):
            s_f, d_f = _hy_filter(feats[n], hy_w1, hy_b1, hy_ffn_w2, hy_b2, hy_ffn_w3, deltas, l)
            (fc, fs, _, _), alt8 = mats[n]
            filt[n] = _hy_spectrum(fc, fs, alt8, s_f, d_f)

        qx = _q_proj(zx, l, SEQ, wuq_p, g_q, g_qn, g_qr, tab_x)
        attn_x = _attention(qx, [kc, kx], [vc, vx], [CTX_LEN, SEQ], l, g_attn)
        hy_x = hyena(zx, l, SEQ, filt[SEQ])
        ret_x, _ = _retention(zx, l, SEQ, dec, state_c, gn_w, gn_b)
        sx = _matmul_residual([attn_x, hy_x, ret_x], w_out_b, l, sx, mod_x[5], SEQ, 1.0)
        sx = ffn(sx, SEQ, mod_x[6:9], w2g, w2u, w2d, l)

        if need_ctx_out:
            qc = _q_proj(zc, l, CTX_LEN, wuq_p, g_q, g_qn, g_qr, tab_c)
            attn_c = _attention(qc, [kc], [vc], [CTX_LEN], l, g_attn)
            hy_c = hyena(zc, l, CTX_LEN, filt[CTX_LEN])
            sc = _matmul_residual([attn_c, hy_c, ret_c], w_out_b, l, sc, mod_c[5], CTX_LEN, 1.0)
            sc = ffn(sc, CTX_LEN, mod_c[6:9], w2g, w2u, w2d, l)

    return sx.reshape(BATCH, SEQ, d)
```

```python
import functools
import math

import jax
import jax.numpy as jnp
import numpy as np
from jax import lax
from jax.experimental import pallas as pl
from jax.experimental.pallas import tpu as pltpu

F32 = jnp.float32
BF16 = jnp.bfloat16

D_MODEL = 2048
BATCH = 8
SEQ = 2048
DEPTH = 2
GRID_W = 64
CTX_LEN = 256
EPS = 1e-6
N_MOD = 9
FFN_HIDDEN = 5632

MLA_HEADS = 8
MLA_NOPE = 128
MLA_ROPE = 64
MLA_V = 128
MLA_Q_LORA = 512
MLA_KV_LORA = 256
MLA_WIDTH = MLA_HEADS * MLA_V
ROPE_BASE = 10000.0

HY_WIDTH = 512
HY_ORDER = 2
HY_EMB = 33
HY_FFN = 64
HY_SIN_FREQ = 1.0
HY_MIN_DECAY = math.log(1e-2) / 1.5
HY_MAX_DECAY = math.log(1e-2) / 0.3
HY_WINDOW_SHIFT = 0.05

RET_HEADS = 4
RET_DK = 64
RET_DV = 128
RET_WIDTH = RET_HEADS * RET_DV
RET_CHUNK = 128

VMEM_LIMIT_BYTES = 60 * 1024 * 1024
LANES = 128
HEAD_SLOT = 256

Z_HY = 0
Z_RV = 1536
Z_QLAT = 2048
Z_RGATE = 2560
Z_RQ = 3072
Z_RK = 3584
Z_KVLAT = 4096
Z_KR = 4352
Z_COLS = 4608

TM = 1024
TN = 512


def _cparams(*sem):
    return pltpu.CompilerParams(dimension_semantics=sem, vmem_limit_bytes=VMEM_LIMIT_BYTES)


def _rms(x, width=None):
    width = x.shape[-1] if width is None else width
    return x * lax.rsqrt(jnp.sum(x * x, axis=-1, keepdims=True) / width + EPS)


def _silu(x):
    return x * jax.nn.sigmoid(x)


def _mod_kernel(c_ref, w_ref, b_ref, o_ref):
    a = _silu(c_ref[...])
    o_ref[...] = jnp.dot(a, w_ref[...], preferred_element_type=F32,
                         precision=lax.Precision.HIGHEST) + b_ref[...]


def _modulation(cond, ada_w, ada_b):
    nl, d, nout = ada_w.shape
    bn = 1024
    return pl.pallas_call(
        _mod_kernel,
        grid=(nl, nout // bn),
        in_specs=[pl.BlockSpec((16, d), lambda l, j: (0, 0)),
                  pl.BlockSpec((None, d, bn), lambda l, j: (l, 0, j)),
                  pl.BlockSpec((None, 1, bn), lambda l, j: (l, 0, j))],
        out_specs=pl.BlockSpec((None, 16, bn), lambda l, j: (l, 0, j)),
        out_shape=jax.ShapeDtypeStruct((nl, 16, nout), F32),
        compiler_params=_cparams("arbitrary", "arbitrary"),
        name="modulation",
    )(cond, ada_w, ada_b.reshape(nl, 1, nout))


def _normmod_kernel(x_ref, sh_ref, sc_ref, o_ref):
    y = _rms(x_ref[...])
    o_ref[...] = (y * (1.0 + sc_ref[...]) + sh_ref[...]).astype(BF16)


def _normmod(s, shift, scale, n):
    r, d = s.shape
    tm = min(512, n)
    per_batch = shift.shape[0] > 1
    mod_spec = pl.BlockSpec((None, 1, d), lambda i: ((i * tm) // n if per_batch else 0, 0, 0))
    return pl.pallas_call(
        _normmod_kernel,
        grid=(r // tm,),
        in_specs=[pl.BlockSpec((tm, d), lambda i: (i, 0)), mod_spec, mod_spec],
        out_specs=pl.BlockSpec((tm, d), lambda i: (i, 0)),
        out_shape=jax.ShapeDtypeStruct((r, d), BF16),
        compiler_params=_cparams("arbitrary"),
        name="normmod",
    )(s, shift, scale)


def _ffn_up_kernel(h_ref, wg_ref, wu_ref, o_ref, wg_bf, wu_bf):
    @pl.when(pl.program_id(1) == 0)
    def _():
        wg_bf[...] = wg_ref[...].astype(BF16)
        wu_bf[...] = wu_ref[...].astype(BF16)

    h = h_ref[...]
    g = jnp.dot(h, wg_bf[...], preferred_element_type=F32)
    u = jnp.dot(h, wu_bf[...], preferred_element_type=F32)
    o_ref[...] = (_silu(g) * u).astype(BF16)


def _ffn_up(h, wg, wu, l):
    r, d = h.shape
    f = wg.shape[-1]
    w_spec = pl.BlockSpec((None, d, TN), lambda j, i: (l, 0, j))
    return pl.pallas_call(
        _ffn_up_kernel,
        grid=(f // TN, r // TM),
        in_specs=[pl.BlockSpec((TM, d), lambda j, i: (i, 0)), w_spec, w_spec],
        out_specs=pl.BlockSpec((TM, TN), lambda j, i: (i, j)),
        out_shape=jax.ShapeDtypeStruct((r, f), BF16),
        scratch_shapes=[pltpu.VMEM((d, TN), BF16), pltpu.VMEM((d, TN), BF16)],
        compiler_params=_cparams("arbitrary", "arbitrary"),
        name="ffn_up",
    )(h, wg, wu)


def _mm_kernel(a_ref, w_ref, o_ref):
    o_ref[...] = jnp.dot(a_ref[...], w_ref[...], preferred_element_type=F32).astype(o_ref.dtype)


def _matmul(a, w, l):
    r, k = a.shape
    nout = w.shape[-1]
    return pl.pallas_call(
        _mm_kernel,
        grid=(r // TM, nout // TN),
        in_specs=[pl.BlockSpec((TM, k), lambda i, j: (i, 0)),
                  pl.BlockSpec((None, k, TN), lambda i, j: (l, 0, j))],
        out_specs=pl.BlockSpec((TM, TN), lambda i, j: (i, j)),
        out_shape=jax.ShapeDtypeStruct((r, nout), BF16),
        compiler_params=_cparams("arbitrary", "arbitrary"),
        name="matmul",
    )(a, w)


def _mm_res_kernel(*refs, n_in, coeff):
    a_refs, w_refs = refs[:n_in], refs[n_in:2 * n_in]
    s_ref, g_ref, o_ref = refs[2 * n_in:2 * n_in + 3]
    w_bfs = refs[2 * n_in + 3:]

    @pl.when(pl.program_id(1) == 0)
    def _():
        for w_ref, w_bf in zip(w_refs, w_bfs):
            w_bf[...] = w_ref[...].astype(BF16)

    acc = None
    for a_ref, w_bf in zip(a_refs, w_bfs):
        part = jnp.dot(a_ref[...], w_bf[...], preferred_element_type=F32)
        acc = part if acc is None else acc + part
    o_ref[...] = s_ref[...] + (coeff * g_ref[...]) * acc


def _matmul_residual(parts, w, l, s, gate, n, coeff):
    r, d = s.shape
    tm = TM if w.shape[1] <= 4096 else TM // 2
    in_specs, w_specs, scratch, row = [], [], [], 0
    for p in parts:
        kp = p.shape[1]
        assert row % kp == 0
        in_specs.append(pl.BlockSpec((tm, kp), lambda j, i: (i, 0)))
        w_specs.append(pl.BlockSpec((None, kp, TN), functools.partial(
            lambda j, i, rb: (l, rb, j), rb=row // kp)))
        scratch.append(pltpu.VMEM((kp, TN), BF16))
        row += kp
    assert row == w.shape[1]
    per_batch = gate.shape[0] > 1
    return pl.pallas_call(
        functools.partial(_mm_res_kernel, n_in=len(parts), coeff=coeff),
        grid=(d // TN, r // tm),
        in_specs=in_specs + w_specs + [
            pl.BlockSpec((tm, TN), lambda j, i: (i, j)),
            pl.BlockSpec((None, 1, TN), lambda j, i: ((i * tm) // n if per_batch else 0, 0, j))],
        out_specs=pl.BlockSpec((tm, TN), lambda j, i: (i, j)),
        out_shape=jax.ShapeDtypeStruct((r, d), F32),
        scratch_shapes=scratch,
        compiler_params=_cparams("arbitrary", "arbitrary"),
        name="matmul_residual",
    )(*parts, *([w] * len(parts)), s, gate)


def _rope_lanes(x):
    lane = lax.broadcasted_iota(jnp.int32, x.shape, 1)
    ms = jnp.sum(jnp.where(lane < MLA_ROPE, x * x, 0.0), axis=-1, keepdims=True) / MLA_ROPE
    return x * lax.rsqrt(ms + EPS)


def _kv_kernel(lat_ref, kr_ref, gkv_ref, w_ref, gkn_ref, gkr_ref, tab_ref, k_ref, v_ref):
    a = _rms(lat_ref[...].astype(F32)) * gkv_ref[...]
    kv = jnp.dot(a.astype(BF16), w_ref[...], preferred_element_type=F32)
    t = _rope_lanes(kr_ref[...].astype(F32)) * gkr_ref[...] * tab_ref[...]
    krot = (t + pltpu.roll(t, MLA_ROPE, 1)).astype(BF16)
    for h in range(MLA_HEADS):
        c0 = h * HEAD_SLOT
        kn = _rms(kv[:, c0:c0 + MLA_NOPE]) * gkn_ref[...]
        k_ref[:, c0:c0 + MLA_NOPE] = kn.astype(BF16)
        k_ref[:, c0 + MLA_NOPE:c0 + HEAD_SLOT] = krot
        v_ref[:, c0:c0 + MLA_V] = kv[:, c0 + MLA_NOPE:c0 + HEAD_SLOT].astype(BF16)
        v_ref[:, c0 + MLA_V:c0 + HEAD_SLOT] = jnp.ones((kv.shape[0], HEAD_SLOT - MLA_V), BF16)


def _kv_proj(z, l, n, wukv, g_kv, g_kn, g_kr, tab):
    r = z.shape[0]
    tm = min(512, n)
    nt = tab.shape[0] // tm
    vec = lambda w: pl.BlockSpec((None, 1, w), lambda i: (l, 0, 0))
    return pl.pallas_call(
        _kv_kernel,
        grid=(r // tm,),
        in_specs=[pl.BlockSpec((tm, MLA_KV_LORA), lambda i: (i, Z_KVLAT // MLA_KV_LORA)),
                  pl.BlockSpec((tm, LANES), lambda i: (i, Z_KR // LANES)),
                  vec(MLA_KV_LORA),
                  pl.BlockSpec((None, MLA_KV_LORA, MLA_HEADS * HEAD_SLOT), lambda i: (l, 0, 0)),
                  vec(MLA_NOPE), vec(LANES),
                  pl.BlockSpec((tm, LANES), lambda i: (i % nt, 0))],
        out_specs=[pl.BlockSpec((tm, MLA_HEADS * HEAD_SLOT), lambda i: (i, 0)),
                   pl.BlockSpec((tm, MLA_HEADS * HEAD_SLOT), lambda i: (i, 0))],
        out_shape=[jax.ShapeDtypeStruct((r, MLA_HEADS * HEAD_SLOT), BF16),
                   jax.ShapeDtypeStruct((r, MLA_HEADS * HEAD_SLOT), BF16)],
        compiler_params=_cparams("arbitrary"),
        name="kv_proj",
    )(z, z, g_kv, wukv, g_kn, g_kr, tab)


def _q_kernel(lat_ref, gq_ref, w_ref, gn_ref, gr_ref, tab_ref, q_ref, *, scale):
    a = _rms(lat_ref[...].astype(F32)) * gq_ref[...]
    q = jnp.dot(a.astype(BF16), w_ref[...], preferred_element_type=F32)
    rope_scale = gr_ref[...] * tab_ref[...] * scale
    for h in range(MLA_HEADS):
        c0 = h * HEAD_SLOT
        qn = _rms(q[:, c0:c0 + MLA_NOPE]) * (gn_ref[...] * scale)
        q_ref[:, c0:c0 + MLA_NOPE] = qn.astype(BF16)
        qr = _rope_lanes(q[:, c0 + MLA_NOPE:c0 + HEAD_SLOT]) * rope_scale
        q_ref[:, c0 + MLA_NOPE:c0 + HEAD_SLOT] = qr.astype(BF16)


def _q_proj(z, l, n, wuq, g_q, g_qn, g_qr, tab):
    r = z.shape[0]
    tm = min(512, n)
    nt = tab.shape[0] // tm
    vec = lambda w: pl.BlockSpec((None, 1, w), lambda i: (l, 0, 0))
    return pl.pallas_call(
        functools.partial(_q_kernel, scale=(MLA_NOPE + MLA_ROPE) ** -0.5 * math.log2(math.e)),
        grid=(r // tm,),
        in_specs=[pl.BlockSpec((tm, MLA_Q_LORA), lambda i: (i, Z_QLAT // MLA_Q_LORA)),
                  vec(MLA_Q_LORA),
                  pl.BlockSpec((None, MLA_Q_LORA, MLA_HEADS * HEAD_SLOT), lambda i: (l, 0, 0)),
                  vec(MLA_NOPE), vec(LANES),
                  pl.BlockSpec((tm, LANES), lambda i: (i % nt, 0))],
        out_specs=pl.BlockSpec((tm, MLA_HEADS * HEAD_SLOT), lambda i: (i, 0)),
        out_shape=jax.ShapeDtypeStruct((r, MLA_HEADS * HEAD_SLOT), BF16),
        compiler_params=_cparams("arbitrary"),
        name="q_proj",
    )(z, g_q, wuq, g_qn, g_qr, tab)


def _attn_kernel(*refs, n_src):
    q_ref = refs[0]
    k_refs, v_refs = refs[1:1 + n_src], refs[1 + n_src:1 + 2 * n_src]
    g_ref, o_ref, acc_ref = refs[1 + 2 * n_src:]
    ssq = None
    for h in range(MLA_HEADS):
        q = q_ref[:, h * HEAD_SLOT:(h + 1) * HEAD_SLOT]
        hs = slice(h * HEAD_SLOT, (h + 1) * HEAD_SLOT)
        ss = [lax.dot_general(q, k_ref[:, hs], (((1,), (1,)), ((), ())),
                              preferred_element_type=F32) for k_ref in k_refs]
        m = functools.reduce(jnp.maximum, [jnp.max(s, axis=-1, keepdims=True) for s in ss])
        o = functools.reduce(jnp.add, [
            jnp.dot(jnp.exp2((s - m).astype(BF16)), v_ref[:, hs], preferred_element_type=F32)
            for s, v_ref in zip(ss, v_refs)])
        o = o[:, :MLA_V] / o[:, MLA_V:MLA_V + 1]
        acc_ref[:, h * MLA_V:(h + 1) * MLA_V] = o
        sq = jnp.sum(o * o, axis=-1, keepdims=True)
        ssq = sq if ssq is None else ssq + sq
    o_ref[...] = (acc_ref[...] * lax.rsqrt(ssq / MLA_WIDTH + EPS) * g_ref[...]).astype(BF16)


def _attention(q, ks, vs, ns, l, g_out):
    r = q.shape[0]
    nq = r // BATCH
    tq = 256
    per_b = nq // tq
    k_specs = [pl.BlockSpec((nk, MLA_HEADS * HEAD_SLOT), lambda b, i: (b, 0)) for nk in ns]
    v_specs = k_specs
    return pl.pallas_call(
        functools.partial(_attn_kernel, n_src=len(ks)),
        grid=(BATCH, per_b),
        in_specs=[pl.BlockSpec((tq, MLA_HEADS * HEAD_SLOT), lambda b, i: (b * per_b + i, 0))]
        + k_specs + v_specs
        + [pl.BlockSpec((None, 1, MLA_WIDTH), lambda b, i: (l, 0, 0))],
        out_specs=pl.BlockSpec((tq, MLA_WIDTH), lambda b, i: (b * per_b + i, 0)),
        out_shape=jax.ShapeDtypeStruct((r, MLA_WIDTH), BF16),
        scratch_shapes=[pltpu.VMEM((tq, MLA_WIDTH), F32)],
        compiler_params=_cparams("arbitrary", "arbitrary"),
        name="attention",
    )(q, *ks, *vs, g_out)


def _log_sigmoid(x):
    return jnp.minimum(x, 0.0) - jnp.log(1.0 + jnp.exp(-jnp.abs(x)))


def _ret_kernel(q_ref, k_ref, v_ref, gate_ref, dec_ref, s0_ref, gnw_ref, gnb_ref,
                o_ref, st_ref, acc_ref, tab_ref, *, n):
    nc = n // RET_CHUNK
    c = RET_CHUNK

    @pl.when(pl.program_id(0) == 0)
    def _():
        ii = lax.broadcasted_iota(jnp.int32, (c, c), 0).astype(F32)
        jj = lax.broadcasted_iota(jnp.int32, (c, c), 1).astype(F32)
        for h in range(RET_HEADS):
            for d in range(2):
                lg = _log_sigmoid(dec_ref[d, h][0:1, :])
                lgm = jnp.broadcast_to(lg, (c, c))
                if d == 0:
                    diff, qpow, kpow = ii - jj, ii + 1.0, (c - 1.0) - ii
                else:
                    diff, qpow, kpow = jj - ii, c - ii, ii
                tab_ref[h, d, 0] = jnp.where(diff >= 0, jnp.exp(lgm * jnp.maximum(diff, 0.0)), 0.0)
                tab_ref[h, d, 1] = jnp.exp(lgm * qpow)
                tab_ref[h, d, 2] = jnp.exp(lgm * kpow)
                tab_ref[h, d, 3] = jnp.exp(lgm * float(c))

    st_ref[...] = s0_ref[...]

    def body(ci, carry):
        for h in range(RET_HEADS):
            hs = slice(h * LANES, (h + 1) * LANES)
            for d in range(2):
                cc = ci if d == 0 else nc - 1 - ci
                r0 = pl.multiple_of(cc * c, c)
                q = q_ref[pl.ds(r0, c), hs]
                k = k_ref[pl.ds(r0, c), hs].astype(F32) * (RET_DK ** -0.5)
                v = v_ref[pl.ds(r0, c), hs]
                state = st_ref[h, d]
                sc = lax.dot_general(q, k.astype(BF16), (((1,), (1,)), ((), ())),
                                     preferred_element_type=F32) * tab_ref[h, d, 0]
                o = jnp.dot(sc.astype(BF16), v, preferred_element_type=F32)
                o = o + jnp.dot(q, state.astype(BF16), preferred_element_type=F32) * tab_ref[h, d, 1]
                kd = (k * tab_ref[h, d, 2]).astype(BF16)
                st_ref[h, d] = tab_ref[h, d, 3] * state + lax.dot_general(
                    kd, v, (((0,), (0,)), ((), ())), preferred_element_type=F32)
                acc_ref[d, pl.ds(r0, c), hs] = o
        return carry

    lax.fori_loop(0, nc, body, 0)

    for h in range(RET_HEADS):
        hs = slice(h * LANES, (h + 1) * LANES)
        o = acc_ref[0, :, hs] + acc_ref[1, :, hs]
        mu = jnp.mean(o, axis=-1, keepdims=True)
        var = jnp.mean(jnp.square(o - mu), axis=-1, keepdims=True)
        on = (o - mu) * lax.rsqrt(var + EPS) * gnw_ref[:, hs] + gnb_ref[:, hs]
        o_ref[:, hs] = (_silu(gate_ref[:, hs].astype(F32)) * on).astype(BF16)


def _retention(z, l, n, dec, s0, gn_w, gn_b):
    r = z.shape[0]
    w = RET_WIDTH
    col = lambda base: (lambda b: (b, base // w))
    st_spec = pl.BlockSpec((None, RET_HEADS, 2, LANES, RET_DV), lambda b: (b, 0, 0, 0, 0))
    gn_spec = pl.BlockSpec((None, 1, w), lambda b: (l, 0, 0))
    return pl.pallas_call(
        functools.partial(_ret_kernel, n=n),
        grid=(BATCH,),
        in_specs=[pl.BlockSpec((n, w), col(Z_RQ)),
                  pl.BlockSpec((n, w), col(Z_RK)),
                  pl.BlockSpec((n, w), col(Z_RV)),
                  pl.BlockSpec((n, w), col(Z_RGATE)),
                  pl.BlockSpec((None, 2, RET_HEADS, 8, LANES), lambda b: (l, 0, 0, 0, 0)),
                  st_spec, gn_spec, gn_spec],
        out_specs=[pl.BlockSpec((n, w), lambda b: (b, 0)), st_spec],
        out_shape=[jax.ShapeDtypeStruct((r, w), BF16),
                   jax.ShapeDtypeStruct((BATCH, RET_HEADS, 2, LANES, RET_DV), F32)],
        scratch_shapes=[pltpu.VMEM((2, n, w), F32),
                        pltpu.VMEM((RET_HEADS, 2, 4, RET_CHUNK, RET_CHUNK), F32)],
        compiler_params=_cparams("arbitrary"),
        name="retention",
    )(z, z, z, z, dec, s0, gn_w, gn_b)


def _hy_filter_kernel(feat_ref, w1_ref, b1_ref, w2_ref, b2_ref, w3_ref, dl_ref, s_ref, d_ref):
    hi = lax.Precision.HIGHEST
    feats = feat_ref[...]
    h = jnp.sin(HY_SIN_FREQ * (jnp.dot(feats, w1_ref[...], preferred_element_type=F32, precision=hi)
                               + b1_ref[...]))
    h = jnp.sin(HY_SIN_FREQ * (jnp.dot(h, w2_ref[...], preferred_element_type=F32, precision=hi)
                               + b2_ref[...]))
    h = jnp.dot(h, w3_ref[...], preferred_element_type=F32, precision=hi)
    window = jnp.exp(-feats[:, 0:1] * dl_ref[...]) + HY_WINDOW_SHIFT
    window = jnp.concatenate([window] * HY_ORDER, axis=-1)
    half = HY_ORDER * HY_WIDTH
    h_fwd = h[:, :half] * window
    h_bwd = h[:, half:] * window
    s_ref[...] = (h_fwd + h_bwd).astype(BF16)
    d_ref[...] = (h_fwd - h_bwd).astype(BF16)


def _hy_filter(feats, w1, b1, w2, b2, w3, deltas, l):
    n = feats.shape[0]
    tn = 256
    half = HY_ORDER * HY_WIDTH
    full = lambda a, b: pl.BlockSpec((None, a, b), lambda i: (l, 0, 0))
    return pl.pallas_call(
        _hy_filter_kernel,
        grid=(n // tn,),
        in_specs=[pl.BlockSpec((tn, LANES), lambda i: (i, 0)),
                  full(LANES, HY_FFN), full(1, HY_FFN), full(HY_FFN, HY_FFN), full(1, HY_FFN),
                  full(HY_FFN, 2 * half),
                  pl.BlockSpec((1, HY_WIDTH), lambda i: (0, 0))],
        out_specs=[pl.BlockSpec((tn, half), lambda i: (i, 0))] * 2,
        out_shape=[jax.ShapeDtypeStruct((n, half), BF16)] * 2,
        compiler_params=_cparams("arbitrary"),
        name="hyena_filter",
    )(feats, w1, b1, w2, b2, w3, deltas)


def _hy_spec_kernel(fc_ref, fs_ref, alt_ref, s_ref, d_ref, hc_ref, hs_ref, hn_ref):
    s = s_ref[...]
    tk, n = fc_ref.shape
    row = lax.broadcasted_iota(jnp.int32, (tk, s.shape[1]), 0) + pl.program_id(0) * tk
    wk = jnp.where(row == 0, 1.0, 2.0) * (1.0 / (2 * n))
    hc_ref[...] = jnp.dot(fc_ref[...], s, preferred_element_type=F32) * wk
    hs_ref[...] = jnp.dot(fs_ref[...], d_ref[...], preferred_element_type=F32) * wk
    hn_ref[...] = jnp.dot(alt_ref[...], s, preferred_element_type=F32)[0:8, :] * (1.0 / (2 * n))


def _hy_spectrum(fc, fs, alt, s, d):
    n = fc.shape[0]
    tk = min(512, n)
    half = s.shape[1]
    out = pl.BlockSpec((tk, half), lambda i: (i, 0))
    return pl.pallas_call(
        _hy_spec_kernel,
        grid=(n // tk,),
        in_specs=[pl.BlockSpec((tk, n), lambda i: (i, 0)),
                  pl.BlockSpec((tk, n), lambda i: (i, 0)),
                  pl.BlockSpec((16, n), lambda i: (0, 0)),
                  pl.BlockSpec((n, half), lambda i: (0, 0)),
                  pl.BlockSpec((n, half), lambda i: (0, 0))],
        out_specs=[out, out, pl.BlockSpec((8, half), lambda i: (0, 0))],
        out_shape=[jax.ShapeDtypeStruct((n, half), F32)] * 2
        + [jax.ShapeDtypeStruct((8, half), F32)],
        compiler_params=_cparams("arbitrary"),
        name="hyena_spectrum",
    )(fc, fs, alt, s, d)


def _short_conv_kernel(u_ref, w_ref, b_ref, o_ref, pad_ref, *, n):
    zeros = jnp.zeros((8, pad_ref.shape[1]), F32)
    pad_ref[0:8, :] = zeros
    pad_ref[8 + n:16 + n, :] = zeros
    pad_ref[8:8 + n, :] = u_ref[...].astype(F32)
    w = w_ref[...]
    y = (pad_ref[7:7 + n, :] * w[0:1, :] + pad_ref[8:8 + n, :] * w[1:2, :]
         + pad_ref[9:9 + n, :] * w[2:3, :] + b_ref[...])
    o_ref[...] = y.astype(BF16)


def _short_conv(z, l, n, w, b):
    r = z.shape[0]
    wc = HY_WIDTH
    return pl.pallas_call(
        functools.partial(_short_conv_kernel, n=n),
        grid=(BATCH, (HY_ORDER + 1)),
        in_specs=[pl.BlockSpec((n, wc), lambda bi, j: (bi, Z_HY // wc + j)),
                  pl.BlockSpec((None, 3, wc), lambda bi, j: (l, 0, j)),
                  pl.BlockSpec((None, 1, wc), lambda bi, j: (l, 0, j))],
        out_specs=pl.BlockSpec((n, wc), lambda bi, j: (bi, j)),
        out_shape=jax.ShapeDtypeStruct((r, (HY_ORDER + 1) * wc), BF16),
        scratch_shapes=[pltpu.VMEM((n + 16, wc), F32)],
        compiler_params=_cparams("arbitrary", "arbitrary"),
        name="short_conv",
    )(z, w, b)


HY_PAIR = 2


def _long_conv_kernel(u_ref, gate_ref, fc_ref, fs_ref, g_ref, alt_ref, hc_ref, hs_ref, hn_ref,
                      skip_ref, gain_ref, o_ref, y_ref, nyq_ref, *, n, tk, final):
    j = pl.program_id(1)
    kt = n // tk

    @pl.when(j == 0)
    def _():
        for p in range(HY_PAIR):
            x_nyq = jnp.dot(alt_ref[...], u_ref[p], preferred_element_type=F32)
            nyq_ref[p] = x_nyq[0:8, :] * hn_ref[...]

    @pl.when(j < kt)
    def _():
        r0 = pl.multiple_of(j * tk, tk)
        hc, hs = hc_ref[...], hs_ref[...]
        for p in range(HY_PAIR):
            u = u_ref[p]
            xc = jnp.dot(fc_ref[...], u, preferred_element_type=F32)
            xs = jnp.dot(fs_ref[...], u, preferred_element_type=F32)
            y_ref[p, pl.ds(r0, tk), :] = (xc * hc - xs * hs).astype(BF16)
            y_ref[p, pl.ds(n + r0, tk), :] = (xc * hs + xs * hc).astype(BF16)

    @pl.when(j >= kt)
    def _():
        t0 = pl.multiple_of((j - kt) * tk, tk)
        t = lax.broadcasted_iota(jnp.int32, (tk, u_ref.shape[-1]), 0)
        alt = (1 - 2 * (t & 1)).astype(F32)
        for p in range(HY_PAIR):
            conv = jnp.dot(g_ref[...], y_ref[p], preferred_element_type=F32)
            u = u_ref[p, pl.ds(t0, tk), :].astype(F32)
            y = conv + alt * nyq_ref[p][0:1, :] + u * skip_ref[...]
            y = gate_ref[p].astype(F32) * y
            if final:
                y = _rms(y) * gain_ref[...]
            o_ref[p] = y.astype(BF16)


def _long_conv(src, u_col, gate_col, mats, spec, order, skip, gain, l, n, final):
    fc, fs, g, alt = mats
    hc, hs, hn = spec
    r = src[0].shape[0]
    wc = HY_WIDTH
    tk = min(512, n)
    kt = n // tk
    nb = BATCH // HY_PAIR
    u4, gate4 = (a.reshape(nb, HY_PAIR, n, a.shape[-1]) for a in src)
    f_spec = pl.BlockSpec((tk, n), lambda bi, j: (jnp.minimum(j, kt - 1), 0))
    h_spec = pl.BlockSpec((tk, wc), lambda bi, j: (jnp.minimum(j, kt - 1), order))
    tile = lambda col: pl.BlockSpec((None, HY_PAIR, tk, wc),
                                    lambda bi, j: (bi, 0, jnp.maximum(j - kt, 0), col))
    out = pl.pallas_call(
        functools.partial(_long_conv_kernel, n=n, tk=tk, final=final),
        grid=(nb, 2 * kt),
        in_specs=[pl.BlockSpec((None, HY_PAIR, n, wc), lambda bi, j: (bi, 0, 0, u_col)),
                  tile(gate_col),
                  f_spec, f_spec,
                  pl.BlockSpec((tk, 2 * n), lambda bi, j: (jnp.maximum(j - kt, 0), 0)),
                  pl.BlockSpec((16, n), lambda bi, j: (0, 0)),
                  h_spec, h_spec,
                  pl.BlockSpec((8, wc), lambda bi, j: (0, order)),
                  pl.BlockSpec((None, None, 1, wc), lambda bi, j: (l, order, 0, 0)),
                  pl.BlockSpec((None, 1, wc), lambda bi, j: (l, 0, 0))],
        out_specs=tile(0),
        out_shape=jax.ShapeDtypeStruct((nb, HY_PAIR, n, wc), BF16),
        scratch_shapes=[pltpu.VMEM((HY_PAIR, 2 * n, wc), BF16),
                        pltpu.VMEM((HY_PAIR, 8, wc), F32)],
        compiler_params=_cparams("arbitrary", "arbitrary"),
        name="long_conv",
    )(u4, gate4, fc, fs, g, alt, hc, hs, hn, skip, gain)
    return out.reshape(r, wc)


def _dft_mats(n):
    big = 2 * n
    k = jnp.arange(n, dtype=jnp.int32)
    ang = ((k[:, None] * k[None, :]) % big).astype(F32) * (2.0 * math.pi / big)
    fc, fs = jnp.cos(ang).astype(BF16), jnp.sin(ang).astype(BF16)
    alt = jnp.broadcast_to((1 - 2 * (k % 2)).astype(F32)[None, :], (16, n))
    return fc, fs, jnp.concatenate([fc, fs], axis=1), alt.astype(BF16)


def _hy_features(n):
    t = jnp.linspace(0.0, 1.0, n, dtype=F32)[:, None]
    bands = (HY_EMB - 1) // 2
    w = 2 * math.pi * jnp.arange(n, dtype=F32)[:, None] / n
    f = jnp.linspace(1e-4, bands - 1, bands, dtype=F32)[None, :]
    feats = jnp.concatenate([t, jnp.cos(f * w), -jnp.sin(f * w)], axis=-1)
    return jnp.pad(feats, ((0, 0), (0, LANES - HY_EMB)))


def _rope_table():
    n_freq = MLA_ROPE // 4
    inv_freq = ROPE_BASE ** (-jnp.arange(n_freq, dtype=F32) / n_freq)
    rows = SEQ // GRID_W
    row = jnp.repeat(jnp.arange(rows), GRID_W).astype(F32)
    col = jnp.tile(jnp.arange(GRID_W), rows).astype(F32)
    ar = row[:, None] * inv_freq[None, :]
    ac = col[:, None] * inv_freq[None, :]
    cos_part = jnp.concatenate([jnp.cos(ar), jnp.cos(ar), jnp.cos(ac), jnp.cos(ac)], axis=-1)
    sin_part = jnp.concatenate([-jnp.sin(ar), jnp.sin(ar), -jnp.sin(ac), jnp.sin(ac)], axis=-1)
    return jnp.concatenate([cos_part, sin_part], axis=-1)


_ROPE_PARTNER = np.concatenate([np.arange(16, 32), np.arange(0, 16),
                                np.arange(48, 64), np.arange(32, 48)])
_QUARTER = MLA_ROPE // 4


def _partner_pieces(base):
    return [(base + int(_ROPE_PARTNER[i]), _QUARTER) for i in range(0, MLA_ROPE, _QUARTER)]


def _w_in_pieces():
    off = np.cumsum([0, MLA_KV_LORA, MLA_ROPE, RET_HEADS * RET_DK, RET_WIDTH, MLA_Q_LORA,
                     RET_HEADS * RET_DK, RET_WIDTH])
    o_kv, o_kr, o_rk, o_rv, o_q, o_rq, o_gate, o_hy = [int(v) for v in off]
    pieces = [(o_hy, 3 * HY_WIDTH), (o_rv, RET_WIDTH), (o_q, MLA_Q_LORA), (o_gate, RET_WIDTH)]
    for base in (o_rq, o_rk):
        for h in range(RET_HEADS):
            pieces += [(base + h * RET_DK, RET_DK), (None, LANES - RET_DK)]
    pieces += [(o_kv, MLA_KV_LORA), (o_kr, MLA_ROPE)] + _partner_pieces(o_kr)
    pieces.append((None, Z_COLS - sum(w for _, w in pieces)))
    return pieces


def _wuq_pieces():
    per = MLA_NOPE + MLA_ROPE
    pieces = []
    for h in range(MLA_HEADS):
        pieces += [(h * per, per)] + _partner_pieces(h * per + MLA_NOPE)
    return pieces


def _cat_cols(w, pieces, dtype):
    cols = [jnp.zeros(w.shape[:-1] + (width,), dtype) if start is None
            else w[..., start:start + width].astype(dtype) for start, width in pieces]
    return jnp.concatenate(cols, axis=-1)


def kernel(x, c, ctx, c_ctx, ada_w, ada_b, ffn1_gate, ffn1_up, ffn1_down, w_in, mla_q_norm, mla_wuq, mla_kv_norm, mla_wukv, mla_qn_nope, mla_qn_rope, mla_kn_nope, mla_kn_rope, mla_out_norm, hy_conv_w, hy_conv_b, hy_ffn_w1, hy_ffn_b1, hy_ffn_w2, hy_ffn_b2, hy_ffn_w3, hy_skip, hy_out_norm, ret_decay, ret_gn_w, ret_gn_b, w_out, ffn2_gate, ffn2_up, ffn2_down):
    nl = DEPTH
    d = D_MODEL

    w1g, w1u, w1d = ffn1_gate, ffn1_up, ffn1_down
    w2g, w2u, w2d = ffn2_gate, ffn2_up, ffn2_down
    w_in_p = _cat_cols(w_in, _w_in_pieces(), BF16)
    wuq_p = _cat_cols(mla_wuq, _wuq_pieces(), BF16)
    wukv = mla_wukv.astype(BF16)
    w_out_b = w_out
    row3 = lambda g: g.reshape(nl, 1, -1)
    partner = jnp.asarray(_ROPE_PARTNER)
    g_qr = row3(jnp.concatenate([mla_qn_rope, mla_qn_rope[:, partner]], axis=-1))
    g_kr = row3(jnp.concatenate([mla_kn_rope, mla_kn_rope[:, partner]], axis=-1))
    g_q, g_kv, g_qn, g_kn = row3(mla_q_norm), row3(mla_kv_norm), row3(mla_qn_nope), row3(mla_kn_nope)
    g_attn, g_hy = row3(mla_out_norm), row3(hy_out_norm)
    gn_w, gn_b = row3(ret_gn_w), row3(ret_gn_b)
    hy_b = row3(hy_conv_b)
    hy_w1 = jnp.pad(hy_ffn_w1, ((0, 0), (0, LANES - HY_EMB), (0, 0)))
    hy_b1, hy_b2 = row3(hy_ffn_b1), row3(hy_ffn_b2)
    hy_skip4 = hy_skip.reshape(nl, HY_ORDER, 1, HY_WIDTH)
    dec = jnp.broadcast_to(ret_decay.astype(F32)[:, :, :, None, None], (nl, 2, RET_HEADS, 8, LANES))

    tab_x = _rope_table()
    tab_c = jnp.concatenate([jnp.ones((CTX_LEN, MLA_ROPE), F32), jnp.zeros((CTX_LEN, MLA_ROPE), F32)], -1)
    deltas = jnp.abs(jnp.linspace(HY_MIN_DECAY, HY_MAX_DECAY, HY_WIDTH, dtype=F32))[None, :]
    mats = {n: _dft_mats(n) for n in (SEQ, CTX_LEN)}
    feats = {n: _hy_features(n) for n in (SEQ, CTX_LEN)}

    cond = jnp.zeros((16, d), F32).at[0].set(c_ctx).at[1:1 + BATCH].set(c)
    mods = _modulation(cond, ada_w, ada_b).reshape(nl, 16, N_MOD, d)

    sx = x.reshape(BATCH * SEQ, d)
    sc = ctx.reshape(BATCH * CTX_LEN, d)

    def ffn(s, n, mod, wg, wu, wd, l):
        h = _normmod(s, mod[0], mod[1], n)
        u = _ffn_up(h, wg, wu, l)
        return _matmul_residual([u], wd, l, s, mod[2], n, 0.5)

    def hyena(z, l, n, spec):
        cv = _short_conv(z, l, n, hy_conv_w, hy_b)
        y1 = _long_conv((cv, cv), 2, 0, mats[n], spec, 0, hy_skip4, g_hy, l, n, False)
        return _long_conv((y1, cv), 0, 1, mats[n], spec, 1, hy_skip4, g_hy, l, n, True)

    for l in range(nl):
        need_ctx_out = l < nl - 1
        mod_x = [mods[l, 1:1 + BATCH, j][:, None, :] for j in range(N_MOD)]
        mod_c = [mods[l, 0:1, j][:, None, :] for j in range(N_MOD)]

        sx = ffn(sx, SEQ, mod_x[0:3], w1g, w1u, w1d, l)
        sc = ffn(sc, CTX_LEN, mod_c[0:3], w1g, w1u, w1d, l)

        zc = _matmul(_normmod(sc, mod_c[3], mod_c[4], CTX_LEN), w_in_p, l)
        zx = _matmul(_normmod(sx, mod_x[3], mod_x[4], SEQ), w_in_p, l)

        kc, vc = _kv_proj(zc, l, CTX_LEN, wukv, g_kv, g_kn, g_kr, tab_c)
        kx, vx = _kv_proj(zx, l, SEQ, wukv, g_kv, g_kn, g_kr, tab_x)
        zero_state = jnp.zeros((BATCH, RET_HEADS, 2, LANES, RET_DV), F32)
        ret_c, state_c = _retention(zc, l, CTX_LEN, dec, zero_state, gn_w, gn_b)

        filt = {}
        for n in ((SEQ, CTX_LEN) if need_ctx_out else (SEQ,)):
            s_f, d_f = _hy_filter(feats[n], hy_w1, hy_b1, hy_ffn_w2, hy_b2, hy_ffn_w3, deltas, l)
            fc, fs, _, alt = mats[n]
            filt[n] = _hy_spectrum(fc, fs, alt, s_f, d_f)

        qx = _q_proj(zx, l, SEQ, wuq_p, g_q, g_qn, g_qr, tab_x)
        attn_x = _attention(qx, [kc, kx], [vc, vx], [CTX_LEN, SEQ], l, g_attn)
        hy_x = hyena(zx, l, SEQ, filt[SEQ])
        ret_x, _ = _retention(zx, l, SEQ, dec, state_c, gn_w, gn_b)
        sx = _matmul_residual([attn_x, hy_x, ret_x], w_out_b, l, sx, mod_x[5], SEQ, 1.0)
        sx = ffn(sx, SEQ, mod_x[6:9], w2g, w2u, w2d, l)

        if need_ctx_out:
            qc = _q_proj(zc, l, CTX_LEN, wuq_p, g_q, g_qn, g_qr, tab_c)
            attn_c = _attention(qc, [kc], [vc], [CTX_LEN], l, g_attn)
            hy_c = hyena(zc, l, CTX_LEN, filt[CTX_LEN])
            sc = _matmul_residual([attn_c, hy_c, ret_c], w_out_b, l, sc, mod_c[5], CTX_LEN, 1.0)
            sc = ffn(sc, CTX_LEN, mod_c[6:9], w2g, w2u, w2d, l)

    return sx.reshape(BATCH, SEQ, d)
```

```python
import functools
import math

import jax
import jax.numpy as jnp
import numpy as np
from jax import lax
from jax.experimental import pallas as pl
from jax.experimental.pallas import tpu as pltpu

F32 = jnp.float32
BF16 = jnp.bfloat16

D_MODEL = 2048
BATCH = 8
SEQ = 2048
DEPTH = 2
GRID_W = 64
CTX_LEN = 256
EPS = 1e-6
N_MOD = 9
FFN_HIDDEN = 5632

MLA_HEADS = 8
MLA_NOPE = 128
MLA_ROPE = 64
MLA_V = 128
MLA_Q_LORA = 512
MLA_KV_LORA = 256
MLA_WIDTH = MLA_HEADS * MLA_V
ROPE_BASE = 10000.0

HY_WIDTH = 512
HY_ORDER = 2
HY_EMB = 33
HY_FFN = 64
HY_SIN_FREQ = 1.0
HY_MIN_DECAY = math.log(1e-2) / 1.5
HY_MAX_DECAY = math.log(1e-2) / 0.3
HY_WINDOW_SHIFT = 0.05

RET_HEADS = 4
RET_DK = 64
RET_DV = 128
RET_WIDTH = RET_HEADS * RET_DV
RET_CHUNK = 128

VMEM_LIMIT_BYTES = 60 * 1024 * 1024
LANES = 128
HEAD_SLOT = 256

Z_HY = 0
Z_RV = 1536
Z_QLAT = 2048
Z_RGATE = 2560
Z_RQ = 3072
Z_RK = 3328
Z_KVLAT = 3584
Z_KR = 3840
Z_COLS = 4096

TM = 1024
TN = 512


def _cparams(*sem):
    return pltpu.CompilerParams(dimension_semantics=sem, vmem_limit_bytes=VMEM_LIMIT_BYTES)


def _rms(x, width=None):
    width = x.shape[-1] if width is None else width
    return x * lax.rsqrt(jnp.sum(x * x, axis=-1, keepdims=True) / width + EPS)


def _silu(x):
    return x * jax.nn.sigmoid(x)


def _mod_kernel(c_ref, w_ref, b_ref, o_ref):
    a = _silu(c_ref[...])
    o_ref[...] = jnp.dot(a, w_ref[...], preferred_element_type=F32,
                         precision=lax.Precision.HIGHEST) + b_ref[...]


def _modulation(cond, ada_w, ada_b):
    nl, d, nout = ada_w.shape
    bn = 1024
    return pl.pallas_call(
        _mod_kernel,
        grid=(nl, nout // bn),
        in_specs=[pl.BlockSpec((16, d), lambda l, j: (0, 0)),
                  pl.BlockSpec((None, d, bn), lambda l, j: (l, 0, j)),
                  pl.BlockSpec((None, 1, bn), lambda l, j: (l, 0, j))],
        out_specs=pl.BlockSpec((None, 16, bn), lambda l, j: (l, 0, j)),
        out_shape=jax.ShapeDtypeStruct((nl, 16, nout), F32),
        compiler_params=_cparams("arbitrary", "arbitrary"),
        name="modulation",
    )(cond, ada_w, ada_b.reshape(nl, 1, nout))


def _normmod_kernel(x_ref, sh_ref, sc_ref, o_ref):
    y = _rms(x_ref[...])
    o_ref[...] = (y * (1.0 + sc_ref[...]) + sh_ref[...]).astype(BF16)


def _normmod(s, shift, scale, n):
    r, d = s.shape
    tm = min(512, n)
    per_batch = shift.shape[0] > 1
    mod_spec = pl.BlockSpec((None, 1, d), lambda i: ((i * tm) // n if per_batch else 0, 0, 0))
    return pl.pallas_call(
        _normmod_kernel,
        grid=(r // tm,),
        in_specs=[pl.BlockSpec((tm, d), lambda i: (i, 0)), mod_spec, mod_spec],
        out_specs=pl.BlockSpec((tm, d), lambda i: (i, 0)),
        out_shape=jax.ShapeDtypeStruct((r, d), BF16),
        compiler_params=_cparams("arbitrary"),
        name="normmod",
    )(s, shift, scale)


def _ffn_up_kernel(h_ref, wg_ref, wu_ref, o_ref, wg_bf, wu_bf):
    @pl.when(pl.program_id(1) == 0)
    def _():
        wg_bf[...] = wg_ref[...].astype(BF16)
        wu_bf[...] = wu_ref[...].astype(BF16)

    h = h_ref[...]
    g = jnp.dot(h, wg_bf[...], preferred_element_type=F32)
    u = jnp.dot(h, wu_bf[...], preferred_element_type=F32)
    o_ref[...] = (_silu(g) * u).astype(BF16)


def _ffn_up(h, wg, wu, l):
    r, d = h.shape
    f = wg.shape[-1]
    w_spec = pl.BlockSpec((None, d, TN), lambda j, i: (l, 0, j))
    return pl.pallas_call(
        _ffn_up_kernel,
        grid=(f // TN, r // TM),
        in_specs=[pl.BlockSpec((TM, d), lambda j, i: (i, 0)), w_spec, w_spec],
        out_specs=pl.BlockSpec((TM, TN), lambda j, i: (i, j)),
        out_shape=jax.ShapeDtypeStruct((r, f), BF16),
        scratch_shapes=[pltpu.VMEM((d, TN), BF16), pltpu.VMEM((d, TN), BF16)],
        compiler_params=_cparams("arbitrary", "arbitrary"),
        name="ffn_up",
    )(h, wg, wu)


def _mm_kernel(a_ref, w_ref, o_ref):
    o_ref[...] = jnp.dot(a_ref[...], w_ref[...], preferred_element_type=F32).astype(o_ref.dtype)


def _matmul(a, w, l):
    r, k = a.shape
    nout = w.shape[-1]
    return pl.pallas_call(
        _mm_kernel,
        grid=(r // TM, nout // TN),
        in_specs=[pl.BlockSpec((TM, k), lambda i, j: (i, 0)),
                  pl.BlockSpec((None, k, TN), lambda i, j: (l, 0, j))],
        out_specs=pl.BlockSpec((TM, TN), lambda i, j: (i, j)),
        out_shape=jax.ShapeDtypeStruct((r, nout), BF16),
        compiler_params=_cparams("arbitrary", "arbitrary"),
        name="matmul",
    )(a, w)


def _ffn_down_kernel(u_ref, w_ref, s_ref, g_ref, o_ref):
    acc = jnp.dot(u_ref[...], w_ref[...], preferred_element_type=F32)
    o_ref[...] = s_ref[...] + (0.5 * g_ref[...]) * acc


def _ffn_down(u, w, l, s, gate, n):
    r, d = s.shape
    f = u.shape[1]
    per_batch = gate.shape[0] > 1
    return pl.pallas_call(
        _ffn_down_kernel,
        grid=(r // TM, d // TN),
        in_specs=[pl.BlockSpec((TM, f), lambda i, j: (i, 0)),
                  pl.BlockSpec((None, f, TN), lambda i, j: (l, 0, j)),
                  pl.BlockSpec((TM, TN), lambda i, j: (i, j)),
                  pl.BlockSpec((None, 1, TN), lambda i, j: ((i * TM) // n if per_batch else 0, 0, j))],
        out_specs=pl.BlockSpec((TM, TN), lambda i, j: (i, j)),
        out_shape=jax.ShapeDtypeStruct((r, d), F32),
        compiler_params=_cparams("arbitrary", "arbitrary"),
        name="ffn_down",
    )(u, w, s, gate)


def _mm_res_kernel(*refs, n_in, coeff):
    a_refs, w_refs = refs[:n_in], refs[n_in:2 * n_in]
    s_ref, g_ref, o_ref = refs[2 * n_in:2 * n_in + 3]
    w_bfs = refs[2 * n_in + 3:]

    @pl.when(pl.program_id(1) == 0)
    def _():
        for w_ref, w_bf in zip(w_refs, w_bfs):
            w_bf[...] = w_ref[...].astype(BF16)

    acc = None
    for a_ref, w_bf in zip(a_refs, w_bfs):
        part = jnp.dot(a_ref[...], w_bf[...], preferred_element_type=F32)
        acc = part if acc is None else acc + part
    o_ref[...] = s_ref[...] + (coeff * g_ref[...]) * acc


def _matmul_residual(parts, w, l, s, gate, n, coeff):
    r, d = s.shape
    tm = TM
    in_specs, w_specs, scratch, row = [], [], [], 0
    for p in parts:
        kp = p.shape[1]
        assert row % kp == 0
        in_specs.append(pl.BlockSpec((tm, kp), lambda j, i: (i, 0)))
        w_specs.append(pl.BlockSpec((None, kp, TN), functools.partial(
            lambda j, i, rb: (l, rb, j), rb=row // kp)))
        scratch.append(pltpu.VMEM((kp, TN), BF16))
        row += kp
    assert row == w.shape[1]
    per_batch = gate.shape[0] > 1
    return pl.pallas_call(
        functools.partial(_mm_res_kernel, n_in=len(parts), coeff=coeff),
        grid=(d // TN, r // tm),
        in_specs=in_specs + w_specs + [
            pl.BlockSpec((tm, TN), lambda j, i: (i, j)),
            pl.BlockSpec((None, 1, TN), lambda j, i: ((i * tm) // n if per_batch else 0, 0, j))],
        out_specs=pl.BlockSpec((tm, TN), lambda j, i: (i, j)),
        out_shape=jax.ShapeDtypeStruct((r, d), F32),
        scratch_shapes=scratch,
        compiler_params=_cparams("arbitrary", "arbitrary"),
        name="matmul_residual",
    )(*parts, *([w] * len(parts)), s, gate)


def _rope_lanes(x):
    lane = lax.broadcasted_iota(jnp.int32, x.shape, 1)
    ms = jnp.sum(jnp.where(lane < MLA_ROPE, x * x, 0.0), axis=-1, keepdims=True) / MLA_ROPE
    return x * lax.rsqrt(ms + EPS)


def _kv_kernel(lat_ref, kr_ref, gkv_ref, w_ref, gkn_ref, gkr_ref, tab_ref, k_ref, v_ref):
    a = _rms(lat_ref[...].astype(F32)) * gkv_ref[...]
    kv = jnp.dot(a.astype(BF16), w_ref[...], preferred_element_type=F32)
    t = _rope_lanes(kr_ref[...].astype(F32)) * gkr_ref[...] * tab_ref[...]
    krot = (t + pltpu.roll(t, MLA_ROPE, 1)).astype(BF16)
    for h in range(MLA_HEADS):
        c0 = h * HEAD_SLOT
        kn = _rms(kv[:, c0:c0 + MLA_NOPE]) * gkn_ref[...]
        k_ref[:, c0:c0 + MLA_NOPE] = kn.astype(BF16)
        k_ref[:, c0 + MLA_NOPE:c0 + HEAD_SLOT] = krot
        v_ref[:, c0:c0 + MLA_V] = kv[:, c0 + MLA_NOPE:c0 + HEAD_SLOT].astype(BF16)
        v_ref[:, c0 + MLA_V:c0 + HEAD_SLOT] = jnp.ones((kv.shape[0], HEAD_SLOT - MLA_V), BF16)


def _kv_proj(z, l, n, wukv, g_kv, g_kn, g_kr, tab):
    r = z.shape[0]
    tm = min(512, n)
    nt = tab.shape[0] // tm
    vec = lambda w: pl.BlockSpec((None, 1, w), lambda i: (l, 0, 0))
    return pl.pallas_call(
        _kv_kernel,
        grid=(r // tm,),
        in_specs=[pl.BlockSpec((tm, MLA_KV_LORA), lambda i: (i, Z_KVLAT // MLA_KV_LORA)),
                  pl.BlockSpec((tm, LANES), lambda i: (i, Z_KR // LANES)),
                  vec(MLA_KV_LORA),
                  pl.BlockSpec((None, MLA_KV_LORA, MLA_HEADS * HEAD_SLOT), lambda i: (l, 0, 0)),
                  vec(MLA_NOPE), vec(LANES),
                  pl.BlockSpec((tm, LANES), lambda i: (i % nt, 0))],
        out_specs=[pl.BlockSpec((tm, MLA_HEADS * HEAD_SLOT), lambda i: (i, 0)),
                   pl.BlockSpec((tm, MLA_HEADS * HEAD_SLOT), lambda i: (i, 0))],
        out_shape=[jax.ShapeDtypeStruct((r, MLA_HEADS * HEAD_SLOT), BF16),
                   jax.ShapeDtypeStruct((r, MLA_HEADS * HEAD_SLOT), BF16)],
        compiler_params=_cparams("arbitrary"),
        name="kv_proj",
    )(z, z, g_kv, wukv, g_kn, g_kr, tab)


def _q_kernel(lat_ref, gq_ref, w_ref, gn_ref, gr_ref, tab_ref, q_ref, *, scale):
    a = _rms(lat_ref[...].astype(F32)) * gq_ref[...]
    q = jnp.dot(a.astype(BF16), w_ref[...], preferred_element_type=F32)
    rope_scale = gr_ref[...] * tab_ref[...] * scale
    for h in range(MLA_HEADS):
        c0 = h * HEAD_SLOT
        qn = _rms(q[:, c0:c0 + MLA_NOPE]) * (gn_ref[...] * scale)
        q_ref[:, c0:c0 + MLA_NOPE] = qn.astype(BF16)
        qr = _rope_lanes(q[:, c0 + MLA_NOPE:c0 + HEAD_SLOT]) * rope_scale
        q_ref[:, c0 + MLA_NOPE:c0 + HEAD_SLOT] = qr.astype(BF16)


def _q_proj(z, l, n, wuq, g_q, g_qn, g_qr, tab):
    r = z.shape[0]
    tm = min(512, n)
    nt = tab.shape[0] // tm
    vec = lambda w: pl.BlockSpec((None, 1, w), lambda i: (l, 0, 0))
    return pl.pallas_call(
        functools.partial(_q_kernel, scale=(MLA_NOPE + MLA_ROPE) ** -0.5 * math.log2(math.e)),
        grid=(r // tm,),
        in_specs=[pl.BlockSpec((tm, MLA_Q_LORA), lambda i: (i, Z_QLAT // MLA_Q_LORA)),
                  vec(MLA_Q_LORA),
                  pl.BlockSpec((None, MLA_Q_LORA, MLA_HEADS * HEAD_SLOT), lambda i: (l, 0, 0)),
                  vec(MLA_NOPE), vec(LANES),
                  pl.BlockSpec((tm, LANES), lambda i: (i % nt, 0))],
        out_specs=pl.BlockSpec((tm, MLA_HEADS * HEAD_SLOT), lambda i: (i, 0)),
        out_shape=jax.ShapeDtypeStruct((r, MLA_HEADS * HEAD_SLOT), BF16),
        compiler_params=_cparams("arbitrary"),
        name="q_proj",
    )(z, g_q, wuq, g_qn, g_qr, tab)


def _attn_kernel(*refs, n_src):
    q_ref = refs[0]
    k_refs, v_refs = refs[1:1 + n_src], refs[1 + n_src:1 + 2 * n_src]
    g_ref, o_ref, acc_ref = refs[1 + 2 * n_src:]
    def scores(h):
        hs = slice(h * HEAD_SLOT, (h + 1) * HEAD_SLOT)
        return [lax.dot_general(q_ref[:, hs], k_ref[:, hs], (((1,), (1,)), ((), ())),
                                preferred_element_type=F32) for k_ref in k_refs]

    ssq = None
    ss_next = scores(0)
    for h in range(MLA_HEADS):
        hs = slice(h * HEAD_SLOT, (h + 1) * HEAD_SLOT)
        ss, ss_next = ss_next, (scores(h + 1) if h + 1 < MLA_HEADS else None)
        m = functools.reduce(jnp.maximum, [jnp.max(s, axis=-1, keepdims=True) for s in ss])
        o = functools.reduce(jnp.add, [
            jnp.dot(jnp.exp2((s - m).astype(BF16)), v_ref[:, hs], preferred_element_type=F32)
            for s, v_ref in zip(ss, v_refs)])
        o = o[:, :MLA_V] / o[:, MLA_V:MLA_V + 1]
        acc_ref[:, h * MLA_V:(h + 1) * MLA_V] = o
        sq = jnp.sum(o * o, axis=-1, keepdims=True)
        ssq = sq if ssq is None else ssq + sq
    o_ref[...] = (acc_ref[...] * lax.rsqrt(ssq / MLA_WIDTH + EPS) * g_ref[...]).astype(BF16)


def _attention(q, ks, vs, ns, l, g_out):
    r = q.shape[0]
    nq = r // BATCH
    tq = 256
    per_b = nq // tq
    k_specs = [pl.BlockSpec((nk, MLA_HEADS * HEAD_SLOT), lambda b, i: (b, 0)) for nk in ns]
    v_specs = k_specs
    return pl.pallas_call(
        functools.partial(_attn_kernel, n_src=len(ks)),
        grid=(BATCH, per_b),
        in_specs=[pl.BlockSpec((tq, MLA_HEADS * HEAD_SLOT), lambda b, i: (b * per_b + i, 0))]
        + k_specs + v_specs
        + [pl.BlockSpec((None, 1, MLA_WIDTH), lambda b, i: (l, 0, 0))],
        out_specs=pl.BlockSpec((tq, MLA_WIDTH), lambda b, i: (b * per_b + i, 0)),
        out_shape=jax.ShapeDtypeStruct((r, MLA_WIDTH), BF16),
        scratch_shapes=[pltpu.VMEM((tq, MLA_WIDTH), F32)],
        compiler_params=_cparams("arbitrary", "arbitrary"),
        name="attention",
    )(q, *ks, *vs, g_out)


def _log_sigmoid(x):
    return jnp.minimum(x, 0.0) - jnp.log(1.0 + jnp.exp(-jnp.abs(x)))


def _ret_kernel(q_ref, k_ref, v_ref, gate_ref, dec_ref, s0_ref, gnw_ref, gnb_ref,
                o_ref, st_ref, tab_ref, kt_ref, u_ref, sin_ref, *state_refs, n):
    nc = n // RET_CHUNK
    c = RET_CHUNK

    @pl.when(pl.program_id(0) == 0)
    def _():
        ii = lax.broadcasted_iota(jnp.int32, (c, c), 0).astype(F32)
        jj = lax.broadcasted_iota(jnp.int32, (c, c), 1).astype(F32)
        for h in range(RET_HEADS):
            for d in range(2):
                lg = _log_sigmoid(dec_ref[d, h][0:1, :])
                lgm = jnp.broadcast_to(lg, (c, c))
                if d == 0:
                    diff, qpow, kpow = ii - jj, ii + 1.0, (c - 1.0) - ii
                else:
                    diff, qpow, kpow = jj - ii, c - ii, ii
                intra = jnp.where(diff >= 0, jnp.exp(lgm * jnp.maximum(diff, 0.0)), 0.0)
                if d == 0:
                    tab_ref[h, 0, 0] = intra
                else:
                    tab_ref[h, 0, 0] += intra
                tab_ref[h, d, 1] = jnp.exp(lgm * qpow)
                kpow_t = (c - 1.0) - jj if d == 0 else jj
                k_scale = jnp.where(ii.astype(jnp.int32) // RET_DK == h % 2, RET_DK ** -0.5, 0.0)
                tab_ref[h, d, 2] = jnp.exp(lgm * kpow_t) * k_scale
                tab_ref[h, d, 3] = jnp.exp(lgm * float(c))
                tab_ref[h, d, 4] = k_scale

    def head_slices(h):
        return slice((h // 2) * LANES, (h // 2 + 1) * LANES), slice(h * RET_DV, (h + 1) * RET_DV)

    chains = [(h, d) for h in range(RET_HEADS) for d in range(2)]

    def contributions(ci, carry):
        r0 = pl.multiple_of(ci * c, c)
        kt_all = k_ref[pl.ds(r0, c), :].astype(F32).T
        kt_ref[ci] = kt_all.astype(BF16)
        for h in range(RET_HEADS):
            grp, hs = head_slices(h)
            v = v_ref[pl.ds(r0, c), hs]
            for d in range(2):
                kd_t = (kt_all[grp, :] * tab_ref[h, d, 2]).astype(BF16)
                u_ref[2 * h + d, ci] = jnp.dot(kd_t, v, preferred_element_type=F32)
        return carry

    lax.fori_loop(0, nc, contributions, 0)

    for (h, d), s_ref in zip(chains, state_refs):
        s_ref[...] = s0_ref[h, d]

    def scan(ci, carry):
        for idx, (h, d) in enumerate(chains):
            cc = ci if d == 0 else nc - 1 - ci
            state = state_refs[idx][...]
            sin_ref[idx, cc] = state.astype(BF16)
            state_refs[idx][...] = tab_ref[h, d, 3] * state + u_ref[idx, cc]
        return carry

    lax.fori_loop(0, nc, scan, 0)
    for (h, d), s_ref in zip(chains, state_refs):
        st_ref[h, d] = s_ref[...]

    per_trip = 2

    def outputs(ti, carry):
        pairs = [(ti * per_trip + j, h) for j in range(per_trip) for h in range(RET_HEADS)]
        rows = [pl.ds(pl.multiple_of(ci * c, c), c) for ci, _ in pairs]
        qs = [q_ref[r, head_slices(h)[0]] for r, (_, h) in zip(rows, pairs)]
        vs = [v_ref[r, head_slices(h)[1]] for r, (_, h) in zip(rows, pairs)]
        raw = [jnp.dot(q, (kt_ref[ci, head_slices(h)[0], :].astype(F32)
                           * tab_ref[h, 0, 4]).astype(BF16), preferred_element_type=F32)
               for q, (ci, h) in zip(qs, pairs)]
        cross = [[jnp.dot(q, sin_ref[2 * h + d, ci], preferred_element_type=F32)
                  for d in range(2)] for q, (ci, h) in zip(qs, pairs)]
        intra = [jnp.dot((s * tab_ref[h, 0, 0]).astype(BF16), v, preferred_element_type=F32)
                 for s, v, (_, h) in zip(raw, vs, pairs)]
        for r, (_, h), o_in, o_cr in zip(rows, pairs, intra, cross):
            hs = head_slices(h)[1]
            o = o_in + o_cr[0] * tab_ref[h, 0, 1] + o_cr[1] * tab_ref[h, 1, 1]
            mu = jnp.mean(o, axis=-1, keepdims=True)
            var = jnp.mean(jnp.square(o - mu), axis=-1, keepdims=True)
            on = (o - mu) * lax.rsqrt(var + EPS) * gnw_ref[:, hs] + gnb_ref[:, hs]
            gate = gate_ref[r, hs].astype(F32)
            o_ref[r, hs] = (_silu(gate) * on).astype(BF16)
        return carry

    lax.fori_loop(0, nc // per_trip, outputs, 0)


def _retention(z, l, n, dec, s0, gn_w, gn_b):
    r = z.shape[0]
    w = RET_WIDTH
    qk = RET_HEADS * RET_DK
    nc = n // RET_CHUNK
    col = lambda base, width: pl.BlockSpec((n, width), lambda b: (b, base // width))
    st_spec = pl.BlockSpec((None, RET_HEADS, 2, LANES, RET_DV), lambda b: (b, 0, 0, 0, 0))
    gn_spec = pl.BlockSpec((None, 1, w), lambda b: (l, 0, 0))
    return pl.pallas_call(
        functools.partial(_ret_kernel, n=n),
        grid=(BATCH,),
        in_specs=[col(Z_RQ, qk), col(Z_RK, qk), col(Z_RV, w), col(Z_RGATE, w),
                  pl.BlockSpec((None, 2, RET_HEADS, 8, LANES), lambda b: (l, 0, 0, 0, 0)),
                  st_spec, gn_spec, gn_spec],
        out_specs=[pl.BlockSpec((n, w), lambda b: (b, 0)), st_spec],
        out_shape=[jax.ShapeDtypeStruct((r, w), BF16),
                   jax.ShapeDtypeStruct((BATCH, RET_HEADS, 2, LANES, RET_DV), F32)],
        scratch_shapes=[pltpu.VMEM((RET_HEADS, 2, 5, RET_CHUNK, RET_CHUNK), F32),
                        pltpu.VMEM((nc, qk, RET_CHUNK), BF16),
                        pltpu.VMEM((2 * RET_HEADS, nc, LANES, RET_DV), F32),
                        pltpu.VMEM((2 * RET_HEADS, nc, LANES, RET_DV), BF16)]
        + [pltpu.VMEM((LANES, RET_DV), F32)] * (2 * RET_HEADS),
        compiler_params=_cparams("arbitrary"),
        name="retention",
    )(z, z, z, z, dec, s0, gn_w, gn_b)


def _hy_filter_kernel(feat_ref, w1_ref, b1_ref, w2_ref, b2_ref, w3_ref, dl_ref, s_ref, d_ref):
    hi = lax.Precision.HIGHEST
    feats = feat_ref[...]
    h = jnp.sin(HY_SIN_FREQ * (jnp.dot(feats, w1_ref[...], preferred_element_type=F32, precision=hi)
                               + b1_ref[...]))
    h = jnp.sin(HY_SIN_FREQ * (jnp.dot(h, w2_ref[...], preferred_element_type=F32, precision=hi)
                               + b2_ref[...]))
    h = jnp.dot(h, w3_ref[...], preferred_element_type=F32, precision=hi)
    window = jnp.exp(-feats[:, 0:1] * dl_ref[...]) + HY_WINDOW_SHIFT
    window = jnp.concatenate([window] * HY_ORDER, axis=-1)
    half = HY_ORDER * HY_WIDTH
    h_fwd = h[:, :half] * window
    h_bwd = h[:, half:] * window
    s_ref[...] = (h_fwd + h_bwd).astype(BF16)
    d_ref[...] = (h_fwd - h_bwd).astype(BF16)


def _hy_filter(feats, w1, b1, w2, b2, w3, deltas, l):
    n = feats.shape[0]
    tn = 256
    half = HY_ORDER * HY_WIDTH
    full = lambda a, b: pl.BlockSpec((None, a, b), lambda i: (l, 0, 0))
    return pl.pallas_call(
        _hy_filter_kernel,
        grid=(n // tn,),
        in_specs=[pl.BlockSpec((tn, LANES), lambda i: (i, 0)),
                  full(LANES, HY_FFN), full(1, HY_FFN), full(HY_FFN, HY_FFN), full(1, HY_FFN),
                  full(HY_FFN, 2 * half),
                  pl.BlockSpec((1, HY_WIDTH), lambda i: (0, 0))],
        out_specs=[pl.BlockSpec((tn, half), lambda i: (i, 0))] * 2,
        out_shape=[jax.ShapeDtypeStruct((n, half), BF16)] * 2,
        compiler_params=_cparams("arbitrary"),
        name="hyena_filter",
    )(feats, w1, b1, w2, b2, w3, deltas)


def _hy_spec_kernel(fc_ref, fs_ref, alt_ref, s_ref, d_ref, hc_ref, hs_ref, hn_ref):
    s = s_ref[...]
    tk, n = fc_ref.shape
    row = lax.broadcasted_iota(jnp.int32, (tk, s.shape[1]), 0) + pl.program_id(0) * tk
    wk = jnp.where(row == 0, 1.0, 2.0) * (1.0 / (2 * n))
    hc_ref[...] = jnp.dot(fc_ref[...], s, preferred_element_type=F32) * wk
    hs_ref[...] = jnp.dot(fs_ref[...], d_ref[...], preferred_element_type=F32) * wk
    hn_ref[...] = jnp.dot(alt_ref[...], s, preferred_element_type=F32)[0:8, :] * (1.0 / (2 * n))


def _hy_spectrum(fc, fs, alt, s, d):
    n = fc.shape[0]
    tk = min(512, n)
    half = s.shape[1]
    out = pl.BlockSpec((tk, half), lambda i: (i, 0))
    return pl.pallas_call(
        _hy_spec_kernel,
        grid=(n // tk,),
        in_specs=[pl.BlockSpec((tk, n), lambda i: (i, 0)),
                  pl.BlockSpec((tk, n), lambda i: (i, 0)),
                  pl.BlockSpec((16, n), lambda i: (0, 0)),
                  pl.BlockSpec((n, half), lambda i: (0, 0)),
                  pl.BlockSpec((n, half), lambda i: (0, 0))],
        out_specs=[out, out, pl.BlockSpec((8, half), lambda i: (0, 0))],
        out_shape=[jax.ShapeDtypeStruct((n, half), F32)] * 2
        + [jax.ShapeDtypeStruct((8, half), F32)],
        compiler_params=_cparams("arbitrary"),
        name="hyena_spectrum",
    )(fc, fs, alt, s, d)


def _short_conv_kernel(u_ref, w_ref, b_ref, o_ref, pad_ref, *, n):
    zeros = jnp.zeros((8, pad_ref.shape[1]), F32)
    pad_ref[0:8, :] = zeros
    pad_ref[8 + n:16 + n, :] = zeros
    pad_ref[8:8 + n, :] = u_ref[...].astype(F32)
    w = w_ref[...]
    y = (pad_ref[7:7 + n, :] * w[0:1, :] + pad_ref[8:8 + n, :] * w[1:2, :]
         + pad_ref[9:9 + n, :] * w[2:3, :] + b_ref[...])
    o_ref[...] = y.astype(BF16)


def _short_conv(z, l, n, w, b):
    r = z.shape[0]
    wc = HY_WIDTH
    return pl.pallas_call(
        functools.partial(_short_conv_kernel, n=n),
        grid=(BATCH, (HY_ORDER + 1)),
        in_specs=[pl.BlockSpec((n, wc), lambda bi, j: (bi, Z_HY // wc + j)),
                  pl.BlockSpec((None, 3, wc), lambda bi, j: (l, 0, j)),
                  pl.BlockSpec((None, 1, wc), lambda bi, j: (l, 0, j))],
        out_specs=pl.BlockSpec((n, wc), lambda bi, j: (bi, j)),
        out_shape=jax.ShapeDtypeStruct((r, (HY_ORDER + 1) * wc), BF16),
        scratch_shapes=[pltpu.VMEM((n + 16, wc), F32)],
        compiler_params=_cparams("arbitrary", "arbitrary"),
        name="short_conv",
    )(z, w, b)


HY_PAIR = 2


def _long_conv_kernel(u_ref, gate_ref, fc_ref, fs_ref, g_ref, alt_ref, hc_ref, hs_ref, hn_ref,
                      skip_ref, gain_ref, o_ref, y_ref, nyq_ref, *, n, tk, final):
    j = pl.program_id(1)
    kt = n // tk

    @pl.when(j == 0)
    def _():
        for p in range(HY_PAIR):
            x_nyq = jnp.dot(alt_ref[...], u_ref[p], preferred_element_type=F32)
            nyq_ref[p] = x_nyq[0:8, :] * hn_ref[...]

    @pl.when(j < kt)
    def _():
        r0 = pl.multiple_of(j * tk, tk)
        hc, hs = hc_ref[...], hs_ref[...]
        for p in range(HY_PAIR):
            u = u_ref[p]
            xc = jnp.dot(fc_ref[...], u, preferred_element_type=F32)
            xs = jnp.dot(fs_ref[...], u, preferred_element_type=F32)
            y_ref[p, pl.ds(r0, tk), :] = (xc * hc - xs * hs).astype(BF16)
            y_ref[p, pl.ds(n + r0, tk), :] = (xc * hs + xs * hc).astype(BF16)

    @pl.when(j >= kt)
    def _():
        t0 = pl.multiple_of((j - kt) * tk, tk)
        t = lax.broadcasted_iota(jnp.int32, (tk, u_ref.shape[-1]), 0)
        alt = (1 - 2 * (t & 1)).astype(F32)
        for p in range(HY_PAIR):
            conv = jnp.dot(g_ref[...], y_ref[p], preferred_element_type=F32)
            u = u_ref[p, pl.ds(t0, tk), :].astype(F32)
            y = conv + alt * nyq_ref[p][0:1, :] + u * skip_ref[...]
            y = gate_ref[p].astype(F32) * y
            if final:
                y = _rms(y) * gain_ref[...]
            o_ref[p] = y.astype(BF16)


def _long_conv(src, u_col, gate_col, mats, spec, order, skip, gain, l, n, final):
    fc, fs, g, alt = mats
    hc, hs, hn = spec
    r = src[0].shape[0]
    wc = HY_WIDTH
    tk = min(512, n)
    kt = n // tk
    nb = BATCH // HY_PAIR
    u4, gate4 = (a.reshape(nb, HY_PAIR, n, a.shape[-1]) for a in src)
    f_spec = pl.BlockSpec((tk, n), lambda bi, j: (jnp.minimum(j, kt - 1), 0))
    h_spec = pl.BlockSpec((tk, wc), lambda bi, j: (jnp.minimum(j, kt - 1), order))
    tile = lambda col: pl.BlockSpec((None, HY_PAIR, tk, wc),
                                    lambda bi, j: (bi, 0, jnp.maximum(j - kt, 0), col))
    out = pl.pallas_call(
        functools.partial(_long_conv_kernel, n=n, tk=tk, final=final),
        grid=(nb, 2 * kt),
        in_specs=[pl.BlockSpec((None, HY_PAIR, n, wc), lambda bi, j: (bi, 0, 0, u_col)),
                  tile(gate_col),
                  f_spec, f_spec,
                  pl.BlockSpec((tk, 2 * n), lambda bi, j: (jnp.maximum(j - kt, 0), 0)),
                  pl.BlockSpec((16, n), lambda bi, j: (0, 0)),
                  h_spec, h_spec,
                  pl.BlockSpec((8, wc), lambda bi, j: (0, order)),
                  pl.BlockSpec((None, None, 1, wc), lambda bi, j: (l, order, 0, 0)),
                  pl.BlockSpec((None, 1, wc), lambda bi, j: (l, 0, 0))],
        out_specs=tile(0),
        out_shape=jax.ShapeDtypeStruct((nb, HY_PAIR, n, wc), BF16),
        scratch_shapes=[pltpu.VMEM((HY_PAIR, 2 * n, wc), BF16),
                        pltpu.VMEM((HY_PAIR, 8, wc), F32)],
        compiler_params=_cparams("arbitrary", "arbitrary"),
        name="long_conv",
    )(u4, gate4, fc, fs, g, alt, hc, hs, hn, skip, gain)
    return out.reshape(r, wc)


def _dft_mats(n):
    big = 2 * n
    fine = 32
    t = jnp.arange(n, dtype=jnp.int32)

    def trig(k):
        ang = ((k[:, None] * t[None, :]) % big).astype(F32) * (2.0 * math.pi / big)
        return jnp.cos(ang), jnp.sin(ang)

    ca, sa = (m[:, None, :] for m in trig(fine * jnp.arange(n // fine, dtype=jnp.int32)))
    cb, sb = (m[None, :, :] for m in trig(jnp.arange(fine, dtype=jnp.int32)))
    fc = (ca * cb - sa * sb).reshape(n, n).astype(BF16)
    fs = (sa * cb + ca * sb).reshape(n, n).astype(BF16)
    alt = jnp.broadcast_to((1 - 2 * (t % 2)).astype(F32)[None, :], (16, n))
    return fc, fs, jnp.concatenate([fc, fs], axis=1), alt.astype(BF16)


def _hy_features(n):
    t = jnp.linspace(0.0, 1.0, n, dtype=F32)[:, None]
    bands = (HY_EMB - 1) // 2
    w = 2 * math.pi * jnp.arange(n, dtype=F32)[:, None] / n
    f = jnp.linspace(1e-4, bands - 1, bands, dtype=F32)[None, :]
    feats = jnp.concatenate([t, jnp.cos(f * w), -jnp.sin(f * w)], axis=-1)
    return jnp.pad(feats, ((0, 0), (0, LANES - HY_EMB)))


def _rope_table():
    n_freq = MLA_ROPE // 4
    inv_freq = ROPE_BASE ** (-jnp.arange(n_freq, dtype=F32) / n_freq)
    rows = SEQ // GRID_W
    row = jnp.repeat(jnp.arange(rows), GRID_W).astype(F32)
    col = jnp.tile(jnp.arange(GRID_W), rows).astype(F32)
    ar = row[:, None] * inv_freq[None, :]
    ac = col[:, None] * inv_freq[None, :]
    cos_part = jnp.concatenate([jnp.cos(ar), jnp.cos(ar), jnp.cos(ac), jnp.cos(ac)], axis=-1)
    sin_part = jnp.concatenate([-jnp.sin(ar), jnp.sin(ar), -jnp.sin(ac), jnp.sin(ac)], axis=-1)
    return jnp.concatenate([cos_part, sin_part], axis=-1)


_ROPE_PARTNER = np.concatenate([np.arange(16, 32), np.arange(0, 16),
                                np.arange(48, 64), np.arange(32, 48)])
_QUARTER = MLA_ROPE // 4


def _partner_pieces(base):
    return [(base + int(_ROPE_PARTNER[i]), _QUARTER) for i in range(0, MLA_ROPE, _QUARTER)]


def _w_in_pieces():
    off = np.cumsum([0, MLA_KV_LORA, MLA_ROPE, RET_HEADS * RET_DK, RET_WIDTH, MLA_Q_LORA,
                     RET_HEADS * RET_DK, RET_WIDTH])
    o_kv, o_kr, o_rk, o_rv, o_q, o_rq, o_gate, o_hy = [int(v) for v in off]
    pieces = [(o_hy, 3 * HY_WIDTH), (o_rv, RET_WIDTH), (o_q, MLA_Q_LORA), (o_gate, RET_WIDTH),
              (o_rq, RET_HEADS * RET_DK), (o_rk, RET_HEADS * RET_DK),
              (o_kv, MLA_KV_LORA), (o_kr, MLA_ROPE)] + _partner_pieces(o_kr)
    pieces.append((None, Z_COLS - sum(w for _, w in pieces)))
    return pieces


def _wuq_pieces():
    per = MLA_NOPE + MLA_ROPE
    pieces = []
    for h in range(MLA_HEADS):
        pieces += [(h * per, per)] + _partner_pieces(h * per + MLA_NOPE)
    return pieces


def _cat_cols(w, pieces, dtype):
    cols = [jnp.zeros(w.shape[:-1] + (width,), dtype) if start is None
            else w[..., start:start + width].astype(dtype) for start, width in pieces]
    return jnp.concatenate(cols, axis=-1)


def kernel(x, c, ctx, c_ctx, ada_w, ada_b, ffn1_gate, ffn1_up, ffn1_down, w_in, mla_q_norm, mla_wuq, mla_kv_norm, mla_wukv, mla_qn_nope, mla_qn_rope, mla_kn_nope, mla_kn_rope, mla_out_norm, hy_conv_w, hy_conv_b, hy_ffn_w1, hy_ffn_b1, hy_ffn_w2, hy_ffn_b2, hy_ffn_w3, hy_skip, hy_out_norm, ret_decay, ret_gn_w, ret_gn_b, w_out, ffn2_gate, ffn2_up, ffn2_down):
    nl = DEPTH
    d = D_MODEL

    w1g, w1u, w1d = ffn1_gate, ffn1_up, ffn1_down.astype(BF16)
    w2g, w2u, w2d = ffn2_gate, ffn2_up, ffn2_down.astype(BF16)
    w_in_p = _cat_cols(w_in, _w_in_pieces(), BF16)
    wuq_p = _cat_cols(mla_wuq, _wuq_pieces(), BF16)
    wukv = mla_wukv.astype(BF16)
    w_out_b = w_out
    row3 = lambda g: g.reshape(nl, 1, -1)
    partner = jnp.asarray(_ROPE_PARTNER)
    g_qr = row3(jnp.concatenate([mla_qn_rope, mla_qn_rope[:, partner]], axis=-1))
    g_kr = row3(jnp.concatenate([mla_kn_rope, mla_kn_rope[:, partner]], axis=-1))
    g_q, g_kv, g_qn, g_kn = row3(mla_q_norm), row3(mla_kv_norm), row3(mla_qn_nope), row3(mla_kn_nope)
    g_attn, g_hy = row3(mla_out_norm), row3(hy_out_norm)
    gn_w, gn_b = row3(ret_gn_w), row3(ret_gn_b)
    hy_b = row3(hy_conv_b)
    hy_w1 = jnp.pad(hy_ffn_w1, ((0, 0), (0, LANES - HY_EMB), (0, 0)))
    hy_b1, hy_b2 = row3(hy_ffn_b1), row3(hy_ffn_b2)
    hy_skip4 = hy_skip.reshape(nl, HY_ORDER, 1, HY_WIDTH)
    dec = jnp.broadcast_to(ret_decay.astype(F32)[:, :, :, None, None], (nl, 2, RET_HEADS, 8, LANES))

    tab_x = _rope_table()
    tab_c = jnp.concatenate([jnp.ones((CTX_LEN, MLA_ROPE), F32), jnp.zeros((CTX_LEN, MLA_ROPE), F32)], -1)
    deltas = jnp.abs(jnp.linspace(HY_MIN_DECAY, HY_MAX_DECAY, HY_WIDTH, dtype=F32))[None, :]
    mats = {n: _dft_mats(n) for n in (SEQ, CTX_LEN)}
    feats = {n: _hy_features(n) for n in (SEQ, CTX_LEN)}

    cond = jnp.zeros((16, d), F32).at[0].set(c_ctx).at[1:1 + BATCH].set(c)
    mods = _modulation(cond, ada_w, ada_b).reshape(nl, 16, N_MOD, d)

    sx = x.reshape(BATCH * SEQ, d)
    sc = ctx.reshape(BATCH * CTX_LEN, d)

    def ffn(s, n, mod, wg, wu, wd, l):
        h = _normmod(s, mod[0], mod[1], n)
        u = _ffn_up(h, wg, wu, l)
        return _ffn_down(u, wd, l, s, mod[2], n)

    def hyena(z, l, n, spec):
        cv = _short_conv(z, l, n, hy_conv_w, hy_b)
        y1 = _long_conv((cv, cv), 2, 0, mats[n], spec, 0, hy_skip4, g_hy, l, n, False)
        return _long_conv((y1, cv), 0, 1, mats[n], spec, 1, hy_skip4, g_hy, l, n, True)

    for l in range(nl):
        need_ctx_out = l < nl - 1
        mod_x = [mods[l, 1:1 + BATCH, j][:, None, :] for j in range(N_MOD)]
        mod_c = [mods[l, 0:1, j][:, None, :] for j in range(N_MOD)]

        sx = ffn(sx, SEQ, mod_x[0:3], w1g, w1u, w1d, l)
        sc = ffn(sc, CTX_LEN, mod_c[0:3], w1g, w1u, w1d, l)

        zc = _matmul(_normmod(sc, mod_c[3], mod_c[4], CTX_LEN), w_in_p, l)
        zx = _matmul(_normmod(sx, mod_x[3], mod_x[4], SEQ), w_in_p, l)

        kc, vc = _kv_proj(zc, l, CTX_LEN, wukv, g_kv, g_kn, g_kr, tab_c)
        kx, vx = _kv_proj(zx, l, SEQ, wukv, g_kv, g_kn, g_kr, tab_x)
        zero_state = jnp.zeros((BATCH, RET_HEADS, 2, LANES, RET_DV), F32)
        ret_c, state_c = _retention(zc, l, CTX_LEN, dec, zero_state, gn_w, gn_b)

        filt = {}
        for n in ((SEQ, CTX_LEN) if need_ctx_out else (SEQ,)):
            s_f, d_f = _hy_filter(feats[n], hy_w1, hy_b1, hy_ffn_w2, hy_b2, hy_ffn_w3, deltas, l)
            fc, fs, _, alt = mats[n]
            filt[n] = _hy_spectrum(fc, fs, alt, s_f, d_f)

        qx = _q_proj(zx, l, SEQ, wuq_p, g_q, g_qn, g_qr, tab_x)
        attn_x = _attention(qx, [kc, kx], [vc, vx], [CTX_LEN, SEQ], l, g_attn)
        hy_x = hyena(zx, l, SEQ, filt[SEQ])
        ret_x, _ = _retention(zx, l, SEQ, dec, state_c, gn_w, gn_b)
        sx = _matmul_residual([attn_x, hy_x, ret_x], w_out_b, l, sx, mod_x[5], SEQ, 1.0)
        sx = ffn(sx, SEQ, mod_x[6:9], w2g, w2u, w2d, l)

        if need_ctx_out:
            qc = _q_proj(zc, l, CTX_LEN, wuq_p, g_q, g_qn, g_qr, tab_c)
            attn_c = _attention(qc, [kc], [vc], [CTX_LEN], l, g_attn)
            hy_c = hyena(zc, l, CTX_LEN, filt[CTX_LEN])
            sc = _matmul_residual([attn_c, hy_c, ret_c], w_out_b, l, sc, mod_c[5], CTX_LEN, 1.0)
            sc = ffn(sc, CTX_LEN, mod_c[6:9], w2g, w2u, w2d, l)

    return sx.reshape(BATCH, SEQ, d)
```

```python
import functools
import math

import jax
import jax.numpy as jnp
import numpy as np
from jax import lax
from jax.experimental import pallas as pl
from jax.experimental.pallas import tpu as pltpu

F32 = jnp.float32
BF16 = jnp.bfloat16

D_MODEL = 2048
BATCH = 8
SEQ = 2048
DEPTH = 2
GRID_W = 64
CTX_LEN = 256
EPS = 1e-6
N_MOD = 9
FFN_HIDDEN = 5632

MLA_HEADS = 8
MLA_NOPE = 128
MLA_ROPE = 64
MLA_V = 128
MLA_Q_LORA = 512
MLA_KV_LORA = 256
MLA_WIDTH = MLA_HEADS * MLA_V
ROPE_BASE = 10000.0

HY_WIDTH = 512
HY_ORDER = 2
HY_EMB = 33
HY_FFN = 64
HY_SIN_FREQ = 1.0
HY_MIN_DECAY = math.log(1e-2) / 1.5
HY_MAX_DECAY = math.log(1e-2) / 0.3
HY_WINDOW_SHIFT = 0.05

RET_HEADS = 4
RET_DK = 64
RET_DV = 128
RET_WIDTH = RET_HEADS * RET_DV
RET_CHUNK = 128

VMEM_LIMIT_BYTES = 60 * 1024 * 1024
LANES = 128
HEAD_SLOT = 256

Z_HY = 0
Z_RV = 1536
Z_QLAT = 2048
Z_RGATE = 2560
Z_RQ = 3072
Z_RK = 3328
Z_KVLAT = 3584
Z_KR = 3840
Z_COLS = 4096

TM = 1024
TN = 512


def _cparams(*sem):
    return pltpu.CompilerParams(dimension_semantics=sem, vmem_limit_bytes=VMEM_LIMIT_BYTES)


def _rms(x, width=None):
    width = x.shape[-1] if width is None else width
    return x * lax.rsqrt(jnp.sum(x * x, axis=-1, keepdims=True) / width + EPS)


def _silu(x):
    return x * jax.nn.sigmoid(x)


def _mod_kernel(c_ref, w_ref, b_ref, o_ref):
    a = _silu(c_ref[...])
    o_ref[...] = jnp.dot(a, w_ref[...], preferred_element_type=F32,
                         precision=lax.Precision.HIGHEST) + b_ref[...]


def _modulation(cond, ada_w, ada_b):
    nl, d, nout = ada_w.shape
    bn = 1024
    return pl.pallas_call(
        _mod_kernel,
        grid=(nl, nout // bn),
        in_specs=[pl.BlockSpec((16, d), lambda l, j: (0, 0)),
                  pl.BlockSpec((None, d, bn), lambda l, j: (l, 0, j)),
                  pl.BlockSpec((None, 1, bn), lambda l, j: (l, 0, j))],
        out_specs=pl.BlockSpec((None, 16, bn), lambda l, j: (l, 0, j)),
        out_shape=jax.ShapeDtypeStruct((nl, 16, nout), F32),
        compiler_params=_cparams("arbitrary", "arbitrary"),
        name="modulation",
    )(cond, ada_w, ada_b.reshape(nl, 1, nout))


def _normmod_kernel(x_ref, sh_ref, sc_ref, o_ref):
    y = _rms(x_ref[...])
    o_ref[...] = (y * (1.0 + sc_ref[...]) + sh_ref[...]).astype(BF16)


def _normmod(s, shift, scale, n):
    r, d = s.shape
    tm = min(512, n)
    per_batch = shift.shape[0] > 1
    mod_spec = pl.BlockSpec((None, 1, d), lambda i: ((i * tm) // n if per_batch else 0, 0, 0))
    return pl.pallas_call(
        _normmod_kernel,
        grid=(r // tm,),
        in_specs=[pl.BlockSpec((tm, d), lambda i: (i, 0)), mod_spec, mod_spec],
        out_specs=pl.BlockSpec((tm, d), lambda i: (i, 0)),
        out_shape=jax.ShapeDtypeStruct((r, d), BF16),
        compiler_params=_cparams("arbitrary"),
        name="normmod",
    )(s, shift, scale)


def _ffn_up_kernel(h_ref, wg_ref, wu_ref, o_ref, wg_bf, wu_bf):
    @pl.when(pl.program_id(1) == 0)
    def _():
        wg_bf[...] = wg_ref[...].astype(BF16)
        wu_bf[...] = wu_ref[...].astype(BF16)

    h = h_ref[...]
    g = jnp.dot(h, wg_bf[...], preferred_element_type=F32)
    u = jnp.dot(h, wu_bf[...], preferred_element_type=F32)
    o_ref[...] = (_silu(g) * u).astype(BF16)


def _ffn_up(h, wg, wu, l):
    r, d = h.shape
    f = wg.shape[-1]
    w_spec = pl.BlockSpec((None, d, TN), lambda j, i: (l, 0, j))
    return pl.pallas_call(
        _ffn_up_kernel,
        grid=(f // TN, r // TM),
        in_specs=[pl.BlockSpec((TM, d), lambda j, i: (i, 0)), w_spec, w_spec],
        out_specs=pl.BlockSpec((TM, TN), lambda j, i: (i, j)),
        out_shape=jax.ShapeDtypeStruct((r, f), BF16),
        scratch_shapes=[pltpu.VMEM((d, TN), BF16), pltpu.VMEM((d, TN), BF16)],
        compiler_params=_cparams("arbitrary", "arbitrary"),
        name="ffn_up",
    )(h, wg, wu)


def _mod_spec(table, tm, n, d):
    per_batch = table.shape[0] > 1
    return pl.BlockSpec((None, 1, d), lambda i, *_: ((i * tm) // n if per_batch else 0, 0, 0))


def _norm_mm_kernel(s_ref, sh_ref, sc_ref, w_ref, o_ref, h_ref):
    @pl.when(pl.program_id(1) == 0)
    def _():
        h_ref[...] = (_rms(s_ref[...]) * (1.0 + sc_ref[...]) + sh_ref[...]).astype(BF16)

    o_ref[...] = jnp.dot(h_ref[...], w_ref[...], preferred_element_type=F32).astype(o_ref.dtype)


def _norm_matmul(s, shift, scale, n, w, l):
    r, d = s.shape
    nout = w.shape[-1]
    return pl.pallas_call(
        _norm_mm_kernel,
        grid=(r // TM, nout // TN),
        in_specs=[pl.BlockSpec((TM, d), lambda i, j: (i, 0)),
                  _mod_spec(shift, TM, n, d), _mod_spec(scale, TM, n, d),
                  pl.BlockSpec((None, d, TN), lambda i, j: (l, 0, j))],
        out_specs=pl.BlockSpec((TM, TN), lambda i, j: (i, j)),
        out_shape=jax.ShapeDtypeStruct((r, nout), BF16),
        scratch_shapes=[pltpu.VMEM((TM, d), BF16)],
        compiler_params=_cparams("arbitrary", "arbitrary"),
        name="norm_matmul",
    )(s, shift, scale, w)


def _ffn_down_kernel(u_ref, w_ref, s_ref, g_ref, o_ref):
    acc = jnp.dot(u_ref[...], w_ref[...], preferred_element_type=F32)
    o_ref[...] = s_ref[...] + (0.5 * g_ref[...]) * acc


def _ffn_down(u, w, l, s, gate, n):
    r, d = s.shape
    f = u.shape[1]
    per_batch = gate.shape[0] > 1
    return pl.pallas_call(
        _ffn_down_kernel,
        grid=(r // TM, d // TN),
        in_specs=[pl.BlockSpec((TM, f), lambda i, j: (i, 0)),
                  pl.BlockSpec((None, f, TN), lambda i, j: (l, 0, j)),
                  pl.BlockSpec((TM, TN), lambda i, j: (i, j)),
                  pl.BlockSpec((None, 1, TN), lambda i, j: ((i * TM) // n if per_batch else 0, 0, j))],
        out_specs=pl.BlockSpec((TM, TN), lambda i, j: (i, j)),
        out_shape=jax.ShapeDtypeStruct((r, d), F32),
        compiler_params=_cparams("arbitrary", "arbitrary"),
        name="ffn_down",
    )(u, w, s, gate)


def _residual_epilogue(s_ref, g_ref, sh_ref, sc_ref, o_ref, h_ref, acc, coeff):
    s_new = s_ref[...] + (coeff * g_ref[...]) * acc
    o_ref[...] = s_new
    if h_ref is not None:
        h_ref[...] = (_rms(s_new) * (1.0 + sc_ref[...]) + sh_ref[...]).astype(BF16)


def _w_out_kernel(*refs, n_in, emit_h):
    a_refs = refs[:n_in]
    w_ref, s_ref, g_ref, sh_ref, sc_ref, o_ref = refs[n_in:n_in + 6]
    h_ref = refs[n_in + 6] if emit_h else None
    acc, row = None, 0
    for a_ref in a_refs:
        kp = a_ref.shape[1]
        part = jnp.dot(a_ref[...], w_ref[row:row + kp, :], preferred_element_type=F32)
        acc = part if acc is None else acc + part
        row += kp
    _residual_epilogue(s_ref, g_ref, sh_ref, sc_ref, o_ref, h_ref, acc, 1.0)


def _w_out(parts, w, l, s, gate, n, next_mod):
    r, d = s.shape
    tm = TM // 2
    emit_h = next_mod is not None
    shift, scale = next_mod if emit_h else (gate, gate)
    row = pl.BlockSpec((tm, d), lambda i: (i, 0))
    outs = pl.pallas_call(
        functools.partial(_w_out_kernel, n_in=len(parts), emit_h=emit_h),
        grid=(r // tm,),
        in_specs=[pl.BlockSpec((tm, p.shape[1]), lambda i: (i, 0)) for p in parts]
        + [pl.BlockSpec((None, d, d), lambda i: (l, 0, 0)),
           row, _mod_spec(gate, tm, n, d), _mod_spec(shift, tm, n, d), _mod_spec(scale, tm, n, d)],
        out_specs=[row, row] if emit_h else [row],
        out_shape=[jax.ShapeDtypeStruct((r, d), F32)]
        + ([jax.ShapeDtypeStruct((r, d), BF16)] if emit_h else []),
        compiler_params=_cparams("arbitrary"),
        name="w_out",
    )(*parts, w, s, gate, shift, scale)
    return (outs[0], outs[1]) if emit_h else (outs[0], None)


def _rope_lanes(x):
    lane = lax.broadcasted_iota(jnp.int32, x.shape, 1)
    ms = jnp.sum(jnp.where(lane < MLA_ROPE, x * x, 0.0), axis=-1, keepdims=True) / MLA_ROPE
    return x * lax.rsqrt(ms + EPS)


def _kv_kernel(lat_ref, kr_ref, gkv_ref, w_ref, gkn_ref, gkr_ref, tab_ref, k_ref, v_ref):
    a = _rms(lat_ref[...].astype(F32)) * gkv_ref[...]
    kv = jnp.dot(a.astype(BF16), w_ref[...], preferred_element_type=F32)
    t = _rope_lanes(kr_ref[...].astype(F32)) * gkr_ref[...] * tab_ref[...]
    krot = (t + pltpu.roll(t, MLA_ROPE, 1)).astype(BF16)
    for h in range(MLA_HEADS):
        c0 = h * HEAD_SLOT
        kn = _rms(kv[:, c0:c0 + MLA_NOPE]) * gkn_ref[...]
        k_ref[:, c0:c0 + MLA_NOPE] = kn.astype(BF16)
        k_ref[:, c0 + MLA_NOPE:c0 + HEAD_SLOT] = krot
        v_ref[:, c0:c0 + MLA_V] = kv[:, c0 + MLA_NOPE:c0 + HEAD_SLOT].astype(BF16)
        v_ref[:, c0 + MLA_V:c0 + HEAD_SLOT] = jnp.ones((kv.shape[0], HEAD_SLOT - MLA_V), BF16)


def _kv_proj(z, l, n, wukv, g_kv, g_kn, g_kr, tab):
    r = z.shape[0]
    tm = min(512, n)
    nt = tab.shape[0] // tm
    vec = lambda w: pl.BlockSpec((None, 1, w), lambda i: (l, 0, 0))
    return pl.pallas_call(
        _kv_kernel,
        grid=(r // tm,),
        in_specs=[pl.BlockSpec((tm, MLA_KV_LORA), lambda i: (i, Z_KVLAT // MLA_KV_LORA)),
                  pl.BlockSpec((tm, LANES), lambda i: (i, Z_KR // LANES)),
                  vec(MLA_KV_LORA),
                  pl.BlockSpec((None, MLA_KV_LORA, MLA_HEADS * HEAD_SLOT), lambda i: (l, 0, 0)),
                  vec(MLA_NOPE), vec(LANES),
                  pl.BlockSpec((tm, LANES), lambda i: (i % nt, 0))],
        out_specs=[pl.BlockSpec((tm, MLA_HEADS * HEAD_SLOT), lambda i: (i, 0)),
                   pl.BlockSpec((tm, MLA_HEADS * HEAD_SLOT), lambda i: (i, 0))],
        out_shape=[jax.ShapeDtypeStruct((r, MLA_HEADS * HEAD_SLOT), BF16),
                   jax.ShapeDtypeStruct((r, MLA_HEADS * HEAD_SLOT), BF16)],
        compiler_params=_cparams("arbitrary"),
        name="kv_proj",
    )(z, z, g_kv, wukv, g_kn, g_kr, tab)


def _q_kernel(lat_ref, gq_ref, w_ref, gn_ref, gr_ref, tab_ref, q_ref, *, scale):
    a = _rms(lat_ref[...].astype(F32)) * gq_ref[...]
    q = jnp.dot(a.astype(BF16), w_ref[...], preferred_element_type=F32)
    rope_scale = gr_ref[...] * tab_ref[...] * scale
    for h in range(MLA_HEADS):
        c0 = h * HEAD_SLOT
        qn = _rms(q[:, c0:c0 + MLA_NOPE]) * (gn_ref[...] * scale)
        q_ref[:, c0:c0 + MLA_NOPE] = qn.astype(BF16)
        qr = _rope_lanes(q[:, c0 + MLA_NOPE:c0 + HEAD_SLOT]) * rope_scale
        q_ref[:, c0 + MLA_NOPE:c0 + HEAD_SLOT] = qr.astype(BF16)


def _q_proj(z, l, n, wuq, g_q, g_qn, g_qr, tab):
    r = z.shape[0]
    tm = min(512, n)
    nt = tab.shape[0] // tm
    vec = lambda w: pl.BlockSpec((None, 1, w), lambda i: (l, 0, 0))
    return pl.pallas_call(
        functools.partial(_q_kernel, scale=(MLA_NOPE + MLA_ROPE) ** -0.5 * math.log2(math.e)),
        grid=(r // tm,),
        in_specs=[pl.BlockSpec((tm, MLA_Q_LORA), lambda i: (i, Z_QLAT // MLA_Q_LORA)),
                  vec(MLA_Q_LORA),
                  pl.BlockSpec((None, MLA_Q_LORA, MLA_HEADS * HEAD_SLOT), lambda i: (l, 0, 0)),
                  vec(MLA_NOPE), vec(LANES),
                  pl.BlockSpec((tm, LANES), lambda i: (i % nt, 0))],
        out_specs=pl.BlockSpec((tm, MLA_HEADS * HEAD_SLOT), lambda i: (i, 0)),
        out_shape=jax.ShapeDtypeStruct((r, MLA_HEADS * HEAD_SLOT), BF16),
        compiler_params=_cparams("arbitrary"),
        name="q_proj",
    )(z, g_q, wuq, g_qn, g_qr, tab)


def _attn_kernel(*refs, n_src):
    q_ref = refs[0]
    k_refs, v_refs = refs[1:1 + n_src], refs[1 + n_src:1 + 2 * n_src]
    g_ref, o_ref, acc_ref = refs[1 + 2 * n_src:]
    def scores(h):
        hs = slice(h * HEAD_SLOT, (h + 1) * HEAD_SLOT)
        return [lax.dot_general(q_ref[:, hs], k_ref[:, hs], (((1,), (1,)), ((), ())),
                                preferred_element_type=F32) for k_ref in k_refs]

    ssq = None
    ss_next = scores(0)
    for h in range(MLA_HEADS):
        hs = slice(h * HEAD_SLOT, (h + 1) * HEAD_SLOT)
        ss, ss_next = ss_next, (scores(h + 1) if h + 1 < MLA_HEADS else None)
        m = functools.reduce(jnp.maximum, [jnp.max(s, axis=-1, keepdims=True) for s in ss])
        o = functools.reduce(jnp.add, [
            jnp.dot(jnp.exp2((s - m).astype(BF16)), v_ref[:, hs], preferred_element_type=F32)
            for s, v_ref in zip(ss, v_refs)])
        o = o[:, :MLA_V] / o[:, MLA_V:MLA_V + 1]
        acc_ref[:, h * MLA_V:(h + 1) * MLA_V] = o
        sq = jnp.sum(o * o, axis=-1, keepdims=True)
        ssq = sq if ssq is None else ssq + sq
    o_ref[...] = (acc_ref[...] * lax.rsqrt(ssq / MLA_WIDTH + EPS) * g_ref[...]).astype(BF16)


def _attention(q, ks, vs, ns, l, g_out):
    r = q.shape[0]
    nq = r // BATCH
    tq = 256
    per_b = nq // tq
    k_specs = [pl.BlockSpec((nk, MLA_HEADS * HEAD_SLOT), lambda b, i: (b, 0)) for nk in ns]
    v_specs = k_specs
    return pl.pallas_call(
        functools.partial(_attn_kernel, n_src=len(ks)),
        grid=(BATCH, per_b),
        in_specs=[pl.BlockSpec((tq, MLA_HEADS * HEAD_SLOT), lambda b, i: (b * per_b + i, 0))]
        + k_specs + v_specs
        + [pl.BlockSpec((None, 1, MLA_WIDTH), lambda b, i: (l, 0, 0))],
        out_specs=pl.BlockSpec((tq, MLA_WIDTH), lambda b, i: (b * per_b + i, 0)),
        out_shape=jax.ShapeDtypeStruct((r, MLA_WIDTH), BF16),
        scratch_shapes=[pltpu.VMEM((tq, MLA_WIDTH), F32)],
        compiler_params=_cparams("arbitrary", "arbitrary"),
        name="attention",
    )(q, *ks, *vs, g_out)


def _log_sigmoid(x):
    return jnp.minimum(x, 0.0) - jnp.log(1.0 + jnp.exp(-jnp.abs(x)))


def _ret_kernel(q_ref, k_ref, v_ref, gate_ref, dec_ref, s0_ref, gnw_ref, gnb_ref,
                o_ref, st_ref, tab_ref, kt_ref, u_ref, sin_ref, *state_refs, n):
    nc = n // RET_CHUNK
    c = RET_CHUNK

    @pl.when(pl.program_id(0) == 0)
    def _():
        ii = lax.broadcasted_iota(jnp.int32, (c, c), 0).astype(F32)
        jj = lax.broadcasted_iota(jnp.int32, (c, c), 1).astype(F32)
        for h in range(RET_HEADS):
            for d in range(2):
                lg = _log_sigmoid(dec_ref[d, h][0:1, :])
                lgm = jnp.broadcast_to(lg, (c, c))
                if d == 0:
                    diff, qpow, kpow = ii - jj, ii + 1.0, (c - 1.0) - ii
                else:
                    diff, qpow, kpow = jj - ii, c - ii, ii
                intra = jnp.where(diff >= 0, jnp.exp(lgm * jnp.maximum(diff, 0.0)), 0.0)
                if d == 0:
                    tab_ref[h, 0, 0] = intra
                else:
                    tab_ref[h, 0, 0] += intra
                tab_ref[h, d, 1] = jnp.exp(lgm * qpow)
                kpow_t = (c - 1.0) - jj if d == 0 else jj
                k_scale = jnp.where(ii.astype(jnp.int32) // RET_DK == h % 2, RET_DK ** -0.5, 0.0)
                tab_ref[h, d, 2] = jnp.exp(lgm * kpow_t) * k_scale
                tab_ref[h, d, 3] = jnp.exp(lgm * float(c))
                tab_ref[h, d, 4] = k_scale

    def head_slices(h):
        return slice((h // 2) * LANES, (h // 2 + 1) * LANES), slice(h * RET_DV, (h + 1) * RET_DV)

    chains = [(h, d) for h in range(RET_HEADS) for d in range(2)]

    def contributions(ci, carry):
        r0 = pl.multiple_of(ci * c, c)
        kt_all = k_ref[pl.ds(r0, c), :].astype(F32).T
        kt_ref[ci] = kt_all.astype(BF16)
        for h in range(RET_HEADS):
            grp, hs = head_slices(h)
            v = v_ref[pl.ds(r0, c), hs]
            for d in range(2):
                kd_t = (kt_all[grp, :] * tab_ref[h, d, 2]).astype(BF16)
                u_ref[2 * h + d, ci] = jnp.dot(kd_t, v, preferred_element_type=F32)
        return carry

    lax.fori_loop(0, nc, contributions, 0)

    for (h, d), s_ref in zip(chains, state_refs):
        s_ref[...] = s0_ref[h, d]

    def scan(ci, carry):
        for idx, (h, d) in enumerate(chains):
            cc = ci if d == 0 else nc - 1 - ci
            state = state_refs[idx][...]
            sin_ref[idx, cc] = state.astype(BF16)
            state_refs[idx][...] = tab_ref[h, d, 3] * state + u_ref[idx, cc]
        return carry

    lax.fori_loop(0, nc, scan, 0)
    for (h, d), s_ref in zip(chains, state_refs):
        st_ref[h, d] = s_ref[...]

    per_trip = 2

    def outputs(ti, carry):
        pairs = [(ti * per_trip + j, h) for j in range(per_trip) for h in range(RET_HEADS)]
        rows = [pl.ds(pl.multiple_of(ci * c, c), c) for ci, _ in pairs]
        qs = [q_ref[r, head_slices(h)[0]] for r, (_, h) in zip(rows, pairs)]
        vs = [v_ref[r, head_slices(h)[1]] for r, (_, h) in zip(rows, pairs)]
        raw = [jnp.dot(q, (kt_ref[ci, head_slices(h)[0], :].astype(F32)
                           * tab_ref[h, 0, 4]).astype(BF16), preferred_element_type=F32)
               for q, (ci, h) in zip(qs, pairs)]
        cross = [[jnp.dot(q, sin_ref[2 * h + d, ci], preferred_element_type=F32)
                  for d in range(2)] for q, (ci, h) in zip(qs, pairs)]
        intra = [jnp.dot((s * tab_ref[h, 0, 0]).astype(BF16), v, preferred_element_type=F32)
                 for s, v, (_, h) in zip(raw, vs, pairs)]
        for r, (_, h), o_in, o_cr in zip(rows, pairs, intra, cross):
            hs = head_slices(h)[1]
            o = o_in + o_cr[0] * tab_ref[h, 0, 1] + o_cr[1] * tab_ref[h, 1, 1]
            mu = jnp.mean(o, axis=-1, keepdims=True)
            var = jnp.mean(jnp.square(o - mu), axis=-1, keepdims=True)
            on = (o - mu) * lax.rsqrt(var + EPS) * gnw_ref[:, hs] + gnb_ref[:, hs]
            gate = gate_ref[r, hs].astype(F32)
            o_ref[r, hs] = (_silu(gate) * on).astype(BF16)
        return carry

    lax.fori_loop(0, nc // per_trip, outputs, 0)


def _retention(z, l, n, dec, s0, gn_w, gn_b):
    r = z.shape[0]
    w = RET_WIDTH
    qk = RET_HEADS * RET_DK
    nc = n // RET_CHUNK
    col = lambda base, width: pl.BlockSpec((n, width), lambda b: (b, base // width))
    st_spec = pl.BlockSpec((None, RET_HEADS, 2, LANES, RET_DV), lambda b: (b, 0, 0, 0, 0))
    gn_spec = pl.BlockSpec((None, 1, w), lambda b: (l, 0, 0))
    return pl.pallas_call(
        functools.partial(_ret_kernel, n=n),
        grid=(BATCH,),
        in_specs=[col(Z_RQ, qk), col(Z_RK, qk), col(Z_RV, w), col(Z_RGATE, w),
                  pl.BlockSpec((None, 2, RET_HEADS, 8, LANES), lambda b: (l, 0, 0, 0, 0)),
                  st_spec, gn_spec, gn_spec],
        out_specs=[pl.BlockSpec((n, w), lambda b: (b, 0)), st_spec],
        out_shape=[jax.ShapeDtypeStruct((r, w), BF16),
                   jax.ShapeDtypeStruct((BATCH, RET_HEADS, 2, LANES, RET_DV), F32)],
        scratch_shapes=[pltpu.VMEM((RET_HEADS, 2, 5, RET_CHUNK, RET_CHUNK), F32),
                        pltpu.VMEM((nc, qk, RET_CHUNK), BF16),
                        pltpu.VMEM((2 * RET_HEADS, nc, LANES, RET_DV), F32),
                        pltpu.VMEM((2 * RET_HEADS, nc, LANES, RET_DV), BF16)]
        + [pltpu.VMEM((LANES, RET_DV), F32)] * (2 * RET_HEADS),
        compiler_params=_cparams("arbitrary"),
        name="retention",
    )(z, z, z, z, dec, s0, gn_w, gn_b)


def _hy_filter_kernel(feat_ref, w1_ref, b1_ref, w2_ref, b2_ref, w3_ref, dl_ref, s_ref, d_ref):
    hi = lax.Precision.HIGHEST
    feats = feat_ref[...]
    h = jnp.sin(HY_SIN_FREQ * (jnp.dot(feats, w1_ref[...], preferred_element_type=F32, precision=hi)
                               + b1_ref[...]))
    h = jnp.sin(HY_SIN_FREQ * (jnp.dot(h, w2_ref[...], preferred_element_type=F32, precision=hi)
                               + b2_ref[...]))
    h = jnp.dot(h, w3_ref[...], preferred_element_type=F32, precision=hi)
    window = jnp.exp(-feats[:, 0:1] * dl_ref[...]) + HY_WINDOW_SHIFT
    window = jnp.concatenate([window] * HY_ORDER, axis=-1)
    half = HY_ORDER * HY_WIDTH
    h_fwd = h[:, :half] * window
    h_bwd = h[:, half:] * window
    s_ref[...] = (h_fwd + h_bwd).astype(BF16)
    d_ref[...] = (h_fwd - h_bwd).astype(BF16)


def _hy_filter(feats, w1, b1, w2, b2, w3, deltas, l):
    n = feats.shape[0]
    tn = 256
    half = HY_ORDER * HY_WIDTH
    full = lambda a, b: pl.BlockSpec((None, a, b), lambda i: (l, 0, 0))
    return pl.pallas_call(
        _hy_filter_kernel,
        grid=(n // tn,),
        in_specs=[pl.BlockSpec((tn, LANES), lambda i: (i, 0)),
                  full(LANES, HY_FFN), full(1, HY_FFN), full(HY_FFN, HY_FFN), full(1, HY_FFN),
                  full(HY_FFN, 2 * half),
                  pl.BlockSpec((1, HY_WIDTH), lambda i: (0, 0))],
        out_specs=[pl.BlockSpec((tn, half), lambda i: (i, 0))] * 2,
        out_shape=[jax.ShapeDtypeStruct((n, half), BF16)] * 2,
        compiler_params=_cparams("arbitrary"),
        name="hyena_filter",
    )(feats, w1, b1, w2, b2, w3, deltas)


def _hy_spec_kernel(fc_ref, fs_ref, alt_ref, s_ref, d_ref, hc_ref, hs_ref, hn_ref):
    s = s_ref[...]
    tk, n = fc_ref.shape
    row = lax.broadcasted_iota(jnp.int32, (tk, s.shape[1]), 0) + pl.program_id(0) * tk
    wk = jnp.where(row == 0, 1.0, 2.0) * (1.0 / (2 * n))
    hc_ref[...] = jnp.dot(fc_ref[...], s, preferred_element_type=F32) * wk
    hs_ref[...] = jnp.dot(fs_ref[...], d_ref[...], preferred_element_type=F32) * wk
    hn_ref[...] = jnp.dot(alt_ref[...], s, preferred_element_type=F32)[0:8, :] * (1.0 / (2 * n))


def _hy_spectrum(fc, fs, alt, s, d):
    n = fc.shape[0]
    tk = min(512, n)
    half = s.shape[1]
    out = pl.BlockSpec((tk, half), lambda i: (i, 0))
    return pl.pallas_call(
        _hy_spec_kernel,
        grid=(n // tk,),
        in_specs=[pl.BlockSpec((tk, n), lambda i: (i, 0)),
                  pl.BlockSpec((tk, n), lambda i: (i, 0)),
                  pl.BlockSpec((16, n), lambda i: (0, 0)),
                  pl.BlockSpec((n, half), lambda i: (0, 0)),
                  pl.BlockSpec((n, half), lambda i: (0, 0))],
        out_specs=[out, out, pl.BlockSpec((8, half), lambda i: (0, 0))],
        out_shape=[jax.ShapeDtypeStruct((n, half), F32)] * 2
        + [jax.ShapeDtypeStruct((8, half), F32)],
        compiler_params=_cparams("arbitrary"),
        name="hyena_spectrum",
    )(fc, fs, alt, s, d)


def _short_conv_kernel(u_ref, w_ref, b_ref, o_ref, pad_ref, *, n):
    zeros = jnp.zeros((8, pad_ref.shape[1]), F32)
    pad_ref[0:8, :] = zeros
    pad_ref[8 + n:16 + n, :] = zeros
    pad_ref[8:8 + n, :] = u_ref[...].astype(F32)
    w = w_ref[...]
    y = (pad_ref[7:7 + n, :] * w[0:1, :] + pad_ref[8:8 + n, :] * w[1:2, :]
         + pad_ref[9:9 + n, :] * w[2:3, :] + b_ref[...])
    o_ref[...] = y.astype(BF16)


def _short_conv(z, l, n, w, b):
    r = z.shape[0]
    wc = HY_WIDTH
    return pl.pallas_call(
        functools.partial(_short_conv_kernel, n=n),
        grid=(BATCH, (HY_ORDER + 1)),
        in_specs=[pl.BlockSpec((n, wc), lambda bi, j: (bi, Z_HY // wc + j)),
                  pl.BlockSpec((None, 3, wc), lambda bi, j: (l, 0, j)),
                  pl.BlockSpec((None, 1, wc), lambda bi, j: (l, 0, j))],
        out_specs=pl.BlockSpec((n, wc), lambda bi, j: (bi, j)),
        out_shape=jax.ShapeDtypeStruct((r, (HY_ORDER + 1) * wc), BF16),
        scratch_shapes=[pltpu.VMEM((n + 16, wc), F32)],
        compiler_params=_cparams("arbitrary", "arbitrary"),
        name="short_conv",
    )(z, w, b)


HY_PAIR = 2


def _long_conv_kernel(u_ref, gate_ref, fc_ref, fs_ref, g_ref, alt_ref, hc_ref, hs_ref, hn_ref,
                      skip_ref, gain_ref, o_ref, y_ref, nyq_ref, *, n, tk, final):
    j = pl.program_id(1)
    kt = n // tk

    @pl.when(j == 0)
    def _():
        for p in range(HY_PAIR):
            x_nyq = jnp.dot(alt_ref[...], u_ref[p], preferred_element_type=F32)
            nyq_ref[p] = x_nyq[0:8, :] * hn_ref[...]

    @pl.when(j < kt)
    def _():
        r0 = pl.multiple_of(j * tk, tk)
        hc, hs = hc_ref[...], hs_ref[...]
        for p in range(HY_PAIR):
            u = u_ref[p]
            xc = jnp.dot(fc_ref[...], u, preferred_element_type=F32)
            xs = jnp.dot(fs_ref[...], u, preferred_element_type=F32)
            y_ref[p, pl.ds(r0, tk), :] = (xc * hc - xs * hs).astype(BF16)
            y_ref[p, pl.ds(n + r0, tk), :] = (xc * hs + xs * hc).astype(BF16)

    @pl.when(j >= kt)
    def _():
        t0 = pl.multiple_of((j - kt) * tk, tk)
        t = lax.broadcasted_iota(jnp.int32, (tk, u_ref.shape[-1]), 0)
        alt = (1 - 2 * (t & 1)).astype(F32)
        for p in range(HY_PAIR):
            conv = jnp.dot(g_ref[...], y_ref[p], preferred_element_type=F32)
            u = u_ref[p, pl.ds(t0, tk), :].astype(F32)
            y = conv + alt * nyq_ref[p][0:1, :] + u * skip_ref[...]
            y = gate_ref[p].astype(F32) * y
            if final:
                y = _rms(y) * gain_ref[...]
            o_ref[p] = y.astype(BF16)


def _long_conv(src, u_col, gate_col, mats, spec, order, skip, gain, l, n, final):
    fc, fs, g, alt = mats
    hc, hs, hn = spec
    r = src[0].shape[0]
    wc = HY_WIDTH
    tk = min(512, n)
    kt = n // tk
    nb = BATCH // HY_PAIR
    u4, gate4 = (a.reshape(nb, HY_PAIR, n, a.shape[-1]) for a in src)
    f_spec = pl.BlockSpec((tk, n), lambda bi, j: (jnp.minimum(j, kt - 1), 0))
    h_spec = pl.BlockSpec((tk, wc), lambda bi, j: (jnp.minimum(j, kt - 1), order))
    tile = lambda col: pl.BlockSpec((None, HY_PAIR, tk, wc),
                                    lambda bi, j: (bi, 0, jnp.maximum(j - kt, 0), col))
    out = pl.pallas_call(
        functools.partial(_long_conv_kernel, n=n, tk=tk, final=final),
        grid=(nb, 2 * kt),
        in_specs=[pl.BlockSpec((None, HY_PAIR, n, wc), lambda bi, j: (bi, 0, 0, u_col)),
                  tile(gate_col),
                  f_spec, f_spec,
                  pl.BlockSpec((tk, 2 * n), lambda bi, j: (jnp.maximum(j - kt, 0), 0)),
                  pl.BlockSpec((16, n), lambda bi, j: (0, 0)),
                  h_spec, h_spec,
                  pl.BlockSpec((8, wc), lambda bi, j: (0, order)),
                  pl.BlockSpec((None, None, 1, wc), lambda bi, j: (l, order, 0, 0)),
                  pl.BlockSpec((None, 1, wc), lambda bi, j: (l, 0, 0))],
        out_specs=tile(0),
        out_shape=jax.ShapeDtypeStruct((nb, HY_PAIR, n, wc), BF16),
        scratch_shapes=[pltpu.VMEM((HY_PAIR, 2 * n, wc), BF16),
                        pltpu.VMEM((HY_PAIR, 8, wc), F32)],
        compiler_params=_cparams("arbitrary", "arbitrary"),
        name="long_conv",
    )(u4, gate4, fc, fs, g, alt, hc, hs, hn, skip, gain)
    return out.reshape(r, wc)


def _dft_mats(n):
    big = 2 * n
    fine = 32
    t = jnp.arange(n, dtype=jnp.int32)

    def trig(k):
        ang = ((k[:, None] * t[None, :]) % big).astype(F32) * (2.0 * math.pi / big)
        return jnp.cos(ang), jnp.sin(ang)

    ca, sa = (m[:, None, :] for m in trig(fine * jnp.arange(n // fine, dtype=jnp.int32)))
    cb, sb = (m[None, :, :] for m in trig(jnp.arange(fine, dtype=jnp.int32)))
    fc = (ca * cb - sa * sb).reshape(n, n).astype(BF16)
    fs = (sa * cb + ca * sb).reshape(n, n).astype(BF16)
    alt = jnp.broadcast_to((1 - 2 * (t % 2)).astype(F32)[None, :], (16, n))
    return fc, fs, jnp.concatenate([fc, fs], axis=1), alt.astype(BF16)


def _hy_features(n):
    t = jnp.linspace(0.0, 1.0, n, dtype=F32)[:, None]
    bands = (HY_EMB - 1) // 2
    w = 2 * math.pi * jnp.arange(n, dtype=F32)[:, None] / n
    f = jnp.linspace(1e-4, bands - 1, bands, dtype=F32)[None, :]
    feats = jnp.concatenate([t, jnp.cos(f * w), -jnp.sin(f * w)], axis=-1)
    return jnp.pad(feats, ((0, 0), (0, LANES - HY_EMB)))


def _rope_table():
    n_freq = MLA_ROPE // 4
    inv_freq = ROPE_BASE ** (-jnp.arange(n_freq, dtype=F32) / n_freq)
    rows = SEQ // GRID_W
    row = jnp.repeat(jnp.arange(rows), GRID_W).astype(F32)
    col = jnp.tile(jnp.arange(GRID_W), rows).astype(F32)
    ar = row[:, None] * inv_freq[None, :]
    ac = col[:, None] * inv_freq[None, :]
    cos_part = jnp.concatenate([jnp.cos(ar), jnp.cos(ar), jnp.cos(ac), jnp.cos(ac)], axis=-1)
    sin_part = jnp.concatenate([-jnp.sin(ar), jnp.sin(ar), -jnp.sin(ac), jnp.sin(ac)], axis=-1)
    return jnp.concatenate([cos_part, sin_part], axis=-1)


_ROPE_PARTNER = np.concatenate([np.arange(16, 32), np.arange(0, 16),
                                np.arange(48, 64), np.arange(32, 48)])
_QUARTER = MLA_ROPE // 4


def _partner_pieces(base):
    return [(base + int(_ROPE_PARTNER[i]), _QUARTER) for i in range(0, MLA_ROPE, _QUARTER)]


def _w_in_pieces():
    off = np.cumsum([0, MLA_KV_LORA, MLA_ROPE, RET_HEADS * RET_DK, RET_WIDTH, MLA_Q_LORA,
                     RET_HEADS * RET_DK, RET_WIDTH])
    o_kv, o_kr, o_rk, o_rv, o_q, o_rq, o_gate, o_hy = [int(v) for v in off]
    pieces = [(o_hy, 3 * HY_WIDTH), (o_rv, RET_WIDTH), (o_q, MLA_Q_LORA), (o_gate, RET_WIDTH),
              (o_rq, RET_HEADS * RET_DK), (o_rk, RET_HEADS * RET_DK),
              (o_kv, MLA_KV_LORA), (o_kr, MLA_ROPE)] + _partner_pieces(o_kr)
    pieces.append((None, Z_COLS - sum(w for _, w in pieces)))
    return pieces


def _wuq_pieces():
    per = MLA_NOPE + MLA_ROPE
    pieces = []
    for h in range(MLA_HEADS):
        pieces += [(h * per, per)] + _partner_pieces(h * per + MLA_NOPE)
    return pieces


def _relayout_kernel(w_ref, o_ref, *, pieces):
    col = 0
    for start, width in pieces:
        if start is None:
            o_ref[:, col:col + width] = jnp.zeros((o_ref.shape[0], width), o_ref.dtype)
        else:
            o_ref[:, col:col + width] = w_ref[:, start:start + width].astype(o_ref.dtype)
        col += width


def _relayout_cols(w, pieces):
    nl, k, n_in = w.shape
    n_out = sum(width for _, width in pieces)
    tr = 256
    return pl.pallas_call(
        functools.partial(_relayout_kernel, pieces=tuple(pieces)),
        grid=(nl, k // tr),
        in_specs=[pl.BlockSpec((None, tr, n_in), lambda l, i: (l, i, 0))],
        out_specs=pl.BlockSpec((None, tr, n_out), lambda l, i: (l, i, 0)),
        out_shape=jax.ShapeDtypeStruct((nl, k, n_out), BF16),
        compiler_params=_cparams("arbitrary", "arbitrary"),
        name="relayout_cols",
    )(w)


def kernel(x, c, ctx, c_ctx, ada_w, ada_b, ffn1_gate, ffn1_up, ffn1_down, w_in, mla_q_norm, mla_wuq, mla_kv_norm, mla_wukv, mla_qn_nope, mla_qn_rope, mla_kn_nope, mla_kn_rope, mla_out_norm, hy_conv_w, hy_conv_b, hy_ffn_w1, hy_ffn_b1, hy_ffn_w2, hy_ffn_b2, hy_ffn_w3, hy_skip, hy_out_norm, ret_decay, ret_gn_w, ret_gn_b, w_out, ffn2_gate, ffn2_up, ffn2_down):
    nl = DEPTH
    d = D_MODEL

    w1g, w1u, w1d = ffn1_gate, ffn1_up, ffn1_down.astype(BF16)
    w2g, w2u, w2d = ffn2_gate, ffn2_up, ffn2_down.astype(BF16)
    w_in_p = _relayout_cols(w_in, _w_in_pieces())
    wuq_p = _relayout_cols(mla_wuq, _wuq_pieces())
    wukv = mla_wukv.astype(BF16)
    w_out_b = w_out.astype(BF16)
    row3 = lambda g: g.reshape(nl, 1, -1)
    partner = jnp.asarray(_ROPE_PARTNER)
    g_qr = row3(jnp.concatenate([mla_qn_rope, mla_qn_rope[:, partner]], axis=-1))
    g_kr = row3(jnp.concatenate([mla_kn_rope, mla_kn_rope[:, partner]], axis=-1))
    g_q, g_kv, g_qn, g_kn = row3(mla_q_norm), row3(mla_kv_norm), row3(mla_qn_nope), row3(mla_kn_nope)
    g_attn, g_hy = row3(mla_out_norm), row3(hy_out_norm)
    gn_w, gn_b = row3(ret_gn_w), row3(ret_gn_b)
    hy_b = row3(hy_conv_b)
    hy_w1 = jnp.pad(hy_ffn_w1, ((0, 0), (0, LANES - HY_EMB), (0, 0)))
    hy_b1, hy_b2 = row3(hy_ffn_b1), row3(hy_ffn_b2)
    hy_skip4 = hy_skip.reshape(nl, HY_ORDER, 1, HY_WIDTH)
    dec = jnp.broadcast_to(ret_decay.astype(F32)[:, :, :, None, None], (nl, 2, RET_HEADS, 8, LANES))

    tab_x = _rope_table()
    tab_c = jnp.concatenate([jnp.ones((CTX_LEN, MLA_ROPE), F32), jnp.zeros((CTX_LEN, MLA_ROPE), F32)], -1)
    deltas = jnp.abs(jnp.linspace(HY_MIN_DECAY, HY_MAX_DECAY, HY_WIDTH, dtype=F32))[None, :]
    mats = {n: _dft_mats(n) for n in (SEQ, CTX_LEN)}
    feats = {n: _hy_features(n) for n in (SEQ, CTX_LEN)}

    cond = jnp.zeros((16, d), F32).at[0].set(c_ctx).at[1:1 + BATCH].set(c)
    mods = _modulation(cond, ada_w, ada_b).reshape(nl, 16, N_MOD, d)

    sx = x.reshape(BATCH * SEQ, d)
    sc = ctx.reshape(BATCH * CTX_LEN, d)

    def mod_tables(l):
        return ([mods[l, 1:1 + BATCH, j][:, None, :] for j in range(N_MOD)],
                [mods[l, 0:1, j][:, None, :] for j in range(N_MOD)])

    all_mods = [mod_tables(l) for l in range(nl)]

    def ffn(s, h, n, gate, wg, wu, wd, l):
        return _ffn_down(_ffn_up(h, wg, wu, l), wd, l, s, gate, n)

    def hyena(z, l, n, spec):
        cv = _short_conv(z, l, n, hy_conv_w, hy_b)
        y1 = _long_conv((cv, cv), 2, 0, mats[n], spec, 0, hy_skip4, g_hy, l, n, False)
        return _long_conv((y1, cv), 0, 1, mats[n], spec, 1, hy_skip4, g_hy, l, n, True)

    for l in range(nl):
        need_ctx_out = l < nl - 1
        mod_x, mod_c = all_mods[l]

        sx = ffn(sx, _normmod(sx, mod_x[0], mod_x[1], SEQ), SEQ, mod_x[2], w1g, w1u, w1d, l)
        sc = ffn(sc, _normmod(sc, mod_c[0], mod_c[1], CTX_LEN), CTX_LEN, mod_c[2], w1g, w1u, w1d, l)

        zc = _norm_matmul(sc, mod_c[3], mod_c[4], CTX_LEN, w_in_p, l)
        zx = _norm_matmul(sx, mod_x[3], mod_x[4], SEQ, w_in_p, l)

        kc, vc = _kv_proj(zc, l, CTX_LEN, wukv, g_kv, g_kn, g_kr, tab_c)
        kx, vx = _kv_proj(zx, l, SEQ, wukv, g_kv, g_kn, g_kr, tab_x)
        zero_state = jnp.zeros((BATCH, RET_HEADS, 2, LANES, RET_DV), F32)
        ret_c, state_c = _retention(zc, l, CTX_LEN, dec, zero_state, gn_w, gn_b)

        filt = {}
        for n in ((SEQ, CTX_LEN) if need_ctx_out else (SEQ,)):
            s_f, d_f = _hy_filter(feats[n], hy_w1, hy_b1, hy_ffn_w2, hy_b2, hy_ffn_w3, deltas, l)
            fc, fs, _, alt = mats[n]
            filt[n] = _hy_spectrum(fc, fs, alt, s_f, d_f)

        qx = _q_proj(zx, l, SEQ, wuq_p, g_q, g_qn, g_qr, tab_x)
        attn_x = _attention(qx, [kc, kx], [vc, vx], [CTX_LEN, SEQ], l, g_attn)
        hy_x = hyena(zx, l, SEQ, filt[SEQ])
        ret_x, _ = _retention(zx, l, SEQ, dec, state_c, gn_w, gn_b)
        sx, hx = _w_out([attn_x, hy_x, ret_x], w_out_b, l, sx, mod_x[5], SEQ, (mod_x[6], mod_x[7]))
        sx = ffn(sx, hx, SEQ, mod_x[8], w2g, w2u, w2d, l)

        if need_ctx_out:
            qc = _q_proj(zc, l, CTX_LEN, wuq_p, g_q, g_qn, g_qr, tab_c)
            attn_c = _attention(qc, [kc], [vc], [CTX_LEN], l, g_attn)
            hy_c = hyena(zc, l, CTX_LEN, filt[CTX_LEN])
            sc, hc = _w_out([attn_c, hy_c, ret_c], w_out_b, l, sc, mod_c[5], CTX_LEN,
                            (mod_c[6], mod_c[7]))
            sc = ffn(sc, hc, CTX_LEN, mod_c[8], w2g, w2u, w2d, l)

    return sx.reshape(BATCH, SEQ, d)
```

```python
import functools
import math

import jax
import jax.numpy as jnp
import numpy as np
from jax import lax
from jax.experimental import pallas as pl
from jax.experimental.pallas import tpu as pltpu

F32 = jnp.float32
BF16 = jnp.bfloat16

D_MODEL = 2048
BATCH = 8
SEQ = 2048
DEPTH = 2
GRID_W = 64
CTX_LEN = 256
EPS = 1e-6
N_MOD = 9
FFN_HIDDEN = 5632

MLA_HEADS = 8
MLA_NOPE = 128
MLA_ROPE = 64
MLA_V = 128
MLA_Q_LORA = 512
MLA_KV_LORA = 256
MLA_WIDTH = MLA_HEADS * MLA_V
ROPE_BASE = 10000.0

HY_WIDTH = 512
HY_ORDER = 2
HY_EMB = 33
HY_FFN = 64
HY_SIN_FREQ = 1.0
HY_MIN_DECAY = math.log(1e-2) / 1.5
HY_MAX_DECAY = math.log(1e-2) / 0.3
HY_WINDOW_SHIFT = 0.05

RET_HEADS = 4
RET_DK = 64
RET_DV = 128
RET_WIDTH = RET_HEADS * RET_DV
RET_CHUNK = 128

VMEM_LIMIT_BYTES = 60 * 1024 * 1024
LANES = 128
HEAD_SLOT = 256

Z_HY = 0
Z_RV = 1536
Z_QLAT = 2048
Z_RGATE = 2560
Z_RQ = 3072
Z_RK = 3328
Z_KVLAT = 3584
Z_KR = 3840
Z_COLS = 4096

TM = 1024
TM_WIDE = 2048
TN = 512


def _cparams(*sem):
    return pltpu.CompilerParams(dimension_semantics=sem, vmem_limit_bytes=VMEM_LIMIT_BYTES)


def _rms(x, width=None):
    width = x.shape[-1] if width is None else width
    return x * lax.rsqrt(jnp.sum(x * x, axis=-1, keepdims=True) / width + EPS)


def _silu(x):
    return x * jax.nn.sigmoid(x)


def _mod_kernel(c_ref, w_ref, b_ref, o_ref):
    a = _silu(c_ref[...])
    w = w_ref[...]
    a_hi, w_hi = a.astype(BF16), w.astype(BF16)
    a_lo = (a - a_hi.astype(F32)).astype(BF16)
    w_lo = (w - w_hi.astype(F32)).astype(BF16)
    rows = a.shape[0]
    acc = jnp.dot(jnp.concatenate([a_hi, a_lo], axis=0), w_hi, preferred_element_type=F32)
    acc = acc[:rows] + acc[rows:] + jnp.dot(a_hi, w_lo, preferred_element_type=F32)
    o_ref[...] = acc + b_ref[...]


def _modulation(cond, ada_w, ada_b):
    nl, d, nout = ada_w.shape
    bn = 1024
    return pl.pallas_call(
        _mod_kernel,
        grid=(nl, nout // bn),
        in_specs=[pl.BlockSpec((16, d), lambda l, j: (0, 0)),
                  pl.BlockSpec((None, d, bn), lambda l, j: (l, 0, j)),
                  pl.BlockSpec((None, 1, bn), lambda l, j: (l, 0, j))],
        out_specs=pl.BlockSpec((None, 16, bn), lambda l, j: (l, 0, j)),
        out_shape=jax.ShapeDtypeStruct((nl, 16, nout), F32),
        compiler_params=_cparams("arbitrary", "arbitrary"),
        name="modulation",
    )(cond, ada_w, ada_b.reshape(nl, 1, nout))


def _normmod_kernel(x_ref, sh_ref, sc_ref, o_ref):
    y = _rms(x_ref[...])
    o_ref[...] = (y * (1.0 + sc_ref[...]) + sh_ref[...]).astype(BF16)


def _normmod(s, shift, scale, n):
    r, d = s.shape
    tm = min(512, n)
    per_batch = shift.shape[0] > 1
    mod_spec = pl.BlockSpec((None, 1, d), lambda i: ((i * tm) // n if per_batch else 0, 0, 0))
    return pl.pallas_call(
        _normmod_kernel,
        grid=(r // tm,),
        in_specs=[pl.BlockSpec((tm, d), lambda i: (i, 0)), mod_spec, mod_spec],
        out_specs=pl.BlockSpec((tm, d), lambda i: (i, 0)),
        out_shape=jax.ShapeDtypeStruct((r, d), BF16),
        compiler_params=_cparams("arbitrary"),
        name="normmod",
    )(s, shift, scale)


def _ffn_up_kernel(h_ref, wg_ref, wu_ref, o_ref, wg_bf, wu_bf):
    @pl.when(pl.program_id(1) == 0)
    def _():
        wg_bf[...] = wg_ref[...].astype(BF16)
        wu_bf[...] = wu_ref[...].astype(BF16)

    h = h_ref[...]
    g = jnp.dot(h, wg_bf[...], preferred_element_type=F32)
    u = jnp.dot(h, wu_bf[...], preferred_element_type=F32)
    o_ref[...] = (_silu(g) * u).astype(BF16)


def _ffn_up(h, wg, wu, l):
    r, d = h.shape
    f = wg.shape[-1]
    w_spec = pl.BlockSpec((None, d, TN), lambda j, i: (l, 0, j))
    tm = TM_WIDE
    return pl.pallas_call(
        _ffn_up_kernel,
        grid=(f // TN, r // tm),
        in_specs=[pl.BlockSpec((tm, d), lambda j, i: (i, 0)), w_spec, w_spec],
        out_specs=pl.BlockSpec((tm, TN), lambda j, i: (i, j)),
        out_shape=jax.ShapeDtypeStruct((r, f), BF16),
        scratch_shapes=[pltpu.VMEM((d, TN), BF16), pltpu.VMEM((d, TN), BF16)],
        compiler_params=_cparams("arbitrary", "arbitrary"),
        name="ffn_up",
    )(h, wg, wu)


def _mod_spec(table, tm, n, d):
    per_batch = table.shape[0] > 1
    return pl.BlockSpec((None, 1, d), lambda i, *_: ((i * tm) // n if per_batch else 0, 0, 0))


def _norm_mm_kernel(s_ref, sh_ref, sc_ref, w_ref, o_ref, h_ref):
    @pl.when(pl.program_id(1) == 0)
    def _():
        h_ref[...] = (_rms(s_ref[...]) * (1.0 + sc_ref[...]) + sh_ref[...]).astype(BF16)

    o_ref[...] = lax.dot_general(h_ref[...], w_ref[...], (((1,), (1,)), ((), ())),
                                 preferred_element_type=F32).astype(o_ref.dtype)


def _norm_matmul(s, shift, scale, n, w_t, l):
    r, d = s.shape
    nout = w_t.shape[1]
    tm = TM_WIDE
    return pl.pallas_call(
        _norm_mm_kernel,
        grid=(r // tm, nout // TN),
        in_specs=[pl.BlockSpec((tm, d), lambda i, j: (i, 0)),
                  _mod_spec(shift, tm, n, d), _mod_spec(scale, tm, n, d),
                  pl.BlockSpec((None, TN, d), lambda i, j: (l, j, 0))],
        out_specs=pl.BlockSpec((tm, TN), lambda i, j: (i, j)),
        out_shape=jax.ShapeDtypeStruct((r, nout), BF16),
        scratch_shapes=[pltpu.VMEM((tm, d), BF16)],
        compiler_params=_cparams("arbitrary", "arbitrary"),
        name="norm_matmul",
    )(s, shift, scale, w_t)


def _ffn_down_kernel(u_ref, w_ref, s_ref, g_ref, o_ref):
    acc = jnp.dot(u_ref[...], w_ref[...], preferred_element_type=F32)
    o_ref[...] = s_ref[...] + (0.5 * g_ref[...]) * acc


def _ffn_down(u, w, l, s, gate, n):
    r, d = s.shape
    f = u.shape[1]
    per_batch = gate.shape[0] > 1
    return pl.pallas_call(
        _ffn_down_kernel,
        grid=(r // TM, d // TN),
        in_specs=[pl.BlockSpec((TM, f), lambda i, j: (i, 0)),
                  pl.BlockSpec((None, f, TN), lambda i, j: (l, 0, j)),
                  pl.BlockSpec((TM, TN), lambda i, j: (i, j)),
                  pl.BlockSpec((None, 1, TN), lambda i, j: ((i * TM) // n if per_batch else 0, 0, j))],
        out_specs=pl.BlockSpec((TM, TN), lambda i, j: (i, j)),
        out_shape=jax.ShapeDtypeStruct((r, d), F32),
        compiler_params=_cparams("arbitrary", "arbitrary"),
        name="ffn_down",
    )(u, w, s, gate)


def _residual_epilogue(s_ref, g_ref, sh_ref, sc_ref, o_ref, h_ref, acc, coeff):
    s_new = s_ref[...] + (coeff * g_ref[...]) * acc
    o_ref[...] = s_new
    if h_ref is not None:
        h_ref[...] = (_rms(s_new) * (1.0 + sc_ref[...]) + sh_ref[...]).astype(BF16)


def _w_out_kernel(*refs, n_in, emit_h):
    a_refs = refs[:n_in]
    w_ref, s_ref, g_ref, sh_ref, sc_ref, o_ref = refs[n_in:n_in + 6]
    h_ref = refs[n_in + 6] if emit_h else None
    acc, row = None, 0
    for a_ref in a_refs:
        kp = a_ref.shape[1]
        part = jnp.dot(a_ref[...], w_ref[row:row + kp, :], preferred_element_type=F32)
        acc = part if acc is None else acc + part
        row += kp
    _residual_epilogue(s_ref, g_ref, sh_ref, sc_ref, o_ref, h_ref, acc, 1.0)


def _w_out(parts, w, l, s, gate, n, next_mod):
    r, d = s.shape
    tm = TM // 2
    emit_h = next_mod is not None
    shift, scale = next_mod if emit_h else (gate, gate)
    row = pl.BlockSpec((tm, d), lambda i: (i, 0))
    outs = pl.pallas_call(
        functools.partial(_w_out_kernel, n_in=len(parts), emit_h=emit_h),
        grid=(r // tm,),
        in_specs=[pl.BlockSpec((tm, p.shape[1]), lambda i: (i, 0)) for p in parts]
        + [pl.BlockSpec((None, d, d), lambda i: (l, 0, 0)),
           row, _mod_spec(gate, tm, n, d), _mod_spec(shift, tm, n, d), _mod_spec(scale, tm, n, d)],
        out_specs=[row, row] if emit_h else [row],
        out_shape=[jax.ShapeDtypeStruct((r, d), F32)]
        + ([jax.ShapeDtypeStruct((r, d), BF16)] if emit_h else []),
        compiler_params=_cparams("arbitrary"),
        name="w_out",
    )(*parts, w, s, gate, shift, scale)
    return (outs[0], outs[1]) if emit_h else (outs[0], None)


def _rope_lanes(x):
    lane = lax.broadcasted_iota(jnp.int32, x.shape, 1)
    ms = jnp.sum(jnp.where(lane < MLA_ROPE, x * x, 0.0), axis=-1, keepdims=True) / MLA_ROPE
    return x * lax.rsqrt(ms + EPS)


def _kv_kernel(lat_ref, kr_ref, gkv_ref, w_ref, gkn_ref, gkr_ref, tab_ref, k_ref, v_ref):
    a = _rms(lat_ref[...].astype(F32)) * gkv_ref[...]
    kv = jnp.dot(a.astype(BF16), w_ref[...], preferred_element_type=F32)
    t = _rope_lanes(kr_ref[...].astype(F32)) * gkr_ref[...] * tab_ref[...]
    krot = (t + pltpu.roll(t, MLA_ROPE, 1)).astype(BF16)
    for h in range(MLA_HEADS):
        c0 = h * HEAD_SLOT
        kn = _rms(kv[:, c0:c0 + MLA_NOPE]) * gkn_ref[...]
        k_ref[:, c0:c0 + MLA_NOPE] = kn.astype(BF16)
        k_ref[:, c0 + MLA_NOPE:c0 + HEAD_SLOT] = krot
        v_ref[:, c0:c0 + MLA_V] = kv[:, c0 + MLA_NOPE:c0 + HEAD_SLOT].astype(BF16)
        v_ref[:, c0 + MLA_V:c0 + HEAD_SLOT] = jnp.ones((kv.shape[0], HEAD_SLOT - MLA_V), BF16)


def _kv_proj(z, l, n, wukv, g_kv, g_kn, g_kr, tab):
    r = z.shape[0]
    tm = min(512, n)
    nt = tab.shape[0] // tm
    vec = lambda w: pl.BlockSpec((None, 1, w), lambda i: (l, 0, 0))
    return pl.pallas_call(
        _kv_kernel,
        grid=(r // tm,),
        in_specs=[pl.BlockSpec((tm, MLA_KV_LORA), lambda i: (i, Z_KVLAT // MLA_KV_LORA)),
                  pl.BlockSpec((tm, LANES), lambda i: (i, Z_KR // LANES)),
                  vec(MLA_KV_LORA),
                  pl.BlockSpec((None, MLA_KV_LORA, MLA_HEADS * HEAD_SLOT), lambda i: (l, 0, 0)),
                  vec(MLA_NOPE), vec(LANES),
                  pl.BlockSpec((tm, LANES), lambda i: (i % nt, 0))],
        out_specs=[pl.BlockSpec((tm, MLA_HEADS * HEAD_SLOT), lambda i: (i, 0)),
                   pl.BlockSpec((tm, MLA_HEADS * HEAD_SLOT), lambda i: (i, 0))],
        out_shape=[jax.ShapeDtypeStruct((r, MLA_HEADS * HEAD_SLOT), BF16),
                   jax.ShapeDtypeStruct((r, MLA_HEADS * HEAD_SLOT), BF16)],
        compiler_params=_cparams("arbitrary"),
        name="kv_proj",
    )(z, z, g_kv, wukv, g_kn, g_kr, tab)


def _q_kernel(lat_ref, gq_ref, w_ref, gn_ref, gr_ref, tab_ref, q_ref, *, scale):
    a = _rms(lat_ref[...].astype(F32)) * gq_ref[...]
    q = jnp.dot(a.astype(BF16), w_ref[...], preferred_element_type=F32)
    rope_scale = gr_ref[...] * tab_ref[...] * scale
    for h in range(MLA_HEADS):
        c0 = h * HEAD_SLOT
        qn = _rms(q[:, c0:c0 + MLA_NOPE]) * (gn_ref[...] * scale)
        q_ref[:, c0:c0 + MLA_NOPE] = qn.astype(BF16)
        qr = _rope_lanes(q[:, c0 + MLA_NOPE:c0 + HEAD_SLOT]) * rope_scale
        q_ref[:, c0 + MLA_NOPE:c0 + HEAD_SLOT] = qr.astype(BF16)


def _q_proj(z, l, n, wuq, g_q, g_qn, g_qr, tab):
    r = z.shape[0]
    tm = min(512, n)
    nt = tab.shape[0] // tm
    vec = lambda w: pl.BlockSpec((None, 1, w), lambda i: (l, 0, 0))
    return pl.pallas_call(
        functools.partial(_q_kernel, scale=(MLA_NOPE + MLA_ROPE) ** -0.5 * math.log2(math.e)),
        grid=(r // tm,),
        in_specs=[pl.BlockSpec((tm, MLA_Q_LORA), lambda i: (i, Z_QLAT // MLA_Q_LORA)),
                  vec(MLA_Q_LORA),
                  pl.BlockSpec((None, MLA_Q_LORA, MLA_HEADS * HEAD_SLOT), lambda i: (l, 0, 0)),
                  vec(MLA_NOPE), vec(LANES),
                  pl.BlockSpec((tm, LANES), lambda i: (i % nt, 0))],
        out_specs=pl.BlockSpec((tm, MLA_HEADS * HEAD_SLOT), lambda i: (i, 0)),
        out_shape=jax.ShapeDtypeStruct((r, MLA_HEADS * HEAD_SLOT), BF16),
        compiler_params=_cparams("arbitrary"),
        name="q_proj",
    )(z, g_q, wuq, g_qn, g_qr, tab)


def _attn_kernel(*refs, n_src):
    q_ref = refs[0]
    k_refs, v_refs = refs[1:1 + n_src], refs[1 + n_src:1 + 2 * n_src]
    g_ref, o_ref, acc_ref = refs[1 + 2 * n_src:]
    def scores(h):
        hs = slice(h * HEAD_SLOT, (h + 1) * HEAD_SLOT)
        return [lax.dot_general(q_ref[:, hs], k_ref[:, hs], (((1,), (1,)), ((), ())),
                                preferred_element_type=F32) for k_ref in k_refs]

    ssq = None
    ss_next = scores(0)
    for h in range(MLA_HEADS):
        hs = slice(h * HEAD_SLOT, (h + 1) * HEAD_SLOT)
        ss, ss_next = ss_next, (scores(h + 1) if h + 1 < MLA_HEADS else None)
        m = functools.reduce(jnp.maximum, [jnp.max(s, axis=-1, keepdims=True) for s in ss])
        o = functools.reduce(jnp.add, [
            jnp.dot(jnp.exp2((s - m).astype(BF16)), v_ref[:, hs], preferred_element_type=F32)
            for s, v_ref in zip(ss, v_refs)])
        o = o[:, :MLA_V] / o[:, MLA_V:MLA_V + 1]
        acc_ref[:, h * MLA_V:(h + 1) * MLA_V] = o
        sq = jnp.sum(o * o, axis=-1, keepdims=True)
        ssq = sq if ssq is None else ssq + sq
    o_ref[...] = (acc_ref[...] * lax.rsqrt(ssq / MLA_WIDTH + EPS) * g_ref[...]).astype(BF16)


def _attention(q, ks, vs, ns, l, g_out):
    r = q.shape[0]
    nq = r // BATCH
    tq = 256
    per_b = nq // tq
    k_specs = [pl.BlockSpec((nk, MLA_HEADS * HEAD_SLOT), lambda b, i: (b, 0)) for nk in ns]
    v_specs = k_specs
    return pl.pallas_call(
        functools.partial(_attn_kernel, n_src=len(ks)),
        grid=(BATCH, per_b),
        in_specs=[pl.BlockSpec((tq, MLA_HEADS * HEAD_SLOT), lambda b, i: (b * per_b + i, 0))]
        + k_specs + v_specs
        + [pl.BlockSpec((None, 1, MLA_WIDTH), lambda b, i: (l, 0, 0))],
        out_specs=pl.BlockSpec((tq, MLA_WIDTH), lambda b, i: (b * per_b + i, 0)),
        out_shape=jax.ShapeDtypeStruct((r, MLA_WIDTH), BF16),
        scratch_shapes=[pltpu.VMEM((tq, MLA_WIDTH), F32)],
        compiler_params=_cparams("arbitrary", "arbitrary"),
        name="attention",
    )(q, *ks, *vs, g_out)


def _log_sigmoid(x):
    return jnp.minimum(x, 0.0) - jnp.log(1.0 + jnp.exp(-jnp.abs(x)))


def _ret_kernel(q_ref, k_ref, v_ref, gate_ref, dec_ref, s0_ref, gnw_ref, gnb_ref,
                o_ref, st_ref, tab_ref, kt_ref, u_ref, sin_ref, *state_refs, n):
    nc = n // RET_CHUNK
    c = RET_CHUNK

    @pl.when(pl.program_id(0) == 0)
    def _():
        ii = lax.broadcasted_iota(jnp.int32, (c, c), 0).astype(F32)
        jj = lax.broadcasted_iota(jnp.int32, (c, c), 1).astype(F32)
        for h in range(RET_HEADS):
            for d in range(2):
                lg = _log_sigmoid(dec_ref[d, h][0:1, :])
                lgm = jnp.broadcast_to(lg, (c, c))
                if d == 0:
                    diff, qpow, kpow = ii - jj, ii + 1.0, (c - 1.0) - ii
                else:
                    diff, qpow, kpow = jj - ii, c - ii, ii
                intra = jnp.where(diff >= 0, jnp.exp(lgm * jnp.maximum(diff, 0.0)), 0.0)
                if d == 0:
                    tab_ref[h, 0, 0] = intra
                else:
                    tab_ref[h, 0, 0] += intra
                tab_ref[h, d, 1] = jnp.exp(lgm * qpow)
                kpow_t = (c - 1.0) - jj if d == 0 else jj
                k_scale = jnp.where(ii.astype(jnp.int32) // RET_DK == h % 2, RET_DK ** -0.5, 0.0)
                tab_ref[h, d, 2] = jnp.exp(lgm * kpow_t) * k_scale
                tab_ref[h, d, 3] = jnp.exp(lgm * float(c))
                tab_ref[h, d, 4] = k_scale

    def head_slices(h):
        return slice((h // 2) * LANES, (h // 2 + 1) * LANES), slice(h * RET_DV, (h + 1) * RET_DV)

    chains = [(h, d) for h in range(RET_HEADS) for d in range(2)]

    def contributions(ci, carry):
        r0 = pl.multiple_of(ci * c, c)
        kt_all = k_ref[pl.ds(r0, c), :].astype(F32).T
        kt_ref[ci] = kt_all.astype(BF16)
        for h in range(RET_HEADS):
            grp, hs = head_slices(h)
            v = v_ref[pl.ds(r0, c), hs]
            for d in range(2):
                kd_t = (kt_all[grp, :] * tab_ref[h, d, 2]).astype(BF16)
                u_ref[2 * h + d, ci] = jnp.dot(kd_t, v, preferred_element_type=F32)
        return carry

    lax.fori_loop(0, nc, contributions, 0)

    for (h, d), s_ref in zip(chains, state_refs):
        s_ref[...] = s0_ref[h, d]

    def scan(ci, carry):
        for idx, (h, d) in enumerate(chains):
            cc = ci if d == 0 else nc - 1 - ci
            state = state_refs[idx][...]
            sin_ref[idx, cc] = state.astype(BF16)
            state_refs[idx][...] = tab_ref[h, d, 3] * state + u_ref[idx, cc]
        return carry

    lax.fori_loop(0, nc, scan, 0)
    for (h, d), s_ref in zip(chains, state_refs):
        st_ref[h, d] = s_ref[...]

    per_trip = 2

    def outputs(ti, carry):
        pairs = [(ti * per_trip + j, h) for j in range(per_trip) for h in range(RET_HEADS)]
        rows = [pl.ds(pl.multiple_of(ci * c, c), c) for ci, _ in pairs]
        qs = [q_ref[r, head_slices(h)[0]] for r, (_, h) in zip(rows, pairs)]
        vs = [v_ref[r, head_slices(h)[1]] for r, (_, h) in zip(rows, pairs)]
        raw = [jnp.dot(q, (kt_ref[ci, head_slices(h)[0], :].astype(F32)
                           * tab_ref[h, 0, 4]).astype(BF16), preferred_element_type=F32)
               for q, (ci, h) in zip(qs, pairs)]
        cross = [[jnp.dot(q, sin_ref[2 * h + d, ci], preferred_element_type=F32)
                  for d in range(2)] for q, (ci, h) in zip(qs, pairs)]
        intra = [jnp.dot((s * tab_ref[h, 0, 0]).astype(BF16), v, preferred_element_type=F32)
                 for s, v, (_, h) in zip(raw, vs, pairs)]
        for r, (_, h), o_in, o_cr in zip(rows, pairs, intra, cross):
            hs = head_slices(h)[1]
            o = o_in + o_cr[0] * tab_ref[h, 0, 1] + o_cr[1] * tab_ref[h, 1, 1]
            mu = jnp.mean(o, axis=-1, keepdims=True)
            var = jnp.mean(jnp.square(o - mu), axis=-1, keepdims=True)
            on = (o - mu) * lax.rsqrt(var + EPS) * gnw_ref[:, hs] + gnb_ref[:, hs]
            gate = gate_ref[r, hs].astype(F32)
            o_ref[r, hs] = (_silu(gate) * on).astype(BF16)
        return carry

    lax.fori_loop(0, nc // per_trip, outputs, 0)


def _retention(z, l, n, dec, s0, gn_w, gn_b):
    r = z.shape[0]
    w = RET_WIDTH
    qk = RET_HEADS * RET_DK
    nc = n // RET_CHUNK
    col = lambda base, width: pl.BlockSpec((n, width), lambda b: (b, base // width))
    st_spec = pl.BlockSpec((None, RET_HEADS, 2, LANES, RET_DV), lambda b: (b, 0, 0, 0, 0))
    gn_spec = pl.BlockSpec((None, 1, w), lambda b: (l, 0, 0))
    return pl.pallas_call(
        functools.partial(_ret_kernel, n=n),
        grid=(BATCH,),
        in_specs=[col(Z_RQ, qk), col(Z_RK, qk), col(Z_RV, w), col(Z_RGATE, w),
                  pl.BlockSpec((None, 2, RET_HEADS, 8, LANES), lambda b: (l, 0, 0, 0, 0)),
                  st_spec, gn_spec, gn_spec],
        out_specs=[pl.BlockSpec((n, w), lambda b: (b, 0)), st_spec],
        out_shape=[jax.ShapeDtypeStruct((r, w), BF16),
                   jax.ShapeDtypeStruct((BATCH, RET_HEADS, 2, LANES, RET_DV), F32)],
        scratch_shapes=[pltpu.VMEM((RET_HEADS, 2, 5, RET_CHUNK, RET_CHUNK), F32),
                        pltpu.VMEM((nc, qk, RET_CHUNK), BF16),
                        pltpu.VMEM((2 * RET_HEADS, nc, LANES, RET_DV), F32),
                        pltpu.VMEM((2 * RET_HEADS, nc, LANES, RET_DV), BF16)]
        + [pltpu.VMEM((LANES, RET_DV), F32)] * (2 * RET_HEADS),
        compiler_params=_cparams("arbitrary"),
        name="retention",
    )(z, z, z, z, dec, s0, gn_w, gn_b)


def _hy_filter_kernel(feat_ref, w1_ref, b1_ref, w2_ref, b2_ref, w3_ref, dl_ref, s_ref, d_ref):
    hi = lax.Precision.HIGHEST
    feats = feat_ref[...]
    h = jnp.sin(HY_SIN_FREQ * (jnp.dot(feats, w1_ref[...], preferred_element_type=F32, precision=hi)
                               + b1_ref[...]))
    h = jnp.sin(HY_SIN_FREQ * (jnp.dot(h, w2_ref[...], preferred_element_type=F32, precision=hi)
                               + b2_ref[...]))
    h = jnp.dot(h, w3_ref[...], preferred_element_type=F32, precision=hi)
    window = jnp.exp(-feats[:, 0:1] * dl_ref[...]) + HY_WINDOW_SHIFT
    window = jnp.concatenate([window] * HY_ORDER, axis=-1)
    half = HY_ORDER * HY_WIDTH
    h_fwd = h[:, :half] * window
    h_bwd = h[:, half:] * window
    s_ref[...] = (h_fwd + h_bwd).astype(BF16)
    d_ref[...] = (h_fwd - h_bwd).astype(BF16)


def _hy_filter(feats, w1, b1, w2, b2, w3, deltas, l):
    n = feats.shape[0]
    tn = 256
    half = HY_ORDER * HY_WIDTH
    full = lambda a, b: pl.BlockSpec((None, a, b), lambda i: (l, 0, 0))
    return pl.pallas_call(
        _hy_filter_kernel,
        grid=(n // tn,),
        in_specs=[pl.BlockSpec((tn, LANES), lambda i: (i, 0)),
                  full(LANES, HY_FFN), full(1, HY_FFN), full(HY_FFN, HY_FFN), full(1, HY_FFN),
                  full(HY_FFN, 2 * half),
                  pl.BlockSpec((1, HY_WIDTH), lambda i: (0, 0))],
        out_specs=[pl.BlockSpec((tn, half), lambda i: (i, 0))] * 2,
        out_shape=[jax.ShapeDtypeStruct((n, half), BF16)] * 2,
        compiler_params=_cparams("arbitrary"),
        name="hyena_filter",
    )(feats, w1, b1, w2, b2, w3, deltas)


def _hy_spec_kernel(fc_ref, fs_ref, alt_ref, s_ref, d_ref, hc_ref, hs_ref, hn_ref):
    s = s_ref[...]
    tk, n = fc_ref.shape
    row = lax.broadcasted_iota(jnp.int32, (tk, s.shape[1]), 0) + pl.program_id(0) * tk
    wk = jnp.where(row == 0, 1.0, 2.0) * (1.0 / (2 * n))
    hc_ref[...] = jnp.dot(fc_ref[...], s, preferred_element_type=F32) * wk
    hs_ref[...] = jnp.dot(fs_ref[...], d_ref[...], preferred_element_type=F32) * wk
    hn_ref[...] = jnp.dot(alt_ref[...], s, preferred_element_type=F32)[0:8, :] * (1.0 / (2 * n))


def _hy_spectrum(fc, fs, alt, s, d):
    n = fc.shape[0]
    tk = min(512, n)
    half = s.shape[1]
    out = pl.BlockSpec((tk, half), lambda i: (i, 0))
    return pl.pallas_call(
        _hy_spec_kernel,
        grid=(n // tk,),
        in_specs=[pl.BlockSpec((tk, n), lambda i: (i, 0)),
                  pl.BlockSpec((tk, n), lambda i: (i, 0)),
                  pl.BlockSpec((16, n), lambda i: (0, 0)),
                  pl.BlockSpec((n, half), lambda i: (0, 0)),
                  pl.BlockSpec((n, half), lambda i: (0, 0))],
        out_specs=[out, out, pl.BlockSpec((8, half), lambda i: (0, 0))],
        out_shape=[jax.ShapeDtypeStruct((n, half), F32)] * 2
        + [jax.ShapeDtypeStruct((8, half), F32)],
        compiler_params=_cparams("arbitrary"),
        name="hyena_spectrum",
    )(fc, fs, alt, s, d)


def _short_conv_kernel(u_ref, w_ref, b_ref, o_ref, pad_ref, *, n):
    zeros = jnp.zeros((8, pad_ref.shape[1]), F32)
    pad_ref[0:8, :] = zeros
    pad_ref[8 + n:16 + n, :] = zeros
    pad_ref[8:8 + n, :] = u_ref[...].astype(F32)
    w = w_ref[...]
    y = (pad_ref[7:7 + n, :] * w[0:1, :] + pad_ref[8:8 + n, :] * w[1:2, :]
         + pad_ref[9:9 + n, :] * w[2:3, :] + b_ref[...])
    o_ref[...] = y.astype(BF16)


def _short_conv(z, l, n, w, b):
    r = z.shape[0]
    wc = HY_WIDTH
    return pl.pallas_call(
        functools.partial(_short_conv_kernel, n=n),
        grid=(BATCH, (HY_ORDER + 1)),
        in_specs=[pl.BlockSpec((n, wc), lambda bi, j: (bi, Z_HY // wc + j)),
                  pl.BlockSpec((None, 3, wc), lambda bi, j: (l, 0, j)),
                  pl.BlockSpec((None, 1, wc), lambda bi, j: (l, 0, j))],
        out_specs=pl.BlockSpec((n, wc), lambda bi, j: (bi, j)),
        out_shape=jax.ShapeDtypeStruct((r, (HY_ORDER + 1) * wc), BF16),
        scratch_shapes=[pltpu.VMEM((n + 16, wc), F32)],
        compiler_params=_cparams("arbitrary", "arbitrary"),
        name="short_conv",
    )(z, w, b)


HY_PAIR = 2


def _long_conv_kernel(u_ref, gate_ref, fc_ref, fs_ref, g_ref, alt_ref, hc_ref, hs_ref, hn_ref,
                      skip_ref, gain_ref, o_ref, y_ref, nyq_ref, *, n, tk, final):
    j = pl.program_id(1)
    kt = n // tk

    @pl.when(j == 0)
    def _():
        for p in range(HY_PAIR):
            x_nyq = jnp.dot(alt_ref[...], u_ref[p], preferred_element_type=F32)
            nyq_ref[p] = x_nyq[0:8, :] * hn_ref[...]

    @pl.when(j < kt)
    def _():
        r0 = pl.multiple_of(j * tk, tk)
        hc, hs = hc_ref[...], hs_ref[...]
        for p in range(HY_PAIR):
            u = u_ref[p]
            xc = jnp.dot(fc_ref[...], u, preferred_element_type=F32)
            xs = jnp.dot(fs_ref[...], u, preferred_element_type=F32)
            y_ref[p, pl.ds(r0, tk), :] = (xc * hc - xs * hs).astype(BF16)
            y_ref[p, pl.ds(n + r0, tk), :] = (xc * hs + xs * hc).astype(BF16)

    @pl.when(j >= kt)
    def _():
        t0 = pl.multiple_of((j - kt) * tk, tk)
        t = lax.broadcasted_iota(jnp.int32, (tk, u_ref.shape[-1]), 0)
        alt = (1 - 2 * (t & 1)).astype(F32)
        for p in range(HY_PAIR):
            conv = jnp.dot(g_ref[...], y_ref[p], preferred_element_type=F32)
            u = u_ref[p, pl.ds(t0, tk), :].astype(F32)
            y = conv + alt * nyq_ref[p][0:1, :] + u * skip_ref[...]
            y = gate_ref[p].astype(F32) * y
            if final:
                y = _rms(y) * gain_ref[...]
            o_ref[p] = y.astype(BF16)


def _long_conv(src, u_col, gate_col, mats, spec, order, skip, gain, l, n, final):
    fc, fs, g, alt = mats
    hc, hs, hn = spec
    r = src[0].shape[0]
    wc = HY_WIDTH
    tk = min(512, n)
    kt = n // tk
    nb = BATCH // HY_PAIR
    u4, gate4 = (a.reshape(nb, HY_PAIR, n, a.shape[-1]) for a in src)
    f_spec = pl.BlockSpec((tk, n), lambda bi, j: (jnp.minimum(j, kt - 1), 0))
    h_spec = pl.BlockSpec((tk, wc), lambda bi, j: (jnp.minimum(j, kt - 1), order))
    tile = lambda col: pl.BlockSpec((None, HY_PAIR, tk, wc),
                                    lambda bi, j: (bi, 0, jnp.maximum(j - kt, 0), col))
    out = pl.pallas_call(
        functools.partial(_long_conv_kernel, n=n, tk=tk, final=final),
        grid=(nb, 2 * kt),
        in_specs=[pl.BlockSpec((None, HY_PAIR, n, wc), lambda bi, j: (bi, 0, 0, u_col)),
                  tile(gate_col),
                  f_spec, f_spec,
                  pl.BlockSpec((tk, 2 * n), lambda bi, j: (jnp.maximum(j - kt, 0), 0)),
                  pl.BlockSpec((16, n), lambda bi, j: (0, 0)),
                  h_spec, h_spec,
                  pl.BlockSpec((8, wc), lambda bi, j: (0, order)),
                  pl.BlockSpec((None, None, 1, wc), lambda bi, j: (l, order, 0, 0)),
                  pl.BlockSpec((None, 1, wc), lambda bi, j: (l, 0, 0))],
        out_specs=tile(0),
        out_shape=jax.ShapeDtypeStruct((nb, HY_PAIR, n, wc), BF16),
        scratch_shapes=[pltpu.VMEM((HY_PAIR, 2 * n, wc), BF16),
                        pltpu.VMEM((HY_PAIR, 8, wc), F32)],
        compiler_params=_cparams("arbitrary", "arbitrary"),
        name="long_conv",
    )(u4, gate4, fc, fs, g, alt, hc, hs, hn, skip, gain)
    return out.reshape(r, wc)


def _dft_mats(n):
    big = 2 * n
    fine = 32
    t = jnp.arange(n, dtype=jnp.int32)

    def trig(k):
        ang = ((k[:, None] * t[None, :]) % big).astype(F32) * (2.0 * math.pi / big)
        return jnp.cos(ang), jnp.sin(ang)

    ca, sa = (m[:, None, :] for m in trig(fine * jnp.arange(n // fine, dtype=jnp.int32)))
    cb, sb = (m[None, :, :] for m in trig(jnp.arange(fine, dtype=jnp.int32)))
    fc = (ca * cb - sa * sb).reshape(n, n).astype(BF16)
    fs = (sa * cb + ca * sb).reshape(n, n).astype(BF16)
    alt = jnp.broadcast_to((1 - 2 * (t % 2)).astype(F32)[None, :], (16, n))
    return fc, fs, jnp.concatenate([fc, fs], axis=1), alt.astype(BF16)


def _hy_features(n):
    t = jnp.linspace(0.0, 1.0, n, dtype=F32)[:, None]
    bands = (HY_EMB - 1) // 2
    w = 2 * math.pi * jnp.arange(n, dtype=F32)[:, None] / n
    f = jnp.linspace(1e-4, bands - 1, bands, dtype=F32)[None, :]
    feats = jnp.concatenate([t, jnp.cos(f * w), -jnp.sin(f * w)], axis=-1)
    return jnp.pad(feats, ((0, 0), (0, LANES - HY_EMB)))


def _rope_table():
    n_freq = MLA_ROPE // 4
    inv_freq = ROPE_BASE ** (-jnp.arange(n_freq, dtype=F32) / n_freq)
    rows = SEQ // GRID_W
    row = jnp.repeat(jnp.arange(rows), GRID_W).astype(F32)
    col = jnp.tile(jnp.arange(GRID_W), rows).astype(F32)
    ar = row[:, None] * inv_freq[None, :]
    ac = col[:, None] * inv_freq[None, :]
    cos_part = jnp.concatenate([jnp.cos(ar), jnp.cos(ar), jnp.cos(ac), jnp.cos(ac)], axis=-1)
    sin_part = jnp.concatenate([-jnp.sin(ar), jnp.sin(ar), -jnp.sin(ac), jnp.sin(ac)], axis=-1)
    return jnp.concatenate([cos_part, sin_part], axis=-1)


_ROPE_PARTNER = np.concatenate([np.arange(16, 32), np.arange(0, 16),
                                np.arange(48, 64), np.arange(32, 48)])
_QUARTER = MLA_ROPE // 4


def _partner_pieces(base):
    return [(base + int(_ROPE_PARTNER[i]), _QUARTER) for i in range(0, MLA_ROPE, _QUARTER)]


def _w_in_pieces():
    off = np.cumsum([0, MLA_KV_LORA, MLA_ROPE, RET_HEADS * RET_DK, RET_WIDTH, MLA_Q_LORA,
                     RET_HEADS * RET_DK, RET_WIDTH])
    o_kv, o_kr, o_rk, o_rv, o_q, o_rq, o_gate, o_hy = [int(v) for v in off]
    pieces = [(o_hy, 3 * HY_WIDTH), (o_rv, RET_WIDTH), (o_q, MLA_Q_LORA), (o_gate, RET_WIDTH),
              (o_rq, RET_HEADS * RET_DK), (o_rk, RET_HEADS * RET_DK),
              (o_kv, MLA_KV_LORA), (o_kr, MLA_ROPE)] + _partner_pieces(o_kr)
    pieces.append((None, Z_COLS - sum(w for _, w in pieces)))
    return pieces


def _wuq_pieces():
    per = MLA_NOPE + MLA_ROPE
    pieces = []
    for h in range(MLA_HEADS):
        pieces += [(h * per, per)] + _partner_pieces(h * per + MLA_NOPE)
    return pieces


def _relayout_kernel(w_ref, o_ref, *, pieces):
    col = 0
    for start, width in pieces:
        if start is None:
            o_ref[:, col:col + width] = jnp.zeros((o_ref.shape[0], width), o_ref.dtype)
        else:
            o_ref[:, col:col + width] = w_ref[:, start:start + width].astype(o_ref.dtype)
        col += width


def _relayout_cols(w, pieces):
    nl, k, n_in = w.shape
    n_out = sum(width for _, width in pieces)
    tr = 256
    return pl.pallas_call(
        functools.partial(_relayout_kernel, pieces=tuple(pieces)),
        grid=(nl, k // tr),
        in_specs=[pl.BlockSpec((None, tr, n_in), lambda l, i: (l, i, 0))],
        out_specs=pl.BlockSpec((None, tr, n_out), lambda l, i: (l, i, 0)),
        out_shape=jax.ShapeDtypeStruct((nl, k, n_out), BF16),
        compiler_params=_cparams("arbitrary", "arbitrary"),
        name="relayout_cols",
    )(w)


def _relayout_rows_kernel(w_ref, o_ref, *, pieces):
    row = 0
    for start, width in pieces:
        if start is None:
            o_ref[row:row + width, :] = jnp.zeros((width, o_ref.shape[1]), o_ref.dtype)
        else:
            o_ref[row:row + width, :] = w_ref[start:start + width, :].astype(o_ref.dtype)
        row += width


def _relayout_rows(w_t, pieces):
    nl, n_in, k = w_t.shape
    n_out = sum(width for _, width in pieces)
    tc = 512
    return pl.pallas_call(
        functools.partial(_relayout_rows_kernel, pieces=tuple(pieces)),
        grid=(nl, k // tc),
        in_specs=[pl.BlockSpec((None, n_in, tc), lambda l, i: (l, 0, i))],
        out_specs=pl.BlockSpec((None, n_out, tc), lambda l, i: (l, 0, i)),
        out_shape=jax.ShapeDtypeStruct((nl, n_out, k), BF16),
        compiler_params=_cparams("arbitrary", "arbitrary"),
        name="relayout_rows",
    )(w_t)


def kernel(x, c, ctx, c_ctx, ada_w, ada_b, ffn1_gate, ffn1_up, ffn1_down, w_in, mla_q_norm, mla_wuq, mla_kv_norm, mla_wukv, mla_qn_nope, mla_qn_rope, mla_kn_nope, mla_kn_rope, mla_out_norm, hy_conv_w, hy_conv_b, hy_ffn_w1, hy_ffn_b1, hy_ffn_w2, hy_ffn_b2, hy_ffn_w3, hy_skip, hy_out_norm, ret_decay, ret_gn_w, ret_gn_b, w_out, ffn2_gate, ffn2_up, ffn2_down):
    nl = DEPTH
    d = D_MODEL

    w1g, w1u, w1d = ffn1_gate, ffn1_up, ffn1_down.astype(BF16)
    w2g, w2u, w2d = ffn2_gate, ffn2_up, ffn2_down.astype(BF16)
    w_in_p = _relayout_rows(jnp.swapaxes(w_in, 1, 2), _w_in_pieces())
    wuq_p = _relayout_cols(mla_wuq, _wuq_pieces())
    wukv = mla_wukv.astype(BF16)
    w_out_b = w_out.astype(BF16)
    row3 = lambda g: g.reshape(nl, 1, -1)
    partner = jnp.asarray(_ROPE_PARTNER)
    g_qr = row3(jnp.concatenate([mla_qn_rope, mla_qn_rope[:, partner]], axis=-1))
    g_kr = row3(jnp.concatenate([mla_kn_rope, mla_kn_rope[:, partner]], axis=-1))
    g_q, g_kv, g_qn, g_kn = row3(mla_q_norm), row3(mla_kv_norm), row3(mla_qn_nope), row3(mla_kn_nope)
    g_attn, g_hy = row3(mla_out_norm), row3(hy_out_norm)
    gn_w, gn_b = row3(ret_gn_w), row3(ret_gn_b)
    hy_b = row3(hy_conv_b)
    hy_w1 = jnp.pad(hy_ffn_w1, ((0, 0), (0, LANES - HY_EMB), (0, 0)))
    hy_b1, hy_b2 = row3(hy_ffn_b1), row3(hy_ffn_b2)
    hy_skip4 = hy_skip.reshape(nl, HY_ORDER, 1, HY_WIDTH)
    dec = jnp.broadcast_to(ret_decay.astype(F32)[:, :, :, None, None], (nl, 2, RET_HEADS, 8, LANES))

    tab_x = _rope_table()
    tab_c = jnp.concatenate([jnp.ones((CTX_LEN, MLA_ROPE), F32), jnp.zeros((CTX_LEN, MLA_ROPE), F32)], -1)
    deltas = jnp.abs(jnp.linspace(HY_MIN_DECAY, HY_MAX_DECAY, HY_WIDTH, dtype=F32))[None, :]
    mats = {n: _dft_mats(n) for n in (SEQ, CTX_LEN)}
    feats = {n: _hy_features(n) for n in (SEQ, CTX_LEN)}

    cond = jnp.zeros((16, d), F32).at[0].set(c_ctx).at[1:1 + BATCH].set(c)
    mods = _modulation(cond, ada_w, ada_b).reshape(nl, 16, N_MOD, d)

    sx = x.reshape(BATCH * SEQ, d)
    sc = ctx.reshape(BATCH * CTX_LEN, d)

    def mod_tables(l):
        return ([mods[l, 1:1 + BATCH, j][:, None, :] for j in range(N_MOD)],
                [mods[l, 0:1, j][:, None, :] for j in range(N_MOD)])

    all_mods = [mod_tables(l) for l in range(nl)]

    def ffn(s, h, n, gate, wg, wu, wd, l):
        return _ffn_down(_ffn_up(h, wg, wu, l), wd, l, s, gate, n)

    def hyena(z, l, n, spec):
        cv = _short_conv(z, l, n, hy_conv_w, hy_b)
        y1 = _long_conv((cv, cv), 2, 0, mats[n], spec, 0, hy_skip4, g_hy, l, n, False)
        return _long_conv((y1, cv), 0, 1, mats[n], spec, 1, hy_skip4, g_hy, l, n, True)

    for l in range(nl):
        need_ctx_out = l < nl - 1
        mod_x, mod_c = all_mods[l]

        sx = ffn(sx, _normmod(sx, mod_x[0], mod_x[1], SEQ), SEQ, mod_x[2], w1g, w1u, w1d, l)
        sc = ffn(sc, _normmod(sc, mod_c[0], mod_c[1], CTX_LEN), CTX_LEN, mod_c[2], w1g, w1u, w1d, l)

        zc = _norm_matmul(sc, mod_c[3], mod_c[4], CTX_LEN, w_in_p, l)
        zx = _norm_matmul(sx, mod_x[3], mod_x[4], SEQ, w_in_p, l)

        kc, vc = _kv_proj(zc, l, CTX_LEN, wukv, g_kv, g_kn, g_kr, tab_c)
        kx, vx = _kv_proj(zx, l, SEQ, wukv, g_kv, g_kn, g_kr, tab_x)
        zero_state = jnp.zeros((BATCH, RET_HEADS, 2, LANES, RET_DV), F32)
        ret_c, state_c = _retention(zc, l, CTX_LEN, dec, zero_state, gn_w, gn_b)

        filt = {}
        for n in ((SEQ, CTX_LEN) if need_ctx_out else (SEQ,)):
            s_f, d_f = _hy_filter(feats[n], hy_w1, hy_b1, hy_ffn_w2, hy_b2, hy_ffn_w3, deltas, l)
            fc, fs, _, alt = mats[n]
            filt[n] = _hy_spectrum(fc, fs, alt, s_f, d_f)

        qx = _q_proj(zx, l, SEQ, wuq_p, g_q, g_qn, g_qr, tab_x)
        attn_x = _attention(qx, [kc, kx], [vc, vx], [CTX_LEN, SEQ], l, g_attn)
        hy_x = hyena(zx, l, SEQ, filt[SEQ])
        ret_x, _ = _retention(zx, l, SEQ, dec, state_c, gn_w, gn_b)
        sx, hx = _w_out([attn_x, hy_x, ret_x], w_out_b, l, sx, mod_x[5], SEQ, (mod_x[6], mod_x[7]))
        sx = ffn(sx, hx, SEQ, mod_x[8], w2g, w2u, w2d, l)

        if need_ctx_out:
            qc = _q_proj(zc, l, CTX_LEN, wuq_p, g_q, g_qn, g_qr, tab_c)
            attn_c = _attention(qc, [kc], [vc], [CTX_LEN], l, g_attn)
            hy_c = hyena(zc, l, CTX_LEN, filt[CTX_LEN])
            sc, hc = _w_out([attn_c, hy_c, ret_c], w_out_b, l, sc, mod_c[5], CTX_LEN,
                            (mod_c[6], mod_c[7]))
            sc = ffn(sc, hc, CTX_LEN, mod_c[8], w2g, w2u, w2d, l)

    return sx.reshape(BATCH, SEQ, d)
```

```python
import functools
import math

import jax
import jax.numpy as jnp
import numpy as np
from jax import lax
from jax.experimental import pallas as pl
from jax.experimental.pallas import tpu as pltpu

F32 = jnp.float32
BF16 = jnp.bfloat16

D_MODEL = 2048
BATCH = 8
SEQ = 2048
DEPTH = 2
GRID_W = 64
CTX_LEN = 256
EPS = 1e-6
N_MOD = 9
FFN_HIDDEN = 5632

MLA_HEADS = 8
MLA_NOPE = 128
MLA_ROPE = 64
MLA_V = 128
MLA_Q_LORA = 512
MLA_KV_LORA = 256
MLA_WIDTH = MLA_HEADS * MLA_V
ROPE_BASE = 10000.0

HY_WIDTH = 512
HY_ORDER = 2
HY_EMB = 33
HY_FFN = 64
HY_SIN_FREQ = 1.0
HY_MIN_DECAY = math.log(1e-2) / 1.5
HY_MAX_DECAY = math.log(1e-2) / 0.3
HY_WINDOW_SHIFT = 0.05

RET_HEADS = 4
RET_DK = 64
RET_DV = 128
RET_WIDTH = RET_HEADS * RET_DV
RET_CHUNK = 128

VMEM_LIMIT_BYTES = 60 * 1024 * 1024
LANES = 128
HEAD_SLOT = 256

Z_HY = 0
Z_RV = 1536
Z_QLAT = 2048
Z_RGATE = 2560
Z_RQ = 3072
Z_RK = 3328
Z_KVLAT = 3584
Z_KR = 3840
Z_COLS = 4096

TM = 1024
TM_WIDE = 2048
TN = 512


def _cparams(*sem):
    return pltpu.CompilerParams(dimension_semantics=sem, vmem_limit_bytes=VMEM_LIMIT_BYTES)


def _rms(x, width=None):
    width = x.shape[-1] if width is None else width
    return x * lax.rsqrt(jnp.sum(x * x, axis=-1, keepdims=True) / width + EPS)


def _silu(x):
    return x * jax.nn.sigmoid(x)


def _mod_kernel(c_ref, w_ref, b_ref, o_ref):
    a = _silu(c_ref[...])
    w = w_ref[...]
    a_hi, w_hi = a.astype(BF16), w.astype(BF16)
    a_lo = (a - a_hi.astype(F32)).astype(BF16)
    w_lo = (w - w_hi.astype(F32)).astype(BF16)
    rows = a.shape[0]
    acc = jnp.dot(jnp.concatenate([a_hi, a_lo], axis=0), w_hi, preferred_element_type=F32)
    acc = acc[:rows] + acc[rows:] + jnp.dot(a_hi, w_lo, preferred_element_type=F32)
    o_ref[...] = acc + b_ref[...]


def _modulation(cond, ada_w, ada_b):
    nl, d, nout = ada_w.shape
    bn = 1024
    return pl.pallas_call(
        _mod_kernel,
        grid=(nl, nout // bn),
        in_specs=[pl.BlockSpec((16, d), lambda l, j: (0, 0)),
                  pl.BlockSpec((None, d, bn), lambda l, j: (l, 0, j)),
                  pl.BlockSpec((None, 1, bn), lambda l, j: (l, 0, j))],
        out_specs=pl.BlockSpec((None, 16, bn), lambda l, j: (l, 0, j)),
        out_shape=jax.ShapeDtypeStruct((nl, 16, nout), F32),
        compiler_params=_cparams("arbitrary", "arbitrary"),
        name="modulation",
    )(cond, ada_w, ada_b.reshape(nl, 1, nout))


def _normmod_kernel(x_ref, sh_ref, sc_ref, o_ref):
    y = _rms(x_ref[...])
    o_ref[...] = (y * (1.0 + sc_ref[...]) + sh_ref[...]).astype(BF16)


def _normmod(s, shift, scale, n):
    r, d = s.shape
    tm = min(512, n)
    per_batch = shift.shape[0] > 1
    mod_spec = pl.BlockSpec((None, 1, d), lambda i: ((i * tm) // n if per_batch else 0, 0, 0))
    return pl.pallas_call(
        _normmod_kernel,
        grid=(r // tm,),
        in_specs=[pl.BlockSpec((tm, d), lambda i: (i, 0)), mod_spec, mod_spec],
        out_specs=pl.BlockSpec((tm, d), lambda i: (i, 0)),
        out_shape=jax.ShapeDtypeStruct((r, d), BF16),
        compiler_params=_cparams("arbitrary"),
        name="normmod",
    )(s, shift, scale)


def _ffn_up_kernel(h_ref, wg_ref, wu_ref, o_ref, wg_bf, wu_bf):
    @pl.when(pl.program_id(1) == 0)
    def _():
        wg_bf[...] = wg_ref[...].astype(BF16)
        wu_bf[...] = wu_ref[...].astype(BF16)

    h = h_ref[...]
    g = jnp.dot(h, wg_bf[...], preferred_element_type=F32)
    u = jnp.dot(h, wu_bf[...], preferred_element_type=F32)
    o_ref[...] = (_silu(g) * u).astype(BF16)


def _ffn_up(h, wg, wu, l):
    r, d = h.shape
    f = wg.shape[-1]
    w_spec = pl.BlockSpec((None, d, TN), lambda j, i: (l, 0, j))
    tm = TM
    return pl.pallas_call(
        _ffn_up_kernel,
        grid=(f // TN, r // tm),
        in_specs=[pl.BlockSpec((tm, d), lambda j, i: (i, 0)), w_spec, w_spec],
        out_specs=pl.BlockSpec((tm, TN), lambda j, i: (i, j)),
        out_shape=jax.ShapeDtypeStruct((r, f), BF16),
        scratch_shapes=[pltpu.VMEM((d, TN), BF16), pltpu.VMEM((d, TN), BF16)],
        compiler_params=_cparams("arbitrary", "arbitrary"),
        name="ffn_up",
    )(h, wg, wu)


def _mod_spec(table, tm, n, d):
    per_batch = table.shape[0] > 1
    return pl.BlockSpec((None, 1, d), lambda i, *_: ((i * tm) // n if per_batch else 0, 0, 0))


def _norm_mm_kernel(s_ref, sh_ref, sc_ref, w_ref, o_ref, h_ref):
    @pl.when(pl.program_id(1) == 0)
    def _():
        h_ref[...] = (_rms(s_ref[...]) * (1.0 + sc_ref[...]) + sh_ref[...]).astype(BF16)

    o_ref[...] = lax.dot_general(h_ref[...], w_ref[...], (((1,), (1,)), ((), ())),
                                 preferred_element_type=F32).astype(o_ref.dtype)


def _norm_matmul(s, shift, scale, n, w_t, l):
    r, d = s.shape
    nout = w_t.shape[1]
    tm = TM_WIDE
    return pl.pallas_call(
        _norm_mm_kernel,
        grid=(r // tm, nout // TN),
        in_specs=[pl.BlockSpec((tm, d), lambda i, j: (i, 0)),
                  _mod_spec(shift, tm, n, d), _mod_spec(scale, tm, n, d),
                  pl.BlockSpec((None, TN, d), lambda i, j: (l, j, 0))],
        out_specs=pl.BlockSpec((tm, TN), lambda i, j: (i, j)),
        out_shape=jax.ShapeDtypeStruct((r, nout), BF16),
        scratch_shapes=[pltpu.VMEM((tm, d), BF16)],
        compiler_params=_cparams("arbitrary", "arbitrary"),
        name="norm_matmul",
    )(s, shift, scale, w_t)


def _ffn_down_kernel(u_ref, w_ref, s_ref, g_ref, o_ref):
    acc = jnp.dot(u_ref[...], w_ref[...], preferred_element_type=F32)
    o_ref[...] = s_ref[...] + (0.5 * g_ref[...]) * acc


def _ffn_down(u, w, l, s, gate, n):
    r, d = s.shape
    f = u.shape[1]
    per_batch = gate.shape[0] > 1
    return pl.pallas_call(
        _ffn_down_kernel,
        grid=(r // TM, d // TN),
        in_specs=[pl.BlockSpec((TM, f), lambda i, j: (i, 0)),
                  pl.BlockSpec((None, f, TN), lambda i, j: (l, 0, j)),
                  pl.BlockSpec((TM, TN), lambda i, j: (i, j)),
                  pl.BlockSpec((None, 1, TN), lambda i, j: ((i * TM) // n if per_batch else 0, 0, j))],
        out_specs=pl.BlockSpec((TM, TN), lambda i, j: (i, j)),
        out_shape=jax.ShapeDtypeStruct((r, d), F32),
        compiler_params=_cparams("arbitrary", "arbitrary"),
        name="ffn_down",
    )(u, w, s, gate)


def _w_out_kernel(*refs, n_in):
    a_refs = refs[:n_in]
    w_ref, s_ref, g_ref, sh_ref, sc_ref, o_ref, h_ref = refs[n_in:]
    half = s_ref.shape[0] // 2
    accs = []
    for r0 in (0, half):
        acc, row = None, 0
        for a_ref in a_refs:
            kp = a_ref.shape[1]
            part = jnp.dot(a_ref[r0:r0 + half, :], w_ref[row:row + kp, :],
                           preferred_element_type=F32)
            acc = part if acc is None else acc + part
            row += kp
        accs.append(acc)
    for r0, acc in zip((0, half), accs):
        rows = slice(r0, r0 + half)
        s_new = s_ref[rows, :] + g_ref[...] * acc
        o_ref[rows, :] = s_new
        h_ref[rows, :] = (_rms(s_new) * (1.0 + sc_ref[...]) + sh_ref[...]).astype(BF16)


def _w_out(parts, w, l, s, gate, n, next_mod):
    r, d = s.shape
    tm = TM // 2
    shift, scale = next_mod
    row = pl.BlockSpec((tm, d), lambda i: (i, 0))
    return pl.pallas_call(
        functools.partial(_w_out_kernel, n_in=len(parts)),
        grid=(r // tm,),
        in_specs=[pl.BlockSpec((tm, p.shape[1]), lambda i: (i, 0)) for p in parts]
        + [pl.BlockSpec((None, d, d), lambda i: (l, 0, 0)),
           row, _mod_spec(gate, tm, n, d), _mod_spec(shift, tm, n, d), _mod_spec(scale, tm, n, d)],
        out_specs=[row, row],
        out_shape=[jax.ShapeDtypeStruct((r, d), F32), jax.ShapeDtypeStruct((r, d), BF16)],
        compiler_params=_cparams("arbitrary"),
        name="w_out",
    )(*parts, w, s, gate, shift, scale)


def _rope_lanes(x):
    lane = lax.broadcasted_iota(jnp.int32, x.shape, 1)
    ms = jnp.sum(jnp.where(lane < MLA_ROPE, x * x, 0.0), axis=-1, keepdims=True) / MLA_ROPE
    return x * lax.rsqrt(ms + EPS)


def _kv_kernel(lat_ref, kr_ref, gkv_ref, w_ref, gkn_ref, gkr_ref, tab_ref, k_ref, v_ref):
    a = _rms(lat_ref[...].astype(F32)) * gkv_ref[...]
    kv = jnp.dot(a.astype(BF16), w_ref[...], preferred_element_type=F32)
    t = _rope_lanes(kr_ref[...].astype(F32)) * gkr_ref[...] * tab_ref[...]
    krot = (t + pltpu.roll(t, MLA_ROPE, 1)).astype(BF16)
    for h in range(MLA_HEADS):
        c0 = h * HEAD_SLOT
        kn = _rms(kv[:, c0:c0 + MLA_NOPE]) * gkn_ref[...]
        k_ref[:, c0:c0 + MLA_NOPE] = kn.astype(BF16)
        k_ref[:, c0 + MLA_NOPE:c0 + HEAD_SLOT] = krot
        v_ref[:, h * MLA_V:(h + 1) * MLA_V] = kv[:, c0 + MLA_NOPE:c0 + HEAD_SLOT].astype(BF16)


def _kv_proj(z, l, n, wukv, g_kv, g_kn, g_kr, tab):
    r = z.shape[0]
    tm = min(512, n)
    nt = tab.shape[0] // tm
    vec = lambda w: pl.BlockSpec((None, 1, w), lambda i: (l, 0, 0))
    return pl.pallas_call(
        _kv_kernel,
        grid=(r // tm,),
        in_specs=[pl.BlockSpec((tm, MLA_KV_LORA), lambda i: (i, Z_KVLAT // MLA_KV_LORA)),
                  pl.BlockSpec((tm, LANES), lambda i: (i, Z_KR // LANES)),
                  vec(MLA_KV_LORA),
                  pl.BlockSpec((None, MLA_KV_LORA, MLA_HEADS * HEAD_SLOT), lambda i: (l, 0, 0)),
                  vec(MLA_NOPE), vec(LANES),
                  pl.BlockSpec((tm, LANES), lambda i: (i % nt, 0))],
        out_specs=[pl.BlockSpec((tm, MLA_HEADS * HEAD_SLOT), lambda i: (i, 0)),
                   pl.BlockSpec((tm, MLA_WIDTH), lambda i: (i, 0))],
        out_shape=[jax.ShapeDtypeStruct((r, MLA_HEADS * HEAD_SLOT), BF16),
                   jax.ShapeDtypeStruct((r, MLA_WIDTH), BF16)],
        compiler_params=_cparams("arbitrary"),
        name="kv_proj",
    )(z, z, g_kv, wukv, g_kn, g_kr, tab)


def _q_kernel(lat_ref, gq_ref, w_ref, gn_ref, gr_ref, tab_ref, q_ref, *, scale):
    a = _rms(lat_ref[...].astype(F32)) * gq_ref[...]
    q = jnp.dot(a.astype(BF16), w_ref[...], preferred_element_type=F32)
    rope_scale = gr_ref[...] * tab_ref[...] * scale
    for h in range(MLA_HEADS):
        c0 = h * HEAD_SLOT
        qn = _rms(q[:, c0:c0 + MLA_NOPE]) * (gn_ref[...] * scale)
        q_ref[:, c0:c0 + MLA_NOPE] = qn.astype(BF16)
        qr = _rope_lanes(q[:, c0 + MLA_NOPE:c0 + HEAD_SLOT]) * rope_scale
        q_ref[:, c0 + MLA_NOPE:c0 + HEAD_SLOT] = qr.astype(BF16)


def _q_proj(z, l, n, wuq, g_q, g_qn, g_qr, tab):
    r = z.shape[0]
    tm = min(512, n)
    nt = tab.shape[0] // tm
    vec = lambda w: pl.BlockSpec((None, 1, w), lambda i: (l, 0, 0))
    return pl.pallas_call(
        functools.partial(_q_kernel, scale=(MLA_NOPE + MLA_ROPE) ** -0.5 * math.log2(math.e)),
        grid=(r // tm,),
        in_specs=[pl.BlockSpec((tm, MLA_Q_LORA), lambda i: (i, Z_QLAT // MLA_Q_LORA)),
                  vec(MLA_Q_LORA),
                  pl.BlockSpec((None, MLA_Q_LORA, MLA_HEADS * HEAD_SLOT), lambda i: (l, 0, 0)),
                  vec(MLA_NOPE), vec(LANES),
                  pl.BlockSpec((tm, LANES), lambda i: (i % nt, 0))],
        out_specs=pl.BlockSpec((tm, MLA_HEADS * HEAD_SLOT), lambda i: (i, 0)),
        out_shape=jax.ShapeDtypeStruct((r, MLA_HEADS * HEAD_SLOT), BF16),
        compiler_params=_cparams("arbitrary"),
        name="q_proj",
    )(z, g_q, wuq, g_qn, g_qr, tab)


def _attn_kernel(*refs, n_src):
    q_ref = refs[0]
    k_refs, v_refs = refs[1:1 + n_src], refs[1 + n_src:1 + 2 * n_src]
    g_ref, o_ref, acc_ref = refs[1 + 2 * n_src:]

    def scores(h):
        hs = slice(h * HEAD_SLOT, (h + 1) * HEAD_SLOT)
        return [lax.dot_general(q_ref[:, hs], k_ref[:, hs], (((1,), (1,)), ((), ())),
                                preferred_element_type=F32) for k_ref in k_refs]

    ssq = None
    ss_next = scores(0)
    for h in range(MLA_HEADS):
        hs = slice(h * HEAD_SLOT, (h + 1) * HEAD_SLOT)
        ss, ss_next = ss_next, (scores(h + 1) if h + 1 < MLA_HEADS else None)
        m = functools.reduce(jnp.maximum, [jnp.max(s, axis=-1, keepdims=True) for s in ss])
        vs = [jnp.concatenate([v_ref[:, h * MLA_V:(h + 1) * MLA_V],
                               jnp.ones((v_ref.shape[0], HEAD_SLOT - MLA_V), BF16)], axis=1)
              for v_ref in v_refs]
        o = functools.reduce(jnp.add, [
            jnp.dot(jnp.exp2((s - m).astype(BF16)), v, preferred_element_type=F32)
            for s, v in zip(ss, vs)])
        o = o[:, :MLA_V] / o[:, MLA_V:MLA_V + 1]
        acc_ref[:, h * MLA_V:(h + 1) * MLA_V] = o
        sq = jnp.sum(o * o, axis=-1, keepdims=True)
        ssq = sq if ssq is None else ssq + sq
    o_ref[...] = (acc_ref[...] * lax.rsqrt(ssq / MLA_WIDTH + EPS) * g_ref[...]).astype(BF16)


def _attention(q, ks, vs, ns, l, g_out):
    r = q.shape[0]
    nq = r // BATCH
    tq = min(512, nq)
    per_b = nq // tq
    k_specs = [pl.BlockSpec((nk, MLA_HEADS * HEAD_SLOT), lambda b, i: (b, 0)) for nk in ns]
    v_specs = [pl.BlockSpec((nk, MLA_WIDTH), lambda b, i: (b, 0)) for nk in ns]
    return pl.pallas_call(
        functools.partial(_attn_kernel, n_src=len(ks)),
        grid=(BATCH, per_b),
        in_specs=[pl.BlockSpec((tq, MLA_HEADS * HEAD_SLOT), lambda b, i: (b * per_b + i, 0))]
        + k_specs + v_specs
        + [pl.BlockSpec((None, 1, MLA_WIDTH), lambda b, i: (l, 0, 0))],
        out_specs=pl.BlockSpec((tq, MLA_WIDTH), lambda b, i: (b * per_b + i, 0)),
        out_shape=jax.ShapeDtypeStruct((r, MLA_WIDTH), BF16),
        scratch_shapes=[pltpu.VMEM((tq, MLA_WIDTH), F32)],
        compiler_params=_cparams("arbitrary", "arbitrary"),
        name="attention",
    )(q, *ks, *vs, g_out)


def _log_sigmoid(x):
    return jnp.minimum(x, 0.0) - jnp.log(1.0 + jnp.exp(-jnp.abs(x)))


def _ret_kernel(q_ref, k_ref, v_ref, gate_ref, dec_ref, s0_ref, gnw_ref, gnb_ref,
                o_ref, st_ref, tab_ref, kt_ref, u_ref, sin_ref, *state_refs, n):
    nc = n // RET_CHUNK
    c = RET_CHUNK

    @pl.when(pl.program_id(0) == 0)
    def _():
        ii = lax.broadcasted_iota(jnp.int32, (c, c), 0).astype(F32)
        jj = lax.broadcasted_iota(jnp.int32, (c, c), 1).astype(F32)
        for h in range(RET_HEADS):
            for d in range(2):
                lg = _log_sigmoid(dec_ref[d, h][0:1, :])
                lgm = jnp.broadcast_to(lg, (c, c))
                if d == 0:
                    diff, qpow, kpow = ii - jj, ii + 1.0, (c - 1.0) - ii
                else:
                    diff, qpow, kpow = jj - ii, c - ii, ii
                intra = jnp.where(diff >= 0, jnp.exp(lgm * jnp.maximum(diff, 0.0)), 0.0)
                if d == 0:
                    tab_ref[h, 0, 0] = intra
                else:
                    tab_ref[h, 0, 0] += intra
                tab_ref[h, d, 1] = jnp.exp(lgm * qpow)
                kpow_t = (c - 1.0) - jj if d == 0 else jj
                k_scale = jnp.where(ii.astype(jnp.int32) // RET_DK == h % 2, RET_DK ** -0.5, 0.0)
                tab_ref[h, d, 2] = jnp.exp(lgm * kpow_t) * k_scale
                tab_ref[h, d, 3] = jnp.exp(lgm * float(c))
                tab_ref[h, d, 4] = k_scale

    def head_slices(h):
        return slice((h // 2) * LANES, (h // 2 + 1) * LANES), slice(h * RET_DV, (h + 1) * RET_DV)

    chains = [(h, d) for h in range(RET_HEADS) for d in range(2)]

    def contributions(ci, carry):
        r0 = pl.multiple_of(ci * c, c)
        kt_all = k_ref[pl.ds(r0, c), :].astype(F32).T
        kt_ref[ci] = kt_all.astype(BF16)
        for h in range(RET_HEADS):
            grp, hs = head_slices(h)
            v = v_ref[pl.ds(r0, c), hs]
            for d in range(2):
                kd_t = (kt_all[grp, :] * tab_ref[h, d, 2]).astype(BF16)
                u_ref[2 * h + d, ci] = jnp.dot(kd_t, v, preferred_element_type=F32)
        return carry

    lax.fori_loop(0, nc, contributions, 0)

    for (h, d), s_ref in zip(chains, state_refs):
        s_ref[...] = s0_ref[h, d]

    def scan(ci, carry):
        for idx, (h, d) in enumerate(chains):
            cc = ci if d == 0 else nc - 1 - ci
            state = state_refs[idx][...]
            sin_ref[idx, cc] = state.astype(BF16)
            state_refs[idx][...] = tab_ref[h, d, 3] * state + u_ref[idx, cc]
        return carry

    lax.fori_loop(0, nc, scan, 0)
    for (h, d), s_ref in zip(chains, state_refs):
        st_ref[h, d] = s_ref[...]

    per_trip = min(4, nc)

    def outputs(ti, carry):
        pairs = [(ti * per_trip + j, h) for j in range(per_trip) for h in range(RET_HEADS)]
        rows = [pl.ds(pl.multiple_of(ci * c, c), c) for ci, _ in pairs]
        qs = [q_ref[r, head_slices(h)[0]] for r, (_, h) in zip(rows, pairs)]
        vs = [v_ref[r, head_slices(h)[1]] for r, (_, h) in zip(rows, pairs)]
        raw = [jnp.dot(q, (kt_ref[ci, head_slices(h)[0], :].astype(F32)
                           * tab_ref[h, 0, 4]).astype(BF16), preferred_element_type=F32)
               for q, (ci, h) in zip(qs, pairs)]
        cross = [[jnp.dot(q, sin_ref[2 * h + d, ci], preferred_element_type=F32)
                  for d in range(2)] for q, (ci, h) in zip(qs, pairs)]
        intra = [jnp.dot((s * tab_ref[h, 0, 0]).astype(BF16), v, preferred_element_type=F32)
                 for s, v, (_, h) in zip(raw, vs, pairs)]
        for r, (_, h), o_in, o_cr in zip(rows, pairs, intra, cross):
            hs = head_slices(h)[1]
            o = o_in + o_cr[0] * tab_ref[h, 0, 1] + o_cr[1] * tab_ref[h, 1, 1]
            mu = jnp.mean(o, axis=-1, keepdims=True)
            var = jnp.mean(jnp.square(o - mu), axis=-1, keepdims=True)
            on = (o - mu) * lax.rsqrt(var + EPS) * gnw_ref[:, hs] + gnb_ref[:, hs]
            gate = gate_ref[r, hs].astype(F32)
            o_ref[r, hs] = (_silu(gate) * on).astype(BF16)
        return carry

    lax.fori_loop(0, nc // per_trip, outputs, 0)


def _retention(z, l, n, dec, s0, gn_w, gn_b):
    r = z.shape[0]
    w = RET_WIDTH
    qk = RET_HEADS * RET_DK
    nc = n // RET_CHUNK
    col = lambda base, width: pl.BlockSpec((n, width), lambda b: (b, base // width))
    st_spec = pl.BlockSpec((None, RET_HEADS, 2, LANES, RET_DV), lambda b: (b, 0, 0, 0, 0))
    gn_spec = pl.BlockSpec((None, 1, w), lambda b: (l, 0, 0))
    return pl.pallas_call(
        functools.partial(_ret_kernel, n=n),
        grid=(BATCH,),
        in_specs=[col(Z_RQ, qk), col(Z_RK, qk), col(Z_RV, w), col(Z_RGATE, w),
                  pl.BlockSpec((None, 2, RET_HEADS, 8, LANES), lambda b: (l, 0, 0, 0, 0)),
                  st_spec, gn_spec, gn_spec],
        out_specs=[pl.BlockSpec((n, w), lambda b: (b, 0)), st_spec],
        out_shape=[jax.ShapeDtypeStruct((r, w), BF16),
                   jax.ShapeDtypeStruct((BATCH, RET_HEADS, 2, LANES, RET_DV), F32)],
        scratch_shapes=[pltpu.VMEM((RET_HEADS, 2, 5, RET_CHUNK, RET_CHUNK), F32),
                        pltpu.VMEM((nc, qk, RET_CHUNK), BF16),
                        pltpu.VMEM((2 * RET_HEADS, nc, LANES, RET_DV), F32),
                        pltpu.VMEM((2 * RET_HEADS, nc, LANES, RET_DV), BF16)]
        + [pltpu.VMEM((LANES, RET_DV), F32)] * (2 * RET_HEADS),
        compiler_params=_cparams("arbitrary"),
        name="retention",
    )(z, z, z, z, dec, s0, gn_w, gn_b)


def _hy_filter_kernel(feat_ref, w1_ref, b1_ref, w2_ref, b2_ref, w3_ref, dl_ref, s_ref, d_ref):
    hi = lax.Precision.HIGHEST
    feats = feat_ref[...]
    h = jnp.sin(HY_SIN_FREQ * (jnp.dot(feats, w1_ref[...], preferred_element_type=F32, precision=hi)
                               + b1_ref[...]))
    h = jnp.sin(HY_SIN_FREQ * (jnp.dot(h, w2_ref[...], preferred_element_type=F32, precision=hi)
                               + b2_ref[...]))
    h = jnp.dot(h, w3_ref[...], preferred_element_type=F32, precision=hi)
    window = jnp.exp(-feats[:, 0:1] * dl_ref[...]) + HY_WINDOW_SHIFT
    window = jnp.concatenate([window] * HY_ORDER, axis=-1)
    half = HY_ORDER * HY_WIDTH
    h_fwd = h[:, :half] * window
    h_bwd = h[:, half:] * window
    s_ref[...] = (h_fwd + h_bwd).astype(BF16)
    d_ref[...] = (h_fwd - h_bwd).astype(BF16)


def _hy_filter(feats, w1, b1, w2, b2, w3, deltas, l):
    n = feats.shape[0]
    tn = 256
    half = HY_ORDER * HY_WIDTH
    full = lambda a, b: pl.BlockSpec((None, a, b), lambda i: (l, 0, 0))
    return pl.pallas_call(
        _hy_filter_kernel,
        grid=(n // tn,),
        in_specs=[pl.BlockSpec((tn, LANES), lambda i: (i, 0)),
                  full(LANES, HY_FFN), full(1, HY_FFN), full(HY_FFN, HY_FFN), full(1, HY_FFN),
                  full(HY_FFN, 2 * half),
                  pl.BlockSpec((1, HY_WIDTH), lambda i: (0, 0))],
        out_specs=[pl.BlockSpec((tn, half), lambda i: (i, 0))] * 2,
        out_shape=[jax.ShapeDtypeStruct((n, half), BF16)] * 2,
        compiler_params=_cparams("arbitrary"),
        name="hyena_filter",
    )(feats, w1, b1, w2, b2, w3, deltas)


def _hy_spec_kernel(fc_ref, fs_ref, alt_ref, s_ref, d_ref, hc_ref, hs_ref, hn_ref):
    s = s_ref[...]
    tk, n = fc_ref.shape
    row = lax.broadcasted_iota(jnp.int32, (tk, s.shape[1]), 0) + pl.program_id(0) * tk
    wk = jnp.where(row == 0, 1.0, 2.0) * (1.0 / (2 * n))
    hc_ref[...] = jnp.dot(fc_ref[...], s, preferred_element_type=F32) * wk
    hs_ref[...] = jnp.dot(fs_ref[...], d_ref[...], preferred_element_type=F32) * wk
    hn_ref[...] = jnp.dot(alt_ref[...], s, preferred_element_type=F32)[0:8, :] * (1.0 / (2 * n))


def _hy_spectrum(fc, fs, alt, s, d):
    n = fc.shape[0]
    tk = min(512, n)
    half = s.shape[1]
    out = pl.BlockSpec((tk, half), lambda i: (i, 0))
    return pl.pallas_call(
        _hy_spec_kernel,
        grid=(n // tk,),
        in_specs=[pl.BlockSpec((tk, n), lambda i: (i, 0)),
                  pl.BlockSpec((tk, n), lambda i: (i, 0)),
                  pl.BlockSpec((16, n), lambda i: (0, 0)),
                  pl.BlockSpec((n, half), lambda i: (0, 0)),
                  pl.BlockSpec((n, half), lambda i: (0, 0))],
        out_specs=[out, out, pl.BlockSpec((8, half), lambda i: (0, 0))],
        out_shape=[jax.ShapeDtypeStruct((n, half), F32)] * 2
        + [jax.ShapeDtypeStruct((8, half), F32)],
        compiler_params=_cparams("arbitrary"),
        name="hyena_spectrum",
    )(fc, fs, alt, s, d)


def _short_conv_kernel(u_ref, w_ref, b_ref, o_ref, pad_ref, *, n):
    zeros = jnp.zeros((8, pad_ref.shape[1]), F32)
    pad_ref[0:8, :] = zeros
    pad_ref[8 + n:16 + n, :] = zeros
    pad_ref[8:8 + n, :] = u_ref[...].astype(F32)
    w = w_ref[...]
    y = (pad_ref[7:7 + n, :] * w[0:1, :] + pad_ref[8:8 + n, :] * w[1:2, :]
         + pad_ref[9:9 + n, :] * w[2:3, :] + b_ref[...])
    o_ref[...] = y.astype(BF16)


def _short_conv(z, l, n, w, b):
    r = z.shape[0]
    wc = HY_WIDTH
    return pl.pallas_call(
        functools.partial(_short_conv_kernel, n=n),
        grid=(BATCH, (HY_ORDER + 1)),
        in_specs=[pl.BlockSpec((n, wc), lambda bi, j: (bi, Z_HY // wc + j)),
                  pl.BlockSpec((None, 3, wc), lambda bi, j: (l, 0, j)),
                  pl.BlockSpec((None, 1, wc), lambda bi, j: (l, 0, j))],
        out_specs=pl.BlockSpec((n, wc), lambda bi, j: (bi, j)),
        out_shape=jax.ShapeDtypeStruct((r, (HY_ORDER + 1) * wc), BF16),
        scratch_shapes=[pltpu.VMEM((n + 16, wc), F32)],
        compiler_params=_cparams("arbitrary", "arbitrary"),
        name="short_conv",
    )(z, w, b)


HY_PAIR = 2


def _long_conv_kernel(u_ref, gate_ref, fc_ref, fs_ref, g_ref, alt_ref, hc_ref, hs_ref, hn_ref,
                      skip_ref, gain_ref, o_ref, y_ref, nyq_ref, *, n, tk, final):
    j = pl.program_id(1)
    kt = n // tk

    @pl.when(j == 0)
    def _():
        for p in range(HY_PAIR):
            x_nyq = jnp.dot(alt_ref[...], u_ref[p], preferred_element_type=F32)
            nyq_ref[p] = x_nyq[0:8, :] * hn_ref[...]

    @pl.when(j < kt)
    def _():
        r0 = pl.multiple_of(j * tk, tk)
        hc, hs = hc_ref[...], hs_ref[...]
        xs_all = [(jnp.dot(fc_ref[...], u_ref[p], preferred_element_type=F32),
                   jnp.dot(fs_ref[...], u_ref[p], preferred_element_type=F32))
                  for p in range(HY_PAIR)]
        for p, (xc, xs) in enumerate(xs_all):
            y_ref[p, pl.ds(r0, tk), :] = (xc * hc - xs * hs).astype(BF16)
            y_ref[p, pl.ds(n + r0, tk), :] = (xc * hs + xs * hc).astype(BF16)

    @pl.when(j >= kt)
    def _():
        t0 = pl.multiple_of((j - kt) * tk, tk)
        t = lax.broadcasted_iota(jnp.int32, (tk, u_ref.shape[-1]), 0)
        alt = (1 - 2 * (t & 1)).astype(F32)
        convs = [jnp.dot(g_ref[...], y_ref[p], preferred_element_type=F32)
                 for p in range(HY_PAIR)]
        for p, conv in enumerate(convs):
            u = u_ref[p, pl.ds(t0, tk), :].astype(F32)
            y = conv + alt * nyq_ref[p][0:1, :] + u * skip_ref[...]
            y = gate_ref[p].astype(F32) * y
            if final:
                y = _rms(y) * gain_ref[...]
            o_ref[p] = y.astype(BF16)


def _long_conv(src, u_col, gate_col, mats, spec, order, skip, gain, l, n, final):
    fc, fs, g, alt = mats
    hc, hs, hn = spec
    r = src[0].shape[0]
    wc = HY_WIDTH
    tk = min(512, n)
    kt = n // tk
    nb = BATCH // HY_PAIR
    u4, gate4 = (a.reshape(nb, HY_PAIR, n, a.shape[-1]) for a in src)
    f_spec = pl.BlockSpec((tk, n), lambda bi, j: (jnp.minimum(j, kt - 1), 0))
    h_spec = pl.BlockSpec((tk, wc), lambda bi, j: (jnp.minimum(j, kt - 1), order))
    tile = lambda col: pl.BlockSpec((None, HY_PAIR, tk, wc),
                                    lambda bi, j: (bi, 0, jnp.maximum(j - kt, 0), col))
    out = pl.pallas_call(
        functools.partial(_long_conv_kernel, n=n, tk=tk, final=final),
        grid=(nb, 2 * kt),
        in_specs=[pl.BlockSpec((None, HY_PAIR, n, wc), lambda bi, j: (bi, 0, 0, u_col)),
                  tile(gate_col),
                  f_spec, f_spec,
                  pl.BlockSpec((tk, 2 * n), lambda bi, j: (jnp.maximum(j - kt, 0), 0)),
                  pl.BlockSpec((16, n), lambda bi, j: (0, 0)),
                  h_spec, h_spec,
                  pl.BlockSpec((8, wc), lambda bi, j: (0, order)),
                  pl.BlockSpec((None, None, 1, wc), lambda bi, j: (l, order, 0, 0)),
                  pl.BlockSpec((None, 1, wc), lambda bi, j: (l, 0, 0))],
        out_specs=tile(0),
        out_shape=jax.ShapeDtypeStruct((nb, HY_PAIR, n, wc), BF16),
        scratch_shapes=[pltpu.VMEM((HY_PAIR, 2 * n, wc), BF16),
                        pltpu.VMEM((HY_PAIR, 8, wc), F32)],
        compiler_params=_cparams("arbitrary", "arbitrary"),
        name="long_conv",
    )(u4, gate4, fc, fs, g, alt, hc, hs, hn, skip, gain)
    return out.reshape(r, wc)


def _dft_mats(n):
    big = 2 * n
    fine = 32
    t = jnp.arange(n, dtype=jnp.int32)

    def trig(k):
        ang = ((k[:, None] * t[None, :]) % big).astype(F32) * (2.0 * math.pi / big)
        return jnp.cos(ang), jnp.sin(ang)

    ca, sa = (m[:, None, :] for m in trig(fine * jnp.arange(n // fine, dtype=jnp.int32)))
    cb, sb = (m[None, :, :] for m in trig(jnp.arange(fine, dtype=jnp.int32)))
    fc = (ca * cb - sa * sb).reshape(n, n).astype(BF16)
    fs = (sa * cb + ca * sb).reshape(n, n).astype(BF16)
    alt = jnp.broadcast_to((1 - 2 * (t % 2)).astype(F32)[None, :], (16, n))
    return fc, fs, jnp.concatenate([fc, fs], axis=1), alt.astype(BF16)


def _hy_features(n):
    t = jnp.linspace(0.0, 1.0, n, dtype=F32)[:, None]
    bands = (HY_EMB - 1) // 2
    w = 2 * math.pi * jnp.arange(n, dtype=F32)[:, None] / n
    f = jnp.linspace(1e-4, bands - 1, bands, dtype=F32)[None, :]
    feats = jnp.concatenate([t, jnp.cos(f * w), -jnp.sin(f * w)], axis=-1)
    return jnp.pad(feats, ((0, 0), (0, LANES - HY_EMB)))


def _rope_table():
    n_freq = MLA_ROPE // 4
    inv_freq = ROPE_BASE ** (-jnp.arange(n_freq, dtype=F32) / n_freq)
    rows = SEQ // GRID_W
    row = jnp.repeat(jnp.arange(rows), GRID_W).astype(F32)
    col = jnp.tile(jnp.arange(GRID_W), rows).astype(F32)
    ar = row[:, None] * inv_freq[None, :]
    ac = col[:, None] * inv_freq[None, :]
    cos_part = jnp.concatenate([jnp.cos(ar), jnp.cos(ar), jnp.cos(ac), jnp.cos(ac)], axis=-1)
    sin_part = jnp.concatenate([-jnp.sin(ar), jnp.sin(ar), -jnp.sin(ac), jnp.sin(ac)], axis=-1)
    return jnp.concatenate([cos_part, sin_part], axis=-1)


_ROPE_PARTNER = np.concatenate([np.arange(16, 32), np.arange(0, 16),
                                np.arange(48, 64), np.arange(32, 48)])
_QUARTER = MLA_ROPE // 4


def _partner_pieces(base):
    return [(base + int(_ROPE_PARTNER[i]), _QUARTER) for i in range(0, MLA_ROPE, _QUARTER)]


def _w_in_pieces():
    off = np.cumsum([0, MLA_KV_LORA, MLA_ROPE, RET_HEADS * RET_DK, RET_WIDTH, MLA_Q_LORA,
                     RET_HEADS * RET_DK, RET_WIDTH])
    o_kv, o_kr, o_rk, o_rv, o_q, o_rq, o_gate, o_hy = [int(v) for v in off]
    pieces = [(o_hy, 3 * HY_WIDTH), (o_rv, RET_WIDTH), (o_q, MLA_Q_LORA), (o_gate, RET_WIDTH),
              (o_rq, RET_HEADS * RET_DK), (o_rk, RET_HEADS * RET_DK),
              (o_kv, MLA_KV_LORA), (o_kr, MLA_ROPE)] + _partner_pieces(o_kr)
    pieces.append((None, Z_COLS - sum(w for _, w in pieces)))
    return pieces


def _wuq_pieces():
    per = MLA_NOPE + MLA_ROPE
    pieces = []
    for h in range(MLA_HEADS):
        pieces += [(h * per, per)] + _partner_pieces(h * per + MLA_NOPE)
    return pieces


def _relayout_kernel(w_ref, o_ref, *, pieces):
    col = 0
    for start, width in pieces:
        if start is None:
            o_ref[:, col:col + width] = jnp.zeros((o_ref.shape[0], width), o_ref.dtype)
        else:
            o_ref[:, col:col + width] = w_ref[:, start:start + width].astype(o_ref.dtype)
        col += width


def _relayout_cols(w, pieces):
    nl, k, n_in = w.shape
    n_out = sum(width for _, width in pieces)
    tr = 256
    return pl.pallas_call(
        functools.partial(_relayout_kernel, pieces=tuple(pieces)),
        grid=(nl, k // tr),
        in_specs=[pl.BlockSpec((None, tr, n_in), lambda l, i: (l, i, 0))],
        out_specs=pl.BlockSpec((None, tr, n_out), lambda l, i: (l, i, 0)),
        out_shape=jax.ShapeDtypeStruct((nl, k, n_out), BF16),
        compiler_params=_cparams("arbitrary", "arbitrary"),
        name="relayout_cols",
    )(w)


def _relayout_rows_kernel(w_ref, o_ref, *, pieces):
    row = 0
    for start, width in pieces:
        if start is None:
            o_ref[row:row + width, :] = jnp.zeros((width, o_ref.shape[1]), o_ref.dtype)
        else:
            o_ref[row:row + width, :] = w_ref[start:start + width, :].astype(o_ref.dtype)
        row += width


def _relayout_rows(w_t, pieces):
    nl, n_in, k = w_t.shape
    n_out = sum(width for _, width in pieces)
    tc = 512
    return pl.pallas_call(
        functools.partial(_relayout_rows_kernel, pieces=tuple(pieces)),
        grid=(nl, k // tc),
        in_specs=[pl.BlockSpec((None, n_in, tc), lambda l, i: (l, 0, i))],
        out_specs=pl.BlockSpec((None, n_out, tc), lambda l, i: (l, 0, i)),
        out_shape=jax.ShapeDtypeStruct((nl, n_out, k), BF16),
        compiler_params=_cparams("arbitrary", "arbitrary"),
        name="relayout_rows",
    )(w_t)


def kernel(x, c, ctx, c_ctx, ada_w, ada_b, ffn1_gate, ffn1_up, ffn1_down, w_in, mla_q_norm, mla_wuq, mla_kv_norm, mla_wukv, mla_qn_nope, mla_qn_rope, mla_kn_nope, mla_kn_rope, mla_out_norm, hy_conv_w, hy_conv_b, hy_ffn_w1, hy_ffn_b1, hy_ffn_w2, hy_ffn_b2, hy_ffn_w3, hy_skip, hy_out_norm, ret_decay, ret_gn_w, ret_gn_b, w_out, ffn2_gate, ffn2_up, ffn2_down):
    nl = DEPTH
    d = D_MODEL

    w1g, w1u, w1d = ffn1_gate, ffn1_up, ffn1_down.astype(BF16)
    w2g, w2u, w2d = ffn2_gate, ffn2_up, ffn2_down.astype(BF16)
    w_in_p = _relayout_rows(jnp.swapaxes(w_in, 1, 2), _w_in_pieces())
    wuq_p = _relayout_cols(mla_wuq, _wuq_pieces())
    wukv = mla_wukv.astype(BF16)
    w_out_b = w_out.astype(BF16)
    row3 = lambda g: g.reshape(nl, 1, -1)
    partner = jnp.asarray(_ROPE_PARTNER)
    g_qr = row3(jnp.concatenate([mla_qn_rope, mla_qn_rope[:, partner]], axis=-1))
    g_kr = row3(jnp.concatenate([mla_kn_rope, mla_kn_rope[:, partner]], axis=-1))
    g_q, g_kv, g_qn, g_kn = row3(mla_q_norm), row3(mla_kv_norm), row3(mla_qn_nope), row3(mla_kn_nope)
    g_attn, g_hy = row3(mla_out_norm), row3(hy_out_norm)
    gn_w, gn_b = row3(ret_gn_w), row3(ret_gn_b)
    hy_b = row3(hy_conv_b)
    hy_w1 = jnp.pad(hy_ffn_w1, ((0, 0), (0, LANES - HY_EMB), (0, 0)))
    hy_b1, hy_b2 = row3(hy_ffn_b1), row3(hy_ffn_b2)
    hy_skip4 = hy_skip.reshape(nl, HY_ORDER, 1, HY_WIDTH)
    dec = jnp.broadcast_to(ret_decay.astype(F32)[:, :, :, None, None], (nl, 2, RET_HEADS, 8, LANES))

    tab_x = _rope_table()
    tab_c = jnp.concatenate([jnp.ones((CTX_LEN, MLA_ROPE), F32), jnp.zeros((CTX_LEN, MLA_ROPE), F32)], -1)
    deltas = jnp.abs(jnp.linspace(HY_MIN_DECAY, HY_MAX_DECAY, HY_WIDTH, dtype=F32))[None, :]
    mats = {n: _dft_mats(n) for n in (SEQ, CTX_LEN)}
    feats = {n: _hy_features(n) for n in (SEQ, CTX_LEN)}

    cond = jnp.zeros((16, d), F32).at[0].set(c_ctx).at[1:1 + BATCH].set(c)
    mods = _modulation(cond, ada_w, ada_b).reshape(nl, 16, N_MOD, d)

    sx = x.reshape(BATCH * SEQ, d)
    sc = ctx.reshape(BATCH * CTX_LEN, d)

    def mod_tables(l):
        return ([mods[l, 1:1 + BATCH, j][:, None, :] for j in range(N_MOD)],
                [mods[l, 0:1, j][:, None, :] for j in range(N_MOD)])

    all_mods = [mod_tables(l) for l in range(nl)]

    def ffn(s, h, n, gate, wg, wu, wd, l):
        return _ffn_down(_ffn_up(h, wg, wu, l), wd, l, s, gate, n)

    def hyena(z, l, n, spec):
        cv = _short_conv(z, l, n, hy_conv_w, hy_b)
        y1 = _long_conv((cv, cv), 2, 0, mats[n], spec, 0, hy_skip4, g_hy, l, n, False)
        return _long_conv((y1, cv), 0, 1, mats[n], spec, 1, hy_skip4, g_hy, l, n, True)

    for l in range(nl):
        need_ctx_out = l < nl - 1
        mod_x, mod_c = all_mods[l]

        sx = ffn(sx, _normmod(sx, mod_x[0], mod_x[1], SEQ), SEQ, mod_x[2], w1g, w1u, w1d, l)
        sc = ffn(sc, _normmod(sc, mod_c[0], mod_c[1], CTX_LEN), CTX_LEN, mod_c[2], w1g, w1u, w1d, l)

        zc = _norm_matmul(sc, mod_c[3], mod_c[4], CTX_LEN, w_in_p, l)
        zx = _norm_matmul(sx, mod_x[3], mod_x[4], SEQ, w_in_p, l)

        kc, vc = _kv_proj(zc, l, CTX_LEN, wukv, g_kv, g_kn, g_kr, tab_c)
        kx, vx = _kv_proj(zx, l, SEQ, wukv, g_kv, g_kn, g_kr, tab_x)
        zero_state = jnp.zeros((BATCH, RET_HEADS, 2, LANES, RET_DV), F32)
        ret_c, state_c = _retention(zc, l, CTX_LEN, dec, zero_state, gn_w, gn_b)

        filt = {}
        for n in ((SEQ, CTX_LEN) if need_ctx_out else (SEQ,)):
            s_f, d_f = _hy_filter(feats[n], hy_w1, hy_b1, hy_ffn_w2, hy_b2, hy_ffn_w3, deltas, l)
            fc, fs, _, alt = mats[n]
            filt[n] = _hy_spectrum(fc, fs, alt, s_f, d_f)

        qx = _q_proj(zx, l, SEQ, wuq_p, g_q, g_qn, g_qr, tab_x)
        attn_x = _attention(qx, [kc, kx], [vc, vx], [CTX_LEN, SEQ], l, g_attn)
        hy_x = hyena(zx, l, SEQ, filt[SEQ])
        ret_x, _ = _retention(zx, l, SEQ, dec, state_c, gn_w, gn_b)
        sx, hx = _w_out([attn_x, hy_x, ret_x], w_out_b, l, sx, mod_x[5], SEQ, (mod_x[6], mod_x[7]))
        sx = ffn(sx, hx, SEQ, mod_x[8], w2g, w2u, w2d, l)

        if need_ctx_out:
            qc = _q_proj(zc, l, CTX_LEN, wuq_p, g_q, g_qn, g_qr, tab_c)
            attn_c = _attention(qc, [kc], [vc], [CTX_LEN], l, g_attn)
            hy_c = hyena(zc, l, CTX_LEN, filt[CTX_LEN])
            sc, hc = _w_out([attn_c, hy_c, ret_c], w_out_b, l, sc, mod_c[5], CTX_LEN,
                            (mod_c[6], mod_c[7]))
            sc = ffn(sc, hc, CTX_LEN, mod_c[8], w2g, w2u, w2d, l)

    return sx.reshape(BATCH, SEQ, d)
```

```python
import functools
import math

import jax
import jax.numpy as jnp
import numpy as np
from jax import lax
from jax.experimental import pallas as pl
from jax.experimental.pallas import tpu as pltpu

F32 = jnp.float32
BF16 = jnp.bfloat16

D_MODEL = 2048
BATCH = 8
SEQ = 2048
DEPTH = 2
GRID_W = 64
CTX_LEN = 256
EPS = 1e-6
N_MOD = 9
FFN_HIDDEN = 5632

MLA_HEADS = 8
MLA_NOPE = 128
MLA_ROPE = 64
MLA_V = 128
MLA_Q_LORA = 512
MLA_KV_LORA = 256
MLA_WIDTH = MLA_HEADS * MLA_V
ROPE_BASE = 10000.0

HY_WIDTH = 512
HY_ORDER = 2
HY_EMB = 33
HY_FFN = 64
HY_SIN_FREQ = 1.0
HY_MIN_DECAY = math.log(1e-2) / 1.5
HY_MAX_DECAY = math.log(1e-2) / 0.3
HY_WINDOW_SHIFT = 0.05

RET_HEADS = 4
RET_DK = 64
RET_DV = 128
RET_WIDTH = RET_HEADS * RET_DV
RET_CHUNK = 128

VMEM_LIMIT_BYTES = 60 * 1024 * 1024
LANES = 128
HEAD_SLOT = 256

Z_HY = 0
Z_RV = 1536
Z_QLAT = 2048
Z_RGATE = 2560
Z_RQ = 3072
Z_RK = 3328
Z_KVLAT = 3584
Z_KR = 3840
Z_COLS = 4096

TM = 1024
TM_WIDE = 2048
TN = 512


def _cparams(*sem):
    return pltpu.CompilerParams(dimension_semantics=sem, vmem_limit_bytes=VMEM_LIMIT_BYTES)


def _rms(x, width=None):
    width = x.shape[-1] if width is None else width
    return x * lax.rsqrt(jnp.sum(x * x, axis=-1, keepdims=True) / width + EPS)


def _silu(x):
    return x * jax.nn.sigmoid(x)


def _dot_3pass(a, w):
    a_hi, w_hi = a.astype(BF16), w.astype(BF16)
    a_lo = (a - a_hi.astype(F32)).astype(BF16)
    w_lo = (w - w_hi.astype(F32)).astype(BF16)
    rows = a.shape[0]
    acc = jnp.dot(jnp.concatenate([a_hi, a_lo], axis=0), w_hi, preferred_element_type=F32)
    return acc[:rows] + acc[rows:] + jnp.dot(a_hi, w_lo, preferred_element_type=F32)


def _mod_kernel(c_ref, w_ref, b_ref, o_ref):
    o_ref[...] = _dot_3pass(_silu(c_ref[...]), w_ref[...]) + b_ref[...]


def _modulation(cond, ada_w, ada_b):
    nl, d, nout = ada_w.shape
    bn = 1024
    return pl.pallas_call(
        _mod_kernel,
        grid=(nl, nout // bn),
        in_specs=[pl.BlockSpec((16, d), lambda l, j: (0, 0)),
                  pl.BlockSpec((None, d, bn), lambda l, j: (l, 0, j)),
                  pl.BlockSpec((None, 1, bn), lambda l, j: (l, 0, j))],
        out_specs=pl.BlockSpec((None, 16, bn), lambda l, j: (l, 0, j)),
        out_shape=jax.ShapeDtypeStruct((nl, 16, nout), F32),
        compiler_params=_cparams("arbitrary", "arbitrary"),
        name="modulation",
    )(cond, ada_w, ada_b.reshape(nl, 1, nout))


def _normmod_kernel(x_ref, sh_ref, sc_ref, o_ref):
    y = _rms(x_ref[...])
    o_ref[...] = (y * (1.0 + sc_ref[...]) + sh_ref[...]).astype(BF16)


def _normmod(s, shift, scale, n):
    r, d = s.shape
    tm = min(512, n)
    per_batch = shift.shape[0] > 1
    mod_spec = pl.BlockSpec((None, 1, d), lambda i: ((i * tm) // n if per_batch else 0, 0, 0))
    return pl.pallas_call(
        _normmod_kernel,
        grid=(r // tm,),
        in_specs=[pl.BlockSpec((tm, d), lambda i: (i, 0)), mod_spec, mod_spec],
        out_specs=pl.BlockSpec((tm, d), lambda i: (i, 0)),
        out_shape=jax.ShapeDtypeStruct((r, d), BF16),
        compiler_params=_cparams("arbitrary"),
        name="normmod",
    )(s, shift, scale)


def _ffn_up_kernel(h_ref, wg_ref, wu_ref, *refs, round_wd):
    if round_wd:
        wd_ref, o_ref, wd_bf, wg_bf, wu_bf = refs
    else:
        o_ref, wg_bf, wu_bf = refs

    @pl.when(pl.program_id(1) == 0)
    def _():
        wg_bf[...] = wg_ref[...].astype(BF16)
        wu_bf[...] = wu_ref[...].astype(BF16)
        if round_wd:
            wd_bf[...] = wd_ref[...].astype(BF16)

    h = h_ref[...]
    g = jnp.dot(h, wg_bf[...], preferred_element_type=F32)
    u = jnp.dot(h, wu_bf[...], preferred_element_type=F32)
    o_ref[...] = (_silu(g) * u).astype(BF16)


def _ffn_up(h, wg, wu, l, wd=None):
    r, d = h.shape
    f = wg.shape[-1]
    w_spec = pl.BlockSpec((None, d, TN), lambda j, i: (l, 0, j))
    tm = TM
    round_wd = wd is not None
    u_spec = pl.BlockSpec((tm, TN), lambda j, i: (i, j))
    u_shape = jax.ShapeDtypeStruct((r, f), BF16)
    slab = pl.BlockSpec((TN, d), lambda j, i: (j, 0))
    outs = pl.pallas_call(
        functools.partial(_ffn_up_kernel, round_wd=round_wd),
        grid=(f // TN, r // tm),
        in_specs=[pl.BlockSpec((tm, d), lambda j, i: (i, 0)), w_spec, w_spec]
        + ([pl.BlockSpec((None, TN, d), lambda j, i: (l, j, 0))] if round_wd else []),
        out_specs=[u_spec, slab] if round_wd else u_spec,
        out_shape=[u_shape, jax.ShapeDtypeStruct((f, d), BF16)] if round_wd else u_shape,
        scratch_shapes=[pltpu.VMEM((d, TN), BF16), pltpu.VMEM((d, TN), BF16)],
        compiler_params=_cparams("arbitrary", "arbitrary"),
        name="ffn_up",
    )(h, wg, wu, *([wd] if round_wd else []))
    return outs if round_wd else (outs, None)


def _mod_spec(table, tm, n, d):
    per_batch = table.shape[0] > 1
    return pl.BlockSpec((None, 1, d), lambda i, *_: ((i * tm) // n if per_batch else 0, 0, 0))


def _norm_mm_kernel(s_ref, sh_ref, sc_ref, w_ref, o_ref, h_ref):
    @pl.when(pl.program_id(1) == 0)
    def _():
        h_ref[...] = (_rms(s_ref[...]) * (1.0 + sc_ref[...]) + sh_ref[...]).astype(BF16)

    o_ref[...] = lax.dot_general(h_ref[...], w_ref[...], (((1,), (1,)), ((), ())),
                                 preferred_element_type=F32).astype(o_ref.dtype)


def _norm_matmul(s, shift, scale, n, w_t, l):
    r, d = s.shape
    nout = w_t.shape[1]
    tm = TM_WIDE
    return pl.pallas_call(
        _norm_mm_kernel,
        grid=(r // tm, nout // TN),
        in_specs=[pl.BlockSpec((tm, d), lambda i, j: (i, 0)),
                  _mod_spec(shift, tm, n, d), _mod_spec(scale, tm, n, d),
                  pl.BlockSpec((None, TN, d), lambda i, j: (l, j, 0))],
        out_specs=pl.BlockSpec((tm, TN), lambda i, j: (i, j)),
        out_shape=jax.ShapeDtypeStruct((r, nout), BF16),
        scratch_shapes=[pltpu.VMEM((tm, d), BF16)],
        compiler_params=_cparams("arbitrary", "arbitrary"),
        name="norm_matmul",
    )(s, shift, scale, w_t)


def _ffn_down_kernel(u_ref, w_ref, s_ref, g_ref, o_ref):
    acc = jnp.dot(u_ref[...], w_ref[...], preferred_element_type=F32)
    o_ref[...] = s_ref[...] + (0.5 * g_ref[...]) * acc


def _ffn_down(u, w, s, gate, n):
    r, d = s.shape
    f = u.shape[1]
    per_batch = gate.shape[0] > 1
    return pl.pallas_call(
        _ffn_down_kernel,
        grid=(r // TM, d // TN),
        in_specs=[pl.BlockSpec((TM, f), lambda i, j: (i, 0)),
                  pl.BlockSpec((f, TN), lambda i, j: (0, j)),
                  pl.BlockSpec((TM, TN), lambda i, j: (i, j)),
                  pl.BlockSpec((None, 1, TN), lambda i, j: ((i * TM) // n if per_batch else 0, 0, j))],
        out_specs=pl.BlockSpec((TM, TN), lambda i, j: (i, j)),
        out_shape=jax.ShapeDtypeStruct((r, d), F32),
        compiler_params=_cparams("arbitrary", "arbitrary"),
        name="ffn_down",
    )(u, w, s, gate)


def _w_out_kernel(*refs, n_in):
    a_refs = refs[:n_in]
    w_ref, s_ref, g_ref, sh_ref, sc_ref, o_ref, h_ref = refs[n_in:]
    half = s_ref.shape[0] // 2
    accs = []
    for r0 in (0, half):
        acc, row = None, 0
        for a_ref in a_refs:
            kp = a_ref.shape[1]
            part = jnp.dot(a_ref[r0:r0 + half, :], w_ref[row:row + kp, :],
                           preferred_element_type=F32)
            acc = part if acc is None else acc + part
            row += kp
        accs.append(acc)
    for r0, acc in zip((0, half), accs):
        rows = slice(r0, r0 + half)
        s_new = s_ref[rows, :] + g_ref[...] * acc
        o_ref[rows, :] = s_new
        h_ref[rows, :] = (_rms(s_new) * (1.0 + sc_ref[...]) + sh_ref[...]).astype(BF16)


def _w_out(parts, w, l, s, gate, n, next_mod):
    r, d = s.shape
    tm = TM // 2
    shift, scale = next_mod
    row = pl.BlockSpec((tm, d), lambda i: (i, 0))
    return pl.pallas_call(
        functools.partial(_w_out_kernel, n_in=len(parts)),
        grid=(r // tm,),
        in_specs=[pl.BlockSpec((tm, p.shape[1]), lambda i: (i, 0)) for p in parts]
        + [pl.BlockSpec((None, d, d), lambda i: (l, 0, 0)),
           row, _mod_spec(gate, tm, n, d), _mod_spec(shift, tm, n, d), _mod_spec(scale, tm, n, d)],
        out_specs=[row, row],
        out_shape=[jax.ShapeDtypeStruct((r, d), F32), jax.ShapeDtypeStruct((r, d), BF16)],
        compiler_params=_cparams("arbitrary"),
        name="w_out",
    )(*parts, w, s, gate, shift, scale)


def _rope_lanes(x):
    lane = lax.broadcasted_iota(jnp.int32, x.shape, 1)
    ms = jnp.sum(jnp.where(lane < MLA_ROPE, x * x, 0.0), axis=-1, keepdims=True) / MLA_ROPE
    return x * lax.rsqrt(ms + EPS)


def _kv_kernel(lat_ref, kr_ref, gkv_ref, w_ref, gkn_ref, gkr_ref, tab_ref, k_ref, v_ref):
    a = _rms(lat_ref[...].astype(F32)) * gkv_ref[...]
    kv = jnp.dot(a.astype(BF16), w_ref[...], preferred_element_type=F32)
    t = _rope_lanes(kr_ref[...].astype(F32)) * gkr_ref[...] * tab_ref[...]
    krot = (t + pltpu.roll(t, MLA_ROPE, 1)).astype(BF16)
    for h in range(MLA_HEADS):
        c0 = h * HEAD_SLOT
        kn = _rms(kv[:, c0:c0 + MLA_NOPE]) * gkn_ref[...]
        k_ref[:, c0:c0 + MLA_NOPE] = kn.astype(BF16)
        k_ref[:, c0 + MLA_NOPE:c0 + HEAD_SLOT] = krot
        v_ref[:, h * MLA_V:(h + 1) * MLA_V] = kv[:, c0 + MLA_NOPE:c0 + HEAD_SLOT].astype(BF16)


def _kv_proj(z, l, n, wukv, g_kv, g_kn, g_kr, tab):
    r = z.shape[0]
    tm = min(512, n)
    nt = tab.shape[0] // tm
    vec = lambda w: pl.BlockSpec((None, 1, w), lambda i: (l, 0, 0))
    return pl.pallas_call(
        _kv_kernel,
        grid=(r // tm,),
        in_specs=[pl.BlockSpec((tm, MLA_KV_LORA), lambda i: (i, Z_KVLAT // MLA_KV_LORA)),
                  pl.BlockSpec((tm, LANES), lambda i: (i, Z_KR // LANES)),
                  vec(MLA_KV_LORA),
                  pl.BlockSpec((None, MLA_KV_LORA, MLA_HEADS * HEAD_SLOT), lambda i: (l, 0, 0)),
                  vec(MLA_NOPE), vec(LANES),
                  pl.BlockSpec((tm, LANES), lambda i: (i % nt, 0))],
        out_specs=[pl.BlockSpec((tm, MLA_HEADS * HEAD_SLOT), lambda i: (i, 0)),
                   pl.BlockSpec((tm, MLA_WIDTH), lambda i: (i, 0))],
        out_shape=[jax.ShapeDtypeStruct((r, MLA_HEADS * HEAD_SLOT), BF16),
                   jax.ShapeDtypeStruct((r, MLA_WIDTH), BF16)],
        compiler_params=_cparams("arbitrary"),
        name="kv_proj",
    )(z, z, g_kv, wukv, g_kn, g_kr, tab)


def _q_kernel(lat_ref, gq_ref, w_ref, gn_ref, gr_ref, tab_ref, q_ref, *, scale):
    a = _rms(lat_ref[...].astype(F32)) * gq_ref[...]
    q = jnp.dot(a.astype(BF16), w_ref[...], preferred_element_type=F32)
    rope_scale = gr_ref[...] * tab_ref[...] * scale
    for h in range(MLA_HEADS):
        c0 = h * HEAD_SLOT
        qn = _rms(q[:, c0:c0 + MLA_NOPE]) * (gn_ref[...] * scale)
        q_ref[:, c0:c0 + MLA_NOPE] = qn.astype(BF16)
        qr = _rope_lanes(q[:, c0 + MLA_NOPE:c0 + HEAD_SLOT]) * rope_scale
        q_ref[:, c0 + MLA_NOPE:c0 + HEAD_SLOT] = qr.astype(BF16)


def _q_proj(z, l, n, wuq, g_q, g_qn, g_qr, tab):
    r = z.shape[0]
    tm = min(512, n)
    nt = tab.shape[0] // tm
    vec = lambda w: pl.BlockSpec((None, 1, w), lambda i: (l, 0, 0))
    return pl.pallas_call(
        functools.partial(_q_kernel, scale=(MLA_NOPE + MLA_ROPE) ** -0.5 * math.log2(math.e)),
        grid=(r // tm,),
        in_specs=[pl.BlockSpec((tm, MLA_Q_LORA), lambda i: (i, Z_QLAT // MLA_Q_LORA)),
                  vec(MLA_Q_LORA),
                  pl.BlockSpec((None, MLA_Q_LORA, MLA_HEADS * HEAD_SLOT), lambda i: (l, 0, 0)),
                  vec(MLA_NOPE), vec(LANES),
                  pl.BlockSpec((tm, LANES), lambda i: (i % nt, 0))],
        out_specs=pl.BlockSpec((tm, MLA_HEADS * HEAD_SLOT), lambda i: (i, 0)),
        out_shape=jax.ShapeDtypeStruct((r, MLA_HEADS * HEAD_SLOT), BF16),
        compiler_params=_cparams("arbitrary"),
        name="q_proj",
    )(z, g_q, wuq, g_qn, g_qr, tab)


def _attn_kernel(*refs, n_src):
    q_ref = refs[0]
    k_refs, v_refs = refs[1:1 + n_src], refs[1 + n_src:1 + 2 * n_src]
    g_ref, o_ref, acc_ref = refs[1 + 2 * n_src:]

    def scores(h):
        hs = slice(h * HEAD_SLOT, (h + 1) * HEAD_SLOT)
        return [lax.dot_general(q_ref[:, hs], k_ref[:, hs], (((1,), (1,)), ((), ())),
                                preferred_element_type=F32) for k_ref in k_refs]

    ssq = None
    ss_next = scores(0)
    for h in range(MLA_HEADS):
        hs = slice(h * HEAD_SLOT, (h + 1) * HEAD_SLOT)
        ss, ss_next = ss_next, (scores(h + 1) if h + 1 < MLA_HEADS else None)
        m = functools.reduce(jnp.maximum, [jnp.max(s, axis=-1, keepdims=True) for s in ss])
        vs = [jnp.concatenate([v_ref[:, h * MLA_V:(h + 1) * MLA_V],
                               jnp.ones((v_ref.shape[0], HEAD_SLOT - MLA_V), BF16)], axis=1)
              for v_ref in v_refs]
        o = functools.reduce(jnp.add, [
            jnp.dot(jnp.exp2((s - m).astype(BF16)), v, preferred_element_type=F32)
            for s, v in zip(ss, vs)])
        o = o[:, :MLA_V] / o[:, MLA_V:MLA_V + 1]
        acc_ref[:, h * MLA_V:(h + 1) * MLA_V] = o
        sq = jnp.sum(o * o, axis=-1, keepdims=True)
        ssq = sq if ssq is None else ssq + sq
    o_ref[...] = (acc_ref[...] * lax.rsqrt(ssq / MLA_WIDTH + EPS) * g_ref[...]).astype(BF16)


def _attention(q, ks, vs, ns, l, g_out):
    r = q.shape[0]
    nq = r // BATCH
    tq = min(512, nq)
    per_b = nq // tq
    k_specs = [pl.BlockSpec((nk, MLA_HEADS * HEAD_SLOT), lambda b, i: (b, 0)) for nk in ns]
    v_specs = [pl.BlockSpec((nk, MLA_WIDTH), lambda b, i: (b, 0)) for nk in ns]
    return pl.pallas_call(
        functools.partial(_attn_kernel, n_src=len(ks)),
        grid=(BATCH, per_b),
        in_specs=[pl.BlockSpec((tq, MLA_HEADS * HEAD_SLOT), lambda b, i: (b * per_b + i, 0))]
        + k_specs + v_specs
        + [pl.BlockSpec((None, 1, MLA_WIDTH), lambda b, i: (l, 0, 0))],
        out_specs=pl.BlockSpec((tq, MLA_WIDTH), lambda b, i: (b * per_b + i, 0)),
        out_shape=jax.ShapeDtypeStruct((r, MLA_WIDTH), BF16),
        scratch_shapes=[pltpu.VMEM((tq, MLA_WIDTH), F32)],
        compiler_params=_cparams("arbitrary", "arbitrary"),
        name="attention",
    )(q, *ks, *vs, g_out)


def _log_sigmoid(x):
    return jnp.minimum(x, 0.0) - jnp.log(1.0 + jnp.exp(-jnp.abs(x)))


def _ret_kernel(q_ref, k_ref, v_ref, gate_ref, dec_ref, s0_ref, gnw_ref, gnb_ref,
                o_ref, st_ref, tab_ref, kt_ref, u_ref, sin_ref, *state_refs, n):
    nc = n // RET_CHUNK
    c = RET_CHUNK

    @pl.when(pl.program_id(0) == 0)
    def _():
        ii = lax.broadcasted_iota(jnp.int32, (c, c), 0).astype(F32)
        jj = lax.broadcasted_iota(jnp.int32, (c, c), 1).astype(F32)
        for h in range(RET_HEADS):
            for d in range(2):
                lg = _log_sigmoid(dec_ref[d, h][0:1, :])
                lgm = jnp.broadcast_to(lg, (c, c))
                if d == 0:
                    diff, qpow, kpow = ii - jj, ii + 1.0, (c - 1.0) - ii
                else:
                    diff, qpow, kpow = jj - ii, c - ii, ii
                intra = jnp.where(diff >= 0, jnp.exp(lgm * jnp.maximum(diff, 0.0)), 0.0)
                if d == 0:
                    tab_ref[h, 0, 0] = intra
                else:
                    tab_ref[h, 0, 0] += intra
                tab_ref[h, d, 1] = jnp.exp(lgm * qpow)
                kpow_t = (c - 1.0) - jj if d == 0 else jj
                k_scale = jnp.where(ii.astype(jnp.int32) // RET_DK == h % 2, RET_DK ** -0.5, 0.0)
                tab_ref[h, d, 2] = jnp.exp(lgm * kpow_t) * k_scale
                tab_ref[h, d, 3] = jnp.exp(lgm * float(c))
                tab_ref[h, d, 4] = k_scale

    def head_slices(h):
        return slice((h // 2) * LANES, (h // 2 + 1) * LANES), slice(h * RET_DV, (h + 1) * RET_DV)

    chains = [(h, d) for h in range(RET_HEADS) for d in range(2)]

    def contributions(ci, carry):
        r0 = pl.multiple_of(ci * c, c)
        kt_all = k_ref[pl.ds(r0, c), :].astype(F32).T
        kt_ref[ci] = kt_all.astype(BF16)
        for h in range(RET_HEADS):
            grp, hs = head_slices(h)
            v = v_ref[pl.ds(r0, c), hs]
            for d in range(2):
                kd_t = (kt_all[grp, :] * tab_ref[h, d, 2]).astype(BF16)
                u_ref[2 * h + d, ci] = jnp.dot(kd_t, v, preferred_element_type=F32)
        return carry

    lax.fori_loop(0, nc, contributions, 0)

    for (h, d), s_ref in zip(chains, state_refs):
        s_ref[...] = s0_ref[h, d]

    def scan(ci, carry):
        for idx, (h, d) in enumerate(chains):
            cc = ci if d == 0 else nc - 1 - ci
            state = state_refs[idx][...]
            sin_ref[idx, cc] = state.astype(BF16)
            state_refs[idx][...] = tab_ref[h, d, 3] * state + u_ref[idx, cc]
        return carry

    lax.fori_loop(0, nc, scan, 0)
    for (h, d), s_ref in zip(chains, state_refs):
        st_ref[h, d] = s_ref[...]

    per_trip = min(4, nc)

    def outputs(ti, carry):
        pairs = [(ti * per_trip + j, h) for j in range(per_trip) for h in range(RET_HEADS)]
        rows = [pl.ds(pl.multiple_of(ci * c, c), c) for ci, _ in pairs]
        qs = [q_ref[r, head_slices(h)[0]] for r, (_, h) in zip(rows, pairs)]
        vs = [v_ref[r, head_slices(h)[1]] for r, (_, h) in zip(rows, pairs)]
        raw = [jnp.dot(q, (kt_ref[ci, head_slices(h)[0], :].astype(F32)
                           * tab_ref[h, 0, 4]).astype(BF16), preferred_element_type=F32)
               for q, (ci, h) in zip(qs, pairs)]
        cross = [[jnp.dot(q, sin_ref[2 * h + d, ci], preferred_element_type=F32)
                  for d in range(2)] for q, (ci, h) in zip(qs, pairs)]
        intra = [jnp.dot((s * tab_ref[h, 0, 0]).astype(BF16), v, preferred_element_type=F32)
                 for s, v, (_, h) in zip(raw, vs, pairs)]
        for r, (_, h), o_in, o_cr in zip(rows, pairs, intra, cross):
            hs = head_slices(h)[1]
            o = o_in + o_cr[0] * tab_ref[h, 0, 1] + o_cr[1] * tab_ref[h, 1, 1]
            mu = jnp.mean(o, axis=-1, keepdims=True)
            var = jnp.mean(jnp.square(o - mu), axis=-1, keepdims=True)
            on = (o - mu) * lax.rsqrt(var + EPS) * gnw_ref[:, hs] + gnb_ref[:, hs]
            gate = gate_ref[r, hs].astype(F32)
            o_ref[r, hs] = (_silu(gate) * on).astype(BF16)
        return carry

    lax.fori_loop(0, nc // per_trip, outputs, 0)


def _retention(z, l, n, dec, s0, gn_w, gn_b):
    r = z.shape[0]
    w = RET_WIDTH
    qk = RET_HEADS * RET_DK
    nc = n // RET_CHUNK
    col = lambda base, width: pl.BlockSpec((n, width), lambda b: (b, base // width))
    st_spec = pl.BlockSpec((None, RET_HEADS, 2, LANES, RET_DV), lambda b: (b, 0, 0, 0, 0))
    gn_spec = pl.BlockSpec((None, 1, w), lambda b: (l, 0, 0))
    return pl.pallas_call(
        functools.partial(_ret_kernel, n=n),
        grid=(BATCH,),
        in_specs=[col(Z_RQ, qk), col(Z_RK, qk), col(Z_RV, w), col(Z_RGATE, w),
                  pl.BlockSpec((None, 2, RET_HEADS, 8, LANES), lambda b: (l, 0, 0, 0, 0)),
                  st_spec, gn_spec, gn_spec],
        out_specs=[pl.BlockSpec((n, w), lambda b: (b, 0)), st_spec],
        out_shape=[jax.ShapeDtypeStruct((r, w), BF16),
                   jax.ShapeDtypeStruct((BATCH, RET_HEADS, 2, LANES, RET_DV), F32)],
        scratch_shapes=[pltpu.VMEM((RET_HEADS, 2, 5, RET_CHUNK, RET_CHUNK), F32),
                        pltpu.VMEM((nc, qk, RET_CHUNK), BF16),
                        pltpu.VMEM((2 * RET_HEADS, nc, LANES, RET_DV), F32),
                        pltpu.VMEM((2 * RET_HEADS, nc, LANES, RET_DV), BF16)]
        + [pltpu.VMEM((LANES, RET_DV), F32)] * (2 * RET_HEADS),
        compiler_params=_cparams("arbitrary"),
        name="retention",
    )(z, z, z, z, dec, s0, gn_w, gn_b)


def _hy_filter_kernel(feat_ref, w1_ref, b1_ref, w2_ref, b2_ref, w3_ref, dl_ref, s_ref, d_ref):
    feats = feat_ref[...]
    h = jnp.sin(HY_SIN_FREQ * (_dot_3pass(feats, w1_ref[...]) + b1_ref[...]))
    h = jnp.sin(HY_SIN_FREQ * (_dot_3pass(h, w2_ref[...]) + b2_ref[...]))
    h = _dot_3pass(h, w3_ref[...])
    window = jnp.exp(-feats[:, 0:1] * dl_ref[...]) + HY_WINDOW_SHIFT
    window = jnp.concatenate([window] * HY_ORDER, axis=-1)
    half = HY_ORDER * HY_WIDTH
    h_fwd = h[:, :half] * window
    h_bwd = h[:, half:] * window
    s_ref[...] = (h_fwd + h_bwd).astype(BF16)
    d_ref[...] = (h_fwd - h_bwd).astype(BF16)


def _hy_filter(feats, w1, b1, w2, b2, w3, deltas, l):
    n = feats.shape[0]
    tn = 256
    half = HY_ORDER * HY_WIDTH
    full = lambda a, b: pl.BlockSpec((None, a, b), lambda i: (l, 0, 0))
    return pl.pallas_call(
        _hy_filter_kernel,
        grid=(n // tn,),
        in_specs=[pl.BlockSpec((tn, LANES), lambda i: (i, 0)),
                  full(LANES, HY_FFN), full(1, HY_FFN), full(HY_FFN, HY_FFN), full(1, HY_FFN),
                  full(HY_FFN, 2 * half),
                  pl.BlockSpec((1, HY_WIDTH), lambda i: (0, 0))],
        out_specs=[pl.BlockSpec((tn, half), lambda i: (i, 0))] * 2,
        out_shape=[jax.ShapeDtypeStruct((n, half), BF16)] * 2,
        compiler_params=_cparams("arbitrary"),
        name="hyena_filter",
    )(feats, w1, b1, w2, b2, w3, deltas)


def _hy_spec_kernel(fc_ref, fs_ref, alt_ref, s_ref, d_ref, hc_ref, hs_ref, hn_ref):
    s = s_ref[...]
    tk, n = fc_ref.shape
    row = lax.broadcasted_iota(jnp.int32, (tk, s.shape[1]), 0) + pl.program_id(0) * tk
    wk = jnp.where(row == 0, 1.0, 2.0) * (1.0 / (2 * n))
    hc_ref[...] = jnp.dot(fc_ref[...], s, preferred_element_type=F32) * wk
    hs_ref[...] = jnp.dot(fs_ref[...], d_ref[...], preferred_element_type=F32) * wk
    hn_ref[...] = jnp.dot(alt_ref[...], s, preferred_element_type=F32)[0:8, :] * (1.0 / (2 * n))


def _hy_spectrum(fc, fs, alt, s, d):
    n = fc.shape[0]
    tk = min(512, n)
    half = s.shape[1]
    out = pl.BlockSpec((tk, half), lambda i: (i, 0))
    return pl.pallas_call(
        _hy_spec_kernel,
        grid=(n // tk,),
        in_specs=[pl.BlockSpec((tk, n), lambda i: (i, 0)),
                  pl.BlockSpec((tk, n), lambda i: (i, 0)),
                  pl.BlockSpec((16, n), lambda i: (0, 0)),
                  pl.BlockSpec((n, half), lambda i: (0, 0)),
                  pl.BlockSpec((n, half), lambda i: (0, 0))],
        out_specs=[out, out, pl.BlockSpec((8, half), lambda i: (0, 0))],
        out_shape=[jax.ShapeDtypeStruct((n, half), F32)] * 2
        + [jax.ShapeDtypeStruct((8, half), F32)],
        compiler_params=_cparams("arbitrary"),
        name="hyena_spectrum",
    )(fc, fs, alt, s, d)


def _short_conv_kernel(u_ref, w_ref, b_ref, o_ref, pad_ref, *, n):
    zeros = jnp.zeros((8, pad_ref.shape[1]), F32)
    pad_ref[0:8, :] = zeros
    pad_ref[8 + n:16 + n, :] = zeros
    pad_ref[8:8 + n, :] = u_ref[...].astype(F32)
    w = w_ref[...]
    y = (pad_ref[7:7 + n, :] * w[0:1, :] + pad_ref[8:8 + n, :] * w[1:2, :]
         + pad_ref[9:9 + n, :] * w[2:3, :] + b_ref[...])
    o_ref[...] = y.astype(BF16)


def _short_conv(z, l, n, w, b):
    r = z.shape[0]
    wc = HY_WIDTH
    return pl.pallas_call(
        functools.partial(_short_conv_kernel, n=n),
        grid=(BATCH, (HY_ORDER + 1)),
        in_specs=[pl.BlockSpec((n, wc), lambda bi, j: (bi, Z_HY // wc + j)),
                  pl.BlockSpec((None, 3, wc), lambda bi, j: (l, 0, j)),
                  pl.BlockSpec((None, 1, wc), lambda bi, j: (l, 0, j))],
        out_specs=pl.BlockSpec((n, wc), lambda bi, j: (bi, j)),
        out_shape=jax.ShapeDtypeStruct((r, (HY_ORDER + 1) * wc), BF16),
        scratch_shapes=[pltpu.VMEM((n + 16, wc), F32)],
        compiler_params=_cparams("arbitrary", "arbitrary"),
        name="short_conv",
    )(z, w, b)


HY_PAIR = 2


def _long_conv_kernel(u_ref, gate_ref, fc_ref, fs_ref, g_ref, alt_ref, hc_ref, hs_ref, hn_ref,
                      skip_ref, gain_ref, o_ref, y_ref, nyq_ref, *, n, tk, final):
    j = pl.program_id(1)
    kt = n // tk

    @pl.when(j == 0)
    def _():
        for p in range(HY_PAIR):
            x_nyq = jnp.dot(alt_ref[...], u_ref[p], preferred_element_type=F32)
            nyq_ref[p] = x_nyq[0:8, :] * hn_ref[...]

    @pl.when(j < kt)
    def _():
        r0 = pl.multiple_of(j * tk, tk)
        hc, hs = hc_ref[...], hs_ref[...]
        xs_all = [(jnp.dot(fc_ref[...], u_ref[p], preferred_element_type=F32),
                   jnp.dot(fs_ref[...], u_ref[p], preferred_element_type=F32))
                  for p in range(HY_PAIR)]
        for p, (xc, xs) in enumerate(xs_all):
            y_ref[p, pl.ds(r0, tk), :] = (xc * hc - xs * hs).astype(BF16)
            y_ref[p, pl.ds(n + r0, tk), :] = (xc * hs + xs * hc).astype(BF16)

    @pl.when(j >= kt)
    def _():
        t0 = pl.multiple_of((j - kt) * tk, tk)
        t = lax.broadcasted_iota(jnp.int32, (tk, u_ref.shape[-1]), 0)
        alt = (1 - 2 * (t & 1)).astype(F32)
        convs = [jnp.dot(g_ref[...], y_ref[p], preferred_element_type=F32)
                 for p in range(HY_PAIR)]
        for p, conv in enumerate(convs):
            u = u_ref[p, pl.ds(t0, tk), :].astype(F32)
            y = conv + alt * nyq_ref[p][0:1, :] + u * skip_ref[...]
            y = gate_ref[p].astype(F32) * y
            if final:
                y = _rms(y) * gain_ref[...]
            o_ref[p] = y.astype(BF16)


def _long_conv(src, u_col, gate_col, mats, spec, order, skip, gain, l, n, final):
    fc, fs, g, alt = mats
    hc, hs, hn = spec
    r = src[0].shape[0]
    wc = HY_WIDTH
    tk = min(512, n)
    kt = n // tk
    nb = BATCH // HY_PAIR
    u4, gate4 = (a.reshape(nb, HY_PAIR, n, a.shape[-1]) for a in src)
    f_spec = pl.BlockSpec((tk, n), lambda bi, j: (jnp.minimum(j, kt - 1), 0))
    h_spec = pl.BlockSpec((tk, wc), lambda bi, j: (jnp.minimum(j, kt - 1), order))
    tile = lambda col: pl.BlockSpec((None, HY_PAIR, tk, wc),
                                    lambda bi, j: (bi, 0, jnp.maximum(j - kt, 0), col))
    out = pl.pallas_call(
        functools.partial(_long_conv_kernel, n=n, tk=tk, final=final),
        grid=(nb, 2 * kt),
        in_specs=[pl.BlockSpec((None, HY_PAIR, n, wc), lambda bi, j: (bi, 0, 0, u_col)),
                  tile(gate_col),
                  f_spec, f_spec,
                  pl.BlockSpec((tk, 2 * n), lambda bi, j: (jnp.maximum(j - kt, 0), 0)),
                  pl.BlockSpec((16, n), lambda bi, j: (0, 0)),
                  h_spec, h_spec,
                  pl.BlockSpec((8, wc), lambda bi, j: (0, order)),
                  pl.BlockSpec((None, None, 1, wc), lambda bi, j: (l, order, 0, 0)),
                  pl.BlockSpec((None, 1, wc), lambda bi, j: (l, 0, 0))],
        out_specs=tile(0),
        out_shape=jax.ShapeDtypeStruct((nb, HY_PAIR, n, wc), BF16),
        scratch_shapes=[pltpu.VMEM((HY_PAIR, 2 * n, wc), BF16),
                        pltpu.VMEM((HY_PAIR, 8, wc), F32)],
        compiler_params=_cparams("arbitrary", "arbitrary"),
        name="long_conv",
    )(u4, gate4, fc, fs, g, alt, hc, hs, hn, skip, gain)
    return out.reshape(r, wc)


def _dft_mats(n):
    big = 2 * n
    fine = 32
    t = jnp.arange(n, dtype=jnp.int32)

    def trig(k):
        ang = ((k[:, None] * t[None, :]) % big).astype(F32) * (2.0 * math.pi / big)
        return jnp.cos(ang), jnp.sin(ang)

    ca, sa = (m[:, None, :] for m in trig(fine * jnp.arange(n // fine, dtype=jnp.int32)))
    cb, sb = (m[None, :, :] for m in trig(jnp.arange(fine, dtype=jnp.int32)))
    fc = (ca * cb - sa * sb).reshape(n, n).astype(BF16)
    fs = (sa * cb + ca * sb).reshape(n, n).astype(BF16)
    alt = jnp.broadcast_to((1 - 2 * (t % 2)).astype(F32)[None, :], (16, n))
    return fc, fs, jnp.concatenate([fc, fs], axis=1), alt.astype(BF16)


def _hy_features(n):
    t = jnp.linspace(0.0, 1.0, n, dtype=F32)[:, None]
    bands = (HY_EMB - 1) // 2
    w = 2 * math.pi * jnp.arange(n, dtype=F32)[:, None] / n
    f = jnp.linspace(1e-4, bands - 1, bands, dtype=F32)[None, :]
    feats = jnp.concatenate([t, jnp.cos(f * w), -jnp.sin(f * w)], axis=-1)
    return jnp.pad(feats, ((0, 0), (0, LANES - HY_EMB)))


def _rope_table():
    n_freq = MLA_ROPE // 4
    inv_freq = ROPE_BASE ** (-jnp.arange(n_freq, dtype=F32) / n_freq)
    rows = SEQ // GRID_W
    row = jnp.repeat(jnp.arange(rows), GRID_W).astype(F32)
    col = jnp.tile(jnp.arange(GRID_W), rows).astype(F32)
    ar = row[:, None] * inv_freq[None, :]
    ac = col[:, None] * inv_freq[None, :]
    cos_part = jnp.concatenate([jnp.cos(ar), jnp.cos(ar), jnp.cos(ac), jnp.cos(ac)], axis=-1)
    sin_part = jnp.concatenate([-jnp.sin(ar), jnp.sin(ar), -jnp.sin(ac), jnp.sin(ac)], axis=-1)
    return jnp.concatenate([cos_part, sin_part], axis=-1)


_ROPE_PARTNER = np.concatenate([np.arange(16, 32), np.arange(0, 16),
                                np.arange(48, 64), np.arange(32, 48)])
_QUARTER = MLA_ROPE // 4


def _partner_pieces(base):
    return [(base + int(_ROPE_PARTNER[i]), _QUARTER) for i in range(0, MLA_ROPE, _QUARTER)]


def _w_in_pieces():
    off = np.cumsum([0, MLA_KV_LORA, MLA_ROPE, RET_HEADS * RET_DK, RET_WIDTH, MLA_Q_LORA,
                     RET_HEADS * RET_DK, RET_WIDTH])
    o_kv, o_kr, o_rk, o_rv, o_q, o_rq, o_gate, o_hy = [int(v) for v in off]
    pieces = [(o_hy, 3 * HY_WIDTH), (o_rv, RET_WIDTH), (o_q, MLA_Q_LORA), (o_gate, RET_WIDTH),
              (o_rq, RET_HEADS * RET_DK), (o_rk, RET_HEADS * RET_DK),
              (o_kv, MLA_KV_LORA), (o_kr, MLA_ROPE)] + _partner_pieces(o_kr)
    pieces.append((None, Z_COLS - sum(w for _, w in pieces)))
    return pieces


def _wuq_pieces():
    per = MLA_NOPE + MLA_ROPE
    pieces = []
    for h in range(MLA_HEADS):
        pieces += [(h * per, per)] + _partner_pieces(h * per + MLA_NOPE)
    return pieces


def _relayout_kernel(w_ref, o_ref, *, pieces):
    col = 0
    for start, width in pieces:
        if start is None:
            o_ref[:, col:col + width] = jnp.zeros((o_ref.shape[0], width), o_ref.dtype)
        else:
            o_ref[:, col:col + width] = w_ref[:, start:start + width].astype(o_ref.dtype)
        col += width


def _relayout_cols(w, pieces):
    nl, k, n_in = w.shape
    n_out = sum(width for _, width in pieces)
    tr = 256
    return pl.pallas_call(
        functools.partial(_relayout_kernel, pieces=tuple(pieces)),
        grid=(nl, k // tr),
        in_specs=[pl.BlockSpec((None, tr, n_in), lambda l, i: (l, i, 0))],
        out_specs=pl.BlockSpec((None, tr, n_out), lambda l, i: (l, i, 0)),
        out_shape=jax.ShapeDtypeStruct((nl, k, n_out), BF16),
        compiler_params=_cparams("arbitrary", "arbitrary"),
        name="relayout_cols",
    )(w)


def _relayout_rows_kernel(w_ref, o_ref, *, pieces):
    row = 0
    for start, width in pieces:
        if start is None:
            o_ref[row:row + width, :] = jnp.zeros((width, o_ref.shape[1]), o_ref.dtype)
        else:
            o_ref[row:row + width, :] = w_ref[start:start + width, :].astype(o_ref.dtype)
        row += width


def _relayout_rows(w_t, pieces):
    nl, n_in, k = w_t.shape
    n_out = sum(width for _, width in pieces)
    tc = 512
    return pl.pallas_call(
        functools.partial(_relayout_rows_kernel, pieces=tuple(pieces)),
        grid=(nl, k // tc),
        in_specs=[pl.BlockSpec((None, n_in, tc), lambda l, i: (l, 0, i))],
        out_specs=pl.BlockSpec((None, n_out, tc), lambda l, i: (l, 0, i)),
        out_shape=jax.ShapeDtypeStruct((nl, n_out, k), BF16),
        compiler_params=_cparams("arbitrary", "arbitrary"),
        name="relayout_rows",
    )(w_t)


def kernel(x, c, ctx, c_ctx, ada_w, ada_b, ffn1_gate, ffn1_up, ffn1_down, w_in, mla_q_norm, mla_wuq, mla_kv_norm, mla_wukv, mla_qn_nope, mla_qn_rope, mla_kn_nope, mla_kn_rope, mla_out_norm, hy_conv_w, hy_conv_b, hy_ffn_w1, hy_ffn_b1, hy_ffn_w2, hy_ffn_b2, hy_ffn_w3, hy_skip, hy_out_norm, ret_decay, ret_gn_w, ret_gn_b, w_out, ffn2_gate, ffn2_up, ffn2_down):
    nl = DEPTH
    d = D_MODEL

    w1g, w1u, w1d = ffn1_gate, ffn1_up, ffn1_down
    w2g, w2u, w2d = ffn2_gate, ffn2_up, ffn2_down
    w_in_p = _relayout_rows(jnp.swapaxes(w_in, 1, 2), _w_in_pieces())
    wuq_p = _relayout_cols(mla_wuq, _wuq_pieces())
    wukv = mla_wukv.astype(BF16)
    w_out_b = w_out.astype(BF16)
    row3 = lambda g: g.reshape(nl, 1, -1)
    partner = jnp.asarray(_ROPE_PARTNER)
    g_qr = row3(jnp.concatenate([mla_qn_rope, mla_qn_rope[:, partner]], axis=-1))
    g_kr = row3(jnp.concatenate([mla_kn_rope, mla_kn_rope[:, partner]], axis=-1))
    g_q, g_kv, g_qn, g_kn = row3(mla_q_norm), row3(mla_kv_norm), row3(mla_qn_nope), row3(mla_kn_nope)
    g_attn, g_hy = row3(mla_out_norm), row3(hy_out_norm)
    gn_w, gn_b = row3(ret_gn_w), row3(ret_gn_b)
    hy_b = row3(hy_conv_b)
    hy_w1 = jnp.pad(hy_ffn_w1, ((0, 0), (0, LANES - HY_EMB), (0, 0)))
    hy_b1, hy_b2 = row3(hy_ffn_b1), row3(hy_ffn_b2)
    hy_skip4 = hy_skip.reshape(nl, HY_ORDER, 1, HY_WIDTH)
    dec = jnp.broadcast_to(ret_decay.astype(F32)[:, :, :, None, None], (nl, 2, RET_HEADS, 8, LANES))

    tab_x = _rope_table()
    tab_c = jnp.concatenate([jnp.ones((CTX_LEN, MLA_ROPE), F32), jnp.zeros((CTX_LEN, MLA_ROPE), F32)], -1)
    deltas = jnp.abs(jnp.linspace(HY_MIN_DECAY, HY_MAX_DECAY, HY_WIDTH, dtype=F32))[None, :]
    mats = {n: _dft_mats(n) for n in (SEQ, CTX_LEN)}
    feats = {n: _hy_features(n) for n in (SEQ, CTX_LEN)}

    cond = jnp.zeros((16, d), F32).at[0].set(c_ctx).at[1:1 + BATCH].set(c)
    mods = _modulation(cond, ada_w, ada_b).reshape(nl, 16, N_MOD, d)

    sx = x.reshape(BATCH * SEQ, d)
    sc = ctx.reshape(BATCH * CTX_LEN, d)

    def mod_tables(l):
        return ([mods[l, 1:1 + BATCH, j][:, None, :] for j in range(N_MOD)],
                [mods[l, 0:1, j][:, None, :] for j in range(N_MOD)])

    all_mods = [mod_tables(l) for l in range(nl)]

    def ffn(s, h, n, gate, wg, wu, wd, l, wd_bf=None):
        u, rounded = _ffn_up(h, wg, wu, l, None if wd_bf is not None else wd)
        wd_bf = rounded if wd_bf is None else wd_bf
        return _ffn_down(u, wd_bf, s, gate, n), wd_bf

    def hyena(z, l, n, spec):
        cv = _short_conv(z, l, n, hy_conv_w, hy_b)
        y1 = _long_conv((cv, cv), 2, 0, mats[n], spec, 0, hy_skip4, g_hy, l, n, False)
        return _long_conv((y1, cv), 0, 1, mats[n], spec, 1, hy_skip4, g_hy, l, n, True)

    for l in range(nl):
        need_ctx_out = l < nl - 1
        mod_x, mod_c = all_mods[l]

        sx, wd_bf = ffn(sx, _normmod(sx, mod_x[0], mod_x[1], SEQ), SEQ, mod_x[2], w1g, w1u, w1d, l)
        sc, _ = ffn(sc, _normmod(sc, mod_c[0], mod_c[1], CTX_LEN), CTX_LEN, mod_c[2], w1g, w1u, w1d,
                    l, wd_bf)

        zc = _norm_matmul(sc, mod_c[3], mod_c[4], CTX_LEN, w_in_p, l)
        zx = _norm_matmul(sx, mod_x[3], mod_x[4], SEQ, w_in_p, l)

        kc, vc = _kv_proj(zc, l, CTX_LEN, wukv, g_kv, g_kn, g_kr, tab_c)
        kx, vx = _kv_proj(zx, l, SEQ, wukv, g_kv, g_kn, g_kr, tab_x)
        zero_state = jnp.zeros((BATCH, RET_HEADS, 2, LANES, RET_DV), F32)
        ret_c, state_c = _retention(zc, l, CTX_LEN, dec, zero_state, gn_w, gn_b)

        filt = {}
        for n in ((SEQ, CTX_LEN) if need_ctx_out else (SEQ,)):
            s_f, d_f = _hy_filter(feats[n], hy_w1, hy_b1, hy_ffn_w2, hy_b2, hy_ffn_w3, deltas, l)
            fc, fs, _, alt = mats[n]
            filt[n] = _hy_spectrum(fc, fs, alt, s_f, d_f)

        qx = _q_proj(zx, l, SEQ, wuq_p, g_q, g_qn, g_qr, tab_x)
        attn_x = _attention(qx, [kc, kx], [vc, vx], [CTX_LEN, SEQ], l, g_attn)
        hy_x = hyena(zx, l, SEQ, filt[SEQ])
        ret_x, _ = _retention(zx, l, SEQ, dec, state_c, gn_w, gn_b)
        sx, hx = _w_out([attn_x, hy_x, ret_x], w_out_b, l, sx, mod_x[5], SEQ, (mod_x[6], mod_x[7]))
        sx, wd_bf = ffn(sx, hx, SEQ, mod_x[8], w2g, w2u, w2d, l)

        if need_ctx_out:
            qc = _q_proj(zc, l, CTX_LEN, wuq_p, g_q, g_qn, g_qr, tab_c)
            attn_c = _attention(qc, [kc], [vc], [CTX_LEN], l, g_attn)
            hy_c = hyena(zc, l, CTX_LEN, filt[CTX_LEN])
            sc, hc = _w_out([attn_c, hy_c, ret_c], w_out_b, l, sc, mod_c[5], CTX_LEN,
                            (mod_c[6], mod_c[7]))
            sc, _ = ffn(sc, hc, CTX_LEN, mod_c[8], w2g, w2u, w2d, l, wd_bf)

    return sx.reshape(BATCH, SEQ, d)
```

```python
import functools
import math

import jax
import jax.numpy as jnp
import numpy as np
from jax import lax
from jax.experimental import pallas as pl
from jax.experimental.pallas import tpu as pltpu

F32 = jnp.float32
BF16 = jnp.bfloat16

D_MODEL = 2048
BATCH = 8
SEQ = 2048
DEPTH = 2
GRID_W = 64
CTX_LEN = 256
EPS = 1e-6
N_MOD = 9
FFN_HIDDEN = 5632

MLA_HEADS = 8
MLA_NOPE = 128
MLA_ROPE = 64
MLA_V = 128
MLA_Q_LORA = 512
MLA_KV_LORA = 256
MLA_WIDTH = MLA_HEADS * MLA_V
ROPE_BASE = 10000.0

HY_WIDTH = 512
HY_ORDER = 2
HY_EMB = 33
HY_FFN = 64
HY_SIN_FREQ = 1.0
HY_MIN_DECAY = math.log(1e-2) / 1.5
HY_MAX_DECAY = math.log(1e-2) / 0.3
HY_WINDOW_SHIFT = 0.05

RET_HEADS = 4
RET_DK = 64
RET_DV = 128
RET_WIDTH = RET_HEADS * RET_DV
RET_CHUNK = 128

VMEM_LIMIT_BYTES = 60 * 1024 * 1024
LANES = 128
HEAD_SLOT = 256

Z_HY = 0
Z_RV = 1536
Z_QLAT = 2048
Z_RGATE = 2560
Z_RQ = 3072
Z_RK = 3328
Z_KVLAT = 3584
Z_KR = 3840
Z_COLS = 4096

TM = 1024
TM_WIDE = 2048
TN = 512


def _cparams(*sem):
    return pltpu.CompilerParams(dimension_semantics=sem, vmem_limit_bytes=VMEM_LIMIT_BYTES)


def _rms(x, width=None):
    width = x.shape[-1] if width is None else width
    return x * lax.rsqrt(jnp.sum(x * x, axis=-1, keepdims=True) / width + EPS)


def _silu(x):
    return x * jax.nn.sigmoid(x)


def _dot_3pass(a, w):
    a_hi, w_hi = a.astype(BF16), w.astype(BF16)
    a_lo = (a - a_hi.astype(F32)).astype(BF16)
    w_lo = (w - w_hi.astype(F32)).astype(BF16)
    rows = a.shape[0]
    acc = jnp.dot(jnp.concatenate([a_hi, a_lo], axis=0), w_hi, preferred_element_type=F32)
    return acc[:rows] + acc[rows:] + jnp.dot(a_hi, w_lo, preferred_element_type=F32)


def _mod_kernel(c_ref, w_ref, b_ref, o_ref):
    o_ref[...] = _dot_3pass(_silu(c_ref[...]), w_ref[...]) + b_ref[...]


def _modulation(cond, ada_w, ada_b):
    nl, d, nout = ada_w.shape
    bn = 1024
    return pl.pallas_call(
        _mod_kernel,
        grid=(nl, nout // bn),
        in_specs=[pl.BlockSpec((16, d), lambda l, j: (0, 0)),
                  pl.BlockSpec((None, d, bn), lambda l, j: (l, 0, j)),
                  pl.BlockSpec((None, 1, bn), lambda l, j: (l, 0, j))],
        out_specs=pl.BlockSpec((None, 16, bn), lambda l, j: (l, 0, j)),
        out_shape=jax.ShapeDtypeStruct((nl, 16, nout), F32),
        compiler_params=_cparams("arbitrary", "arbitrary"),
        name="modulation",
    )(cond, ada_w, ada_b.reshape(nl, 1, nout))


def _normmod_kernel(x_ref, sh_ref, sc_ref, o_ref):
    y = _rms(x_ref[...])
    o_ref[...] = (y * (1.0 + sc_ref[...]) + sh_ref[...]).astype(BF16)


def _normmod(s, shift, scale, n):
    r, d = s.shape
    tm = min(512, n)
    per_batch = shift.shape[0] > 1
    mod_spec = pl.BlockSpec((None, 1, d), lambda i: ((i * tm) // n if per_batch else 0, 0, 0))
    return pl.pallas_call(
        _normmod_kernel,
        grid=(r // tm,),
        in_specs=[pl.BlockSpec((tm, d), lambda i: (i, 0)), mod_spec, mod_spec],
        out_specs=pl.BlockSpec((tm, d), lambda i: (i, 0)),
        out_shape=jax.ShapeDtypeStruct((r, d), BF16),
        compiler_params=_cparams("arbitrary"),
        name="normmod",
    )(s, shift, scale)


def _ffn_up_kernel(h_ref, wg_ref, wu_ref, *refs, round_wd):
    if round_wd:
        wd_ref, o_ref, wd_bf, wg_bf, wu_bf = refs
        wd_bf[...] = wd_ref[...].astype(BF16)
    else:
        o_ref, wg_bf, wu_bf = refs

    @pl.when(pl.program_id(1) == 0)
    def _():
        wg_bf[...] = wg_ref[...].astype(BF16)
        wu_bf[...] = wu_ref[...].astype(BF16)

    h = h_ref[...]
    g = jnp.dot(h, wg_bf[...], preferred_element_type=F32)
    u = jnp.dot(h, wu_bf[...], preferred_element_type=F32)
    o_ref[...] = (_silu(g) * u).astype(BF16)


def _ffn_up(h, wg, wu, l, wd=None):
    r, d = h.shape
    f = wg.shape[-1]
    w_spec = pl.BlockSpec((None, d, TN), lambda j, i: (l, 0, j))
    tm = TM
    round_wd = wd is not None
    u_spec = pl.BlockSpec((tm, TN), lambda j, i: (i, j))
    u_shape = jax.ShapeDtypeStruct((r, f), BF16)
    steps = r // tm
    rows = f // (f // TN * steps)
    slab = pl.BlockSpec((rows, d), lambda j, i: (j * steps + i, 0))
    outs = pl.pallas_call(
        functools.partial(_ffn_up_kernel, round_wd=round_wd),
        grid=(f // TN, steps),
        in_specs=[pl.BlockSpec((tm, d), lambda j, i: (i, 0)), w_spec, w_spec]
        + ([pl.BlockSpec((None, rows, d), lambda j, i: (l, j * steps + i, 0))] if round_wd else []),
        out_specs=[u_spec, slab] if round_wd else u_spec,
        out_shape=[u_shape, jax.ShapeDtypeStruct((f, d), BF16)] if round_wd else u_shape,
        scratch_shapes=[pltpu.VMEM((d, TN), BF16), pltpu.VMEM((d, TN), BF16)],
        compiler_params=_cparams("arbitrary", "arbitrary"),
        name="ffn_up",
    )(h, wg, wu, *([wd] if round_wd else []))
    return outs if round_wd else (outs, None)


def _mod_spec(table, tm, n, d):
    per_batch = table.shape[0] > 1
    return pl.BlockSpec((None, 1, d), lambda i, *_: ((i * tm) // n if per_batch else 0, 0, 0))


def _norm_mm_kernel(s_ref, sh_ref, sc_ref, w_ref, o_ref, h_ref):
    @pl.when(pl.program_id(1) == 0)
    def _():
        h_ref[...] = (_rms(s_ref[...]) * (1.0 + sc_ref[...]) + sh_ref[...]).astype(BF16)

    o_ref[...] = lax.dot_general(h_ref[...], w_ref[...], (((1,), (1,)), ((), ())),
                                 preferred_element_type=F32).astype(o_ref.dtype)


def _norm_matmul(s, shift, scale, n, w_t, l):
    r, d = s.shape
    nout = w_t.shape[1]
    tm = TM_WIDE
    return pl.pallas_call(
        _norm_mm_kernel,
        grid=(r // tm, nout // TN),
        in_specs=[pl.BlockSpec((tm, d), lambda i, j: (i, 0)),
                  _mod_spec(shift, tm, n, d), _mod_spec(scale, tm, n, d),
                  pl.BlockSpec((None, TN, d), lambda i, j: (l, j, 0))],
        out_specs=pl.BlockSpec((tm, TN), lambda i, j: (i, j)),
        out_shape=jax.ShapeDtypeStruct((r, nout), BF16),
        scratch_shapes=[pltpu.VMEM((tm, d), BF16)],
        compiler_params=_cparams("arbitrary", "arbitrary"),
        name="norm_matmul",
    )(s, shift, scale, w_t)


def _ffn_down_kernel(u_ref, w_ref, s_ref, g_ref, o_ref):
    acc = jnp.dot(u_ref[...], w_ref[...], preferred_element_type=F32)
    o_ref[...] = s_ref[...] + (0.5 * g_ref[...]) * acc


def _ffn_down(u, w, s, gate, n):
    r, d = s.shape
    f = u.shape[1]
    per_batch = gate.shape[0] > 1
    return pl.pallas_call(
        _ffn_down_kernel,
        grid=(r // TM, d // TN),
        in_specs=[pl.BlockSpec((TM, f), lambda i, j: (i, 0)),
                  pl.BlockSpec((f, TN), lambda i, j: (0, j)),
                  pl.BlockSpec((TM, TN), lambda i, j: (i, j)),
                  pl.BlockSpec((None, 1, TN), lambda i, j: ((i * TM) // n if per_batch else 0, 0, j))],
        out_specs=pl.BlockSpec((TM, TN), lambda i, j: (i, j)),
        out_shape=jax.ShapeDtypeStruct((r, d), F32),
        compiler_params=_cparams("arbitrary", "arbitrary"),
        name="ffn_down",
    )(u, w, s, gate)


def _w_out_kernel(*refs, n_in):
    a_refs = refs[:n_in]
    w_ref, s_ref, g_ref, sh_ref, sc_ref, o_ref, h_ref = refs[n_in:]
    half = s_ref.shape[0] // 2
    accs = []
    for r0 in (0, half):
        acc, row = None, 0
        for a_ref in a_refs:
            kp = a_ref.shape[1]
            part = jnp.dot(a_ref[r0:r0 + half, :], w_ref[row:row + kp, :],
                           preferred_element_type=F32)
            acc = part if acc is None else acc + part
            row += kp
        accs.append(acc)
    for r0, acc in zip((0, half), accs):
        rows = slice(r0, r0 + half)
        s_new = s_ref[rows, :] + g_ref[...] * acc
        o_ref[rows, :] = s_new
        h_ref[rows, :] = (_rms(s_new) * (1.0 + sc_ref[...]) + sh_ref[...]).astype(BF16)


def _w_out(parts, w, l, s, gate, n, next_mod):
    r, d = s.shape
    tm = TM // 2
    shift, scale = next_mod
    row = pl.BlockSpec((tm, d), lambda i: (i, 0))
    return pl.pallas_call(
        functools.partial(_w_out_kernel, n_in=len(parts)),
        grid=(r // tm,),
        in_specs=[pl.BlockSpec((tm, p.shape[1]), lambda i: (i, 0)) for p in parts]
        + [pl.BlockSpec((None, d, d), lambda i: (l, 0, 0)),
           row, _mod_spec(gate, tm, n, d), _mod_spec(shift, tm, n, d), _mod_spec(scale, tm, n, d)],
        out_specs=[row, row],
        out_shape=[jax.ShapeDtypeStruct((r, d), F32), jax.ShapeDtypeStruct((r, d), BF16)],
        compiler_params=_cparams("arbitrary"),
        name="w_out",
    )(*parts, w, s, gate, shift, scale)


def _rope_lanes(x):
    lane = lax.broadcasted_iota(jnp.int32, x.shape, 1)
    ms = jnp.sum(jnp.where(lane < MLA_ROPE, x * x, 0.0), axis=-1, keepdims=True) / MLA_ROPE
    return x * lax.rsqrt(ms + EPS)


def _kv_kernel(lat_ref, kr_ref, gkv_ref, w_ref, gkn_ref, gkr_ref, tab_ref, k_ref, v_ref):
    a = _rms(lat_ref[...].astype(F32)) * gkv_ref[...]
    kv = jnp.dot(a.astype(BF16), w_ref[...], preferred_element_type=F32)
    t = _rope_lanes(kr_ref[...].astype(F32)) * gkr_ref[...] * tab_ref[...]
    krot = (t + pltpu.roll(t, MLA_ROPE, 1)).astype(BF16)
    for h in range(MLA_HEADS):
        c0 = h * HEAD_SLOT
        kn = _rms(kv[:, c0:c0 + MLA_NOPE]) * gkn_ref[...]
        k_ref[:, c0:c0 + MLA_NOPE] = kn.astype(BF16)
        k_ref[:, c0 + MLA_NOPE:c0 + HEAD_SLOT] = krot
        v_ref[:, h * MLA_V:(h + 1) * MLA_V] = kv[:, c0 + MLA_NOPE:c0 + HEAD_SLOT].astype(BF16)


def _kv_proj(z, l, n, wukv, g_kv, g_kn, g_kr, tab):
    r = z.shape[0]
    tm = min(512, n)
    nt = tab.shape[0] // tm
    vec = lambda w: pl.BlockSpec((None, 1, w), lambda i: (l, 0, 0))
    return pl.pallas_call(
        _kv_kernel,
        grid=(r // tm,),
        in_specs=[pl.BlockSpec((tm, MLA_KV_LORA), lambda i: (i, Z_KVLAT // MLA_KV_LORA)),
                  pl.BlockSpec((tm, LANES), lambda i: (i, Z_KR // LANES)),
                  vec(MLA_KV_LORA),
                  pl.BlockSpec((None, MLA_KV_LORA, MLA_HEADS * HEAD_SLOT), lambda i: (l, 0, 0)),
                  vec(MLA_NOPE), vec(LANES),
                  pl.BlockSpec((tm, LANES), lambda i: (i % nt, 0))],
        out_specs=[pl.BlockSpec((tm, MLA_HEADS * HEAD_SLOT), lambda i: (i, 0)),
                   pl.BlockSpec((tm, MLA_WIDTH), lambda i: (i, 0))],
        out_shape=[jax.ShapeDtypeStruct((r, MLA_HEADS * HEAD_SLOT), BF16),
                   jax.ShapeDtypeStruct((r, MLA_WIDTH), BF16)],
        compiler_params=_cparams("arbitrary"),
        name="kv_proj",
    )(z, z, g_kv, wukv, g_kn, g_kr, tab)


def _q_kernel(lat_ref, gq_ref, w_ref, gn_ref, gr_ref, tab_ref, q_ref, *, scale):
    a = _rms(lat_ref[...].astype(F32)) * gq_ref[...]
    q = jnp.dot(a.astype(BF16), w_ref[...], preferred_element_type=F32)
    rope_scale = gr_ref[...] * tab_ref[...] * scale
    for h in range(MLA_HEADS):
        c0 = h * HEAD_SLOT
        qn = _rms(q[:, c0:c0 + MLA_NOPE]) * (gn_ref[...] * scale)
        q_ref[:, c0:c0 + MLA_NOPE] = qn.astype(BF16)
        qr = _rope_lanes(q[:, c0 + MLA_NOPE:c0 + HEAD_SLOT]) * rope_scale
        q_ref[:, c0 + MLA_NOPE:c0 + HEAD_SLOT] = qr.astype(BF16)


def _q_proj(z, l, n, wuq, g_q, g_qn, g_qr, tab):
    r = z.shape[0]
    tm = min(512, n)
    nt = tab.shape[0] // tm
    vec = lambda w: pl.BlockSpec((None, 1, w), lambda i: (l, 0, 0))
    return pl.pallas_call(
        functools.partial(_q_kernel, scale=(MLA_NOPE + MLA_ROPE) ** -0.5 * math.log2(math.e)),
        grid=(r // tm,),
        in_specs=[pl.BlockSpec((tm, MLA_Q_LORA), lambda i: (i, Z_QLAT // MLA_Q_LORA)),
                  vec(MLA_Q_LORA),
                  pl.BlockSpec((None, MLA_Q_LORA, MLA_HEADS * HEAD_SLOT), lambda i: (l, 0, 0)),
                  vec(MLA_NOPE), vec(LANES),
                  pl.BlockSpec((tm, LANES), lambda i: (i % nt, 0))],
        out_specs=pl.BlockSpec((tm, MLA_HEADS * HEAD_SLOT), lambda i: (i, 0)),
        out_shape=jax.ShapeDtypeStruct((r, MLA_HEADS * HEAD_SLOT), BF16),
        compiler_params=_cparams("arbitrary"),
        name="q_proj",
    )(z, g_q, wuq, g_qn, g_qr, tab)


def _attn_kernel(*refs, n_src):
    q_ref = refs[0]
    k_refs, v_refs = refs[1:1 + n_src], refs[1 + n_src:1 + 2 * n_src]
    g_ref, o_ref, acc_ref = refs[1 + 2 * n_src:]

    def scores(h):
        hs = slice(h * HEAD_SLOT, (h + 1) * HEAD_SLOT)
        return [lax.dot_general(q_ref[:, hs], k_ref[:, hs], (((1,), (1,)), ((), ())),
                                preferred_element_type=F32) for k_ref in k_refs]

    ssq = None
    ss_next = scores(0)
    for h in range(MLA_HEADS):
        hs = slice(h * HEAD_SLOT, (h + 1) * HEAD_SLOT)
        ss, ss_next = ss_next, (scores(h + 1) if h + 1 < MLA_HEADS else None)
        m = functools.reduce(jnp.maximum, [jnp.max(s, axis=-1, keepdims=True) for s in ss])
        vs = [jnp.concatenate([v_ref[:, h * MLA_V:(h + 1) * MLA_V],
                               jnp.ones((v_ref.shape[0], HEAD_SLOT - MLA_V), BF16)], axis=1)
              for v_ref in v_refs]
        o = functools.reduce(jnp.add, [
            jnp.dot(jnp.exp2((s - m).astype(BF16)), v, preferred_element_type=F32)
            for s, v in zip(ss, vs)])
        o = o[:, :MLA_V] / o[:, MLA_V:MLA_V + 1]
        acc_ref[:, h * MLA_V:(h + 1) * MLA_V] = o
        sq = jnp.sum(o * o, axis=-1, keepdims=True)
        ssq = sq if ssq is None else ssq + sq
    o_ref[...] = (acc_ref[...] * lax.rsqrt(ssq / MLA_WIDTH + EPS) * g_ref[...]).astype(BF16)


def _attention(q, ks, vs, ns, l, g_out):
    r = q.shape[0]
    nq = r // BATCH
    tq = min(512, nq)
    per_b = nq // tq
    k_specs = [pl.BlockSpec((nk, MLA_HEADS * HEAD_SLOT), lambda b, i: (b, 0)) for nk in ns]
    v_specs = [pl.BlockSpec((nk, MLA_WIDTH), lambda b, i: (b, 0)) for nk in ns]
    return pl.pallas_call(
        functools.partial(_attn_kernel, n_src=len(ks)),
        grid=(BATCH, per_b),
        in_specs=[pl.BlockSpec((tq, MLA_HEADS * HEAD_SLOT), lambda b, i: (b * per_b + i, 0))]
        + k_specs + v_specs
        + [pl.BlockSpec((None, 1, MLA_WIDTH), lambda b, i: (l, 0, 0))],
        out_specs=pl.BlockSpec((tq, MLA_WIDTH), lambda b, i: (b * per_b + i, 0)),
        out_shape=jax.ShapeDtypeStruct((r, MLA_WIDTH), BF16),
        scratch_shapes=[pltpu.VMEM((tq, MLA_WIDTH), F32)],
        compiler_params=_cparams("arbitrary", "arbitrary"),
        name="attention",
    )(q, *ks, *vs, g_out)


def _log_sigmoid(x):
    return jnp.minimum(x, 0.0) - jnp.log(1.0 + jnp.exp(-jnp.abs(x)))


def _ret_kernel(q_ref, k_ref, v_ref, gate_ref, dec_ref, s0_ref, gnw_ref, gnb_ref,
                o_ref, st_ref, tab_ref, kt_ref, u_ref, sin_ref, *state_refs, n):
    nc = n // RET_CHUNK
    c = RET_CHUNK

    @pl.when(pl.program_id(0) == 0)
    def _():
        ii = lax.broadcasted_iota(jnp.int32, (c, c), 0).astype(F32)
        jj = lax.broadcasted_iota(jnp.int32, (c, c), 1).astype(F32)
        for h in range(RET_HEADS):
            for d in range(2):
                lg = _log_sigmoid(dec_ref[d, h][0:1, :])
                lgm = jnp.broadcast_to(lg, (c, c))
                if d == 0:
                    diff, qpow, kpow = ii - jj, ii + 1.0, (c - 1.0) - ii
                else:
                    diff, qpow, kpow = jj - ii, c - ii, ii
                intra = jnp.where(diff >= 0, jnp.exp(lgm * jnp.maximum(diff, 0.0)), 0.0)
                if d == 0:
                    tab_ref[h, 0, 0] = intra
                else:
                    tab_ref[h, 0, 0] += intra
                tab_ref[h, d, 1] = jnp.exp(lgm * qpow)
                kpow_t = (c - 1.0) - jj if d == 0 else jj
                k_scale = jnp.where(ii.astype(jnp.int32) // RET_DK == h % 2, RET_DK ** -0.5, 0.0)
                tab_ref[h, d, 2] = jnp.exp(lgm * kpow_t) * k_scale
                tab_ref[h, d, 3] = jnp.exp(lgm * float(c))
                tab_ref[h, d, 4] = k_scale

    def head_slices(h):
        return slice((h // 2) * LANES, (h // 2 + 1) * LANES), slice(h * RET_DV, (h + 1) * RET_DV)

    chains = [(h, d) for h in range(RET_HEADS) for d in range(2)]

    def contributions(ci, carry):
        r0 = pl.multiple_of(ci * c, c)
        kt_all = k_ref[pl.ds(r0, c), :].astype(F32).T
        kt_ref[ci] = kt_all.astype(BF16)
        for h in range(RET_HEADS):
            grp, hs = head_slices(h)
            v = v_ref[pl.ds(r0, c), hs]
            for d in range(2):
                kd_t = (kt_all[grp, :] * tab_ref[h, d, 2]).astype(BF16)
                u_ref[2 * h + d, ci] = jnp.dot(kd_t, v, preferred_element_type=F32)
        return carry

    lax.fori_loop(0, nc, contributions, 0)

    for (h, d), s_ref in zip(chains, state_refs):
        s_ref[...] = s0_ref[h, d]

    def scan(ci, carry):
        for idx, (h, d) in enumerate(chains):
            cc = ci if d == 0 else nc - 1 - ci
            state = state_refs[idx][...]
            sin_ref[idx, cc] = state.astype(BF16)
            state_refs[idx][...] = tab_ref[h, d, 3] * state + u_ref[idx, cc]
        return carry

    lax.fori_loop(0, nc, scan, 0)
    for (h, d), s_ref in zip(chains, state_refs):
        st_ref[h, d] = s_ref[...]

    per_trip = min(4, nc)

    def outputs(ti, carry):
        pairs = [(ti * per_trip + j, h) for j in range(per_trip) for h in range(RET_HEADS)]
        rows = [pl.ds(pl.multiple_of(ci * c, c), c) for ci, _ in pairs]
        qs = [q_ref[r, head_slices(h)[0]] for r, (_, h) in zip(rows, pairs)]
        vs = [v_ref[r, head_slices(h)[1]] for r, (_, h) in zip(rows, pairs)]
        raw = [jnp.dot(q, (kt_ref[ci, head_slices(h)[0], :].astype(F32)
                           * tab_ref[h, 0, 4]).astype(BF16), preferred_element_type=F32)
               for q, (ci, h) in zip(qs, pairs)]
        cross = [[jnp.dot(q, sin_ref[2 * h + d, ci], preferred_element_type=F32)
                  for d in range(2)] for q, (ci, h) in zip(qs, pairs)]
        intra = [jnp.dot((s * tab_ref[h, 0, 0]).astype(BF16), v, preferred_element_type=F32)
                 for s, v, (_, h) in zip(raw, vs, pairs)]
        for r, (_, h), o_in, o_cr in zip(rows, pairs, intra, cross):
            hs = head_slices(h)[1]
            o = o_in + o_cr[0] * tab_ref[h, 0, 1] + o_cr[1] * tab_ref[h, 1, 1]
            mu = jnp.mean(o, axis=-1, keepdims=True)
            var = jnp.mean(jnp.square(o - mu), axis=-1, keepdims=True)
            on = (o - mu) * lax.rsqrt(var + EPS) * gnw_ref[:, hs] + gnb_ref[:, hs]
            gate = gate_ref[r, hs].astype(F32)
            o_ref[r, hs] = (_silu(gate) * on).astype(BF16)
        return carry

    lax.fori_loop(0, nc // per_trip, outputs, 0)


def _retention(z, l, n, dec, s0, gn_w, gn_b):
    r = z.shape[0]
    w = RET_WIDTH
    qk = RET_HEADS * RET_DK
    nc = n // RET_CHUNK
    col = lambda base, width: pl.BlockSpec((n, width), lambda b: (b, base // width))
    st_spec = pl.BlockSpec((None, RET_HEADS, 2, LANES, RET_DV), lambda b: (b, 0, 0, 0, 0))
    gn_spec = pl.BlockSpec((None, 1, w), lambda b: (l, 0, 0))
    return pl.pallas_call(
        functools.partial(_ret_kernel, n=n),
        grid=(BATCH,),
        in_specs=[col(Z_RQ, qk), col(Z_RK, qk), col(Z_RV, w), col(Z_RGATE, w),
                  pl.BlockSpec((None, 2, RET_HEADS, 8, LANES), lambda b: (l, 0, 0, 0, 0)),
                  st_spec, gn_spec, gn_spec],
        out_specs=[pl.BlockSpec((n, w), lambda b: (b, 0)), st_spec],
        out_shape=[jax.ShapeDtypeStruct((r, w), BF16),
                   jax.ShapeDtypeStruct((BATCH, RET_HEADS, 2, LANES, RET_DV), F32)],
        scratch_shapes=[pltpu.VMEM((RET_HEADS, 2, 5, RET_CHUNK, RET_CHUNK), F32),
                        pltpu.VMEM((nc, qk, RET_CHUNK), BF16),
                        pltpu.VMEM((2 * RET_HEADS, nc, LANES, RET_DV), F32),
                        pltpu.VMEM((2 * RET_HEADS, nc, LANES, RET_DV), BF16)]
        + [pltpu.VMEM((LANES, RET_DV), F32)] * (2 * RET_HEADS),
        compiler_params=_cparams("arbitrary"),
        name="retention",
    )(z, z, z, z, dec, s0, gn_w, gn_b)


def _hy_filter_kernel(feat_ref, w1_ref, b1_ref, w2_ref, b2_ref, w3_ref, dl_ref, s_ref, d_ref):
    feats = feat_ref[...]
    h = jnp.sin(HY_SIN_FREQ * (_dot_3pass(feats, w1_ref[...]) + b1_ref[...]))
    h = jnp.sin(HY_SIN_FREQ * (_dot_3pass(h, w2_ref[...]) + b2_ref[...]))
    h = _dot_3pass(h, w3_ref[...])
    window = jnp.exp(-feats[:, 0:1] * dl_ref[...]) + HY_WINDOW_SHIFT
    window = jnp.concatenate([window] * HY_ORDER, axis=-1)
    half = HY_ORDER * HY_WIDTH
    h_fwd = h[:, :half] * window
    h_bwd = h[:, half:] * window
    s_ref[...] = (h_fwd + h_bwd).astype(BF16)
    d_ref[...] = (h_fwd - h_bwd).astype(BF16)


def _hy_filter(feats, w1, b1, w2, b2, w3, deltas, l):
    n = feats.shape[0]
    tn = 256
    half = HY_ORDER * HY_WIDTH
    full = lambda a, b: pl.BlockSpec((None, a, b), lambda i: (l, 0, 0))
    return pl.pallas_call(
        _hy_filter_kernel,
        grid=(n // tn,),
        in_specs=[pl.BlockSpec((tn, LANES), lambda i: (i, 0)),
                  full(LANES, HY_FFN), full(1, HY_FFN), full(HY_FFN, HY_FFN), full(1, HY_FFN),
                  full(HY_FFN, 2 * half),
                  pl.BlockSpec((1, HY_WIDTH), lambda i: (0, 0))],
        out_specs=[pl.BlockSpec((tn, half), lambda i: (i, 0))] * 2,
        out_shape=[jax.ShapeDtypeStruct((n, half), BF16)] * 2,
        compiler_params=_cparams("arbitrary"),
        name="hyena_filter",
    )(feats, w1, b1, w2, b2, w3, deltas)


def _hy_spec_kernel(fc_ref, fs_ref, alt_ref, s_ref, d_ref, hc_ref, hs_ref, hn_ref):
    s = s_ref[...]
    tk, n = fc_ref.shape
    row = lax.broadcasted_iota(jnp.int32, (tk, s.shape[1]), 0) + pl.program_id(0) * tk
    wk = jnp.where(row == 0, 1.0, 2.0) * (1.0 / (2 * n))
    hc_ref[...] = jnp.dot(fc_ref[...], s, preferred_element_type=F32) * wk
    hs_ref[...] = jnp.dot(fs_ref[...], d_ref[...], preferred_element_type=F32) * wk
    hn_ref[...] = jnp.dot(alt_ref[...], s, preferred_element_type=F32)[0:8, :] * (1.0 / (2 * n))


def _hy_spectrum(fc, fs, alt, s, d):
    n = fc.shape[0]
    tk = min(512, n)
    half = s.shape[1]
    out = pl.BlockSpec((tk, half), lambda i: (i, 0))
    return pl.pallas_call(
        _hy_spec_kernel,
        grid=(n // tk,),
        in_specs=[pl.BlockSpec((tk, n), lambda i: (i, 0)),
                  pl.BlockSpec((tk, n), lambda i: (i, 0)),
                  pl.BlockSpec((16, n), lambda i: (0, 0)),
                  pl.BlockSpec((n, half), lambda i: (0, 0)),
                  pl.BlockSpec((n, half), lambda i: (0, 0))],
        out_specs=[out, out, pl.BlockSpec((8, half), lambda i: (0, 0))],
        out_shape=[jax.ShapeDtypeStruct((n, half), F32)] * 2
        + [jax.ShapeDtypeStruct((8, half), F32)],
        compiler_params=_cparams("arbitrary"),
        name="hyena_spectrum",
    )(fc, fs, alt, s, d)


def _short_conv_kernel(u_ref, w_ref, b_ref, o_ref, pad_ref, *, n):
    zeros = jnp.zeros((8, pad_ref.shape[1]), F32)
    pad_ref[0:8, :] = zeros
    pad_ref[8 + n:16 + n, :] = zeros
    pad_ref[8:8 + n, :] = u_ref[...].astype(F32)
    w = w_ref[...]
    y = (pad_ref[7:7 + n, :] * w[0:1, :] + pad_ref[8:8 + n, :] * w[1:2, :]
         + pad_ref[9:9 + n, :] * w[2:3, :] + b_ref[...])
    o_ref[...] = y.astype(BF16)


def _short_conv(z, l, n, w, b):
    r = z.shape[0]
    wc = HY_WIDTH
    return pl.pallas_call(
        functools.partial(_short_conv_kernel, n=n),
        grid=(BATCH, (HY_ORDER + 1)),
        in_specs=[pl.BlockSpec((n, wc), lambda bi, j: (bi, Z_HY // wc + j)),
                  pl.BlockSpec((None, 3, wc), lambda bi, j: (l, 0, j)),
                  pl.BlockSpec((None, 1, wc), lambda bi, j: (l, 0, j))],
        out_specs=pl.BlockSpec((n, wc), lambda bi, j: (bi, j)),
        out_shape=jax.ShapeDtypeStruct((r, (HY_ORDER + 1) * wc), BF16),
        scratch_shapes=[pltpu.VMEM((n + 16, wc), F32)],
        compiler_params=_cparams("arbitrary", "arbitrary"),
        name="short_conv",
    )(z, w, b)


HY_PAIR = 2


def _long_conv_kernel(u_ref, gate_ref, fc_ref, fs_ref, g_ref, alt_ref, hc_ref, hs_ref, hn_ref,
                      skip_ref, gain_ref, o_ref, y_ref, nyq_ref, *, n, tk, final):
    j = pl.program_id(1)
    kt = n // tk

    @pl.when(j == 0)
    def _():
        for p in range(HY_PAIR):
            x_nyq = jnp.dot(alt_ref[...], u_ref[p], preferred_element_type=F32)
            nyq_ref[p] = x_nyq[0:8, :] * hn_ref[...]

    @pl.when(j < kt)
    def _():
        r0 = pl.multiple_of(j * tk, tk)
        hc, hs = hc_ref[...], hs_ref[...]
        xs_all = [(jnp.dot(fc_ref[...], u_ref[p], preferred_element_type=F32),
                   jnp.dot(fs_ref[...], u_ref[p], preferred_element_type=F32))
                  for p in range(HY_PAIR)]
        for p, (xc, xs) in enumerate(xs_all):
            y_ref[p, pl.ds(r0, tk), :] = (xc * hc - xs * hs).astype(BF16)
            y_ref[p, pl.ds(n + r0, tk), :] = (xc * hs + xs * hc).astype(BF16)

    @pl.when(j >= kt)
    def _():
        t0 = pl.multiple_of((j - kt) * tk, tk)
        t = lax.broadcasted_iota(jnp.int32, (tk, u_ref.shape[-1]), 0)
        alt = (1 - 2 * (t & 1)).astype(F32)
        convs = [jnp.dot(g_ref[...], y_ref[p], preferred_element_type=F32)
                 for p in range(HY_PAIR)]
        for p, conv in enumerate(convs):
            u = u_ref[p, pl.ds(t0, tk), :].astype(F32)
            y = conv + alt * nyq_ref[p][0:1, :] + u * skip_ref[...]
            y = gate_ref[p].astype(F32) * y
            if final:
                y = _rms(y) * gain_ref[...]
            o_ref[p] = y.astype(BF16)


def _long_conv(src, u_col, gate_col, mats, spec, order, skip, gain, l, n, final):
    fc, fs, g, alt = mats
    hc, hs, hn = spec
    r = src[0].shape[0]
    wc = HY_WIDTH
    tk = min(512, n)
    kt = n // tk
    nb = BATCH // HY_PAIR
    u4, gate4 = (a.reshape(nb, HY_PAIR, n, a.shape[-1]) for a in src)
    f_spec = pl.BlockSpec((tk, n), lambda bi, j: (jnp.minimum(j, kt - 1), 0))
    h_spec = pl.BlockSpec((tk, wc), lambda bi, j: (jnp.minimum(j, kt - 1), order))
    tile = lambda col: pl.BlockSpec((None, HY_PAIR, tk, wc),
                                    lambda bi, j: (bi, 0, jnp.maximum(j - kt, 0), col))
    out = pl.pallas_call(
        functools.partial(_long_conv_kernel, n=n, tk=tk, final=final),
        grid=(nb, 2 * kt),
        in_specs=[pl.BlockSpec((None, HY_PAIR, n, wc), lambda bi, j: (bi, 0, 0, u_col)),
                  tile(gate_col),
                  f_spec, f_spec,
                  pl.BlockSpec((tk, 2 * n), lambda bi, j: (jnp.maximum(j - kt, 0), 0)),
                  pl.BlockSpec((16, n), lambda bi, j: (0, 0)),
                  h_spec, h_spec,
                  pl.BlockSpec((8, wc), lambda bi, j: (0, order)),
                  pl.BlockSpec((None, None, 1, wc), lambda bi, j: (l, order, 0, 0)),
                  pl.BlockSpec((None, 1, wc), lambda bi, j: (l, 0, 0))],
        out_specs=tile(0),
        out_shape=jax.ShapeDtypeStruct((nb, HY_PAIR, n, wc), BF16),
        scratch_shapes=[pltpu.VMEM((HY_PAIR, 2 * n, wc), BF16),
                        pltpu.VMEM((HY_PAIR, 8, wc), F32)],
        compiler_params=_cparams("arbitrary", "arbitrary"),
        name="long_conv",
    )(u4, gate4, fc, fs, g, alt, hc, hs, hn, skip, gain)
    return out.reshape(r, wc)


def _dft_mats(n):
    big = 2 * n
    fine = 32
    t = jnp.arange(n, dtype=jnp.int32)

    def trig(k):
        ang = ((k[:, None] * t[None, :]) % big).astype(F32) * (2.0 * math.pi / big)
        return jnp.cos(ang), jnp.sin(ang)

    ca, sa = (m[:, None, :] for m in trig(fine * jnp.arange(n // fine, dtype=jnp.int32)))
    cb, sb = (m[None, :, :] for m in trig(jnp.arange(fine, dtype=jnp.int32)))
    fc = (ca * cb - sa * sb).reshape(n, n).astype(BF16)
    fs = (sa * cb + ca * sb).reshape(n, n).astype(BF16)
    alt = jnp.broadcast_to((1 - 2 * (t % 2)).astype(F32)[None, :], (16, n))
    return fc, fs, jnp.concatenate([fc, fs], axis=1), alt.astype(BF16)


def _hy_features(n):
    t = jnp.linspace(0.0, 1.0, n, dtype=F32)[:, None]
    bands = (HY_EMB - 1) // 2
    w = 2 * math.pi * jnp.arange(n, dtype=F32)[:, None] / n
    f = jnp.linspace(1e-4, bands - 1, bands, dtype=F32)[None, :]
    feats = jnp.concatenate([t, jnp.cos(f * w), -jnp.sin(f * w)], axis=-1)
    return jnp.pad(feats, ((0, 0), (0, LANES - HY_EMB)))


def _rope_table():
    n_freq = MLA_ROPE // 4
    inv_freq = ROPE_BASE ** (-jnp.arange(n_freq, dtype=F32) / n_freq)
    rows = SEQ // GRID_W
    row = jnp.repeat(jnp.arange(rows), GRID_W).astype(F32)
    col = jnp.tile(jnp.arange(GRID_W), rows).astype(F32)
    ar = row[:, None] * inv_freq[None, :]
    ac = col[:, None] * inv_freq[None, :]
    cos_part = jnp.concatenate([jnp.cos(ar), jnp.cos(ar), jnp.cos(ac), jnp.cos(ac)], axis=-1)
    sin_part = jnp.concatenate([-jnp.sin(ar), jnp.sin(ar), -jnp.sin(ac), jnp.sin(ac)], axis=-1)
    return jnp.concatenate([cos_part, sin_part], axis=-1)


_ROPE_PARTNER = np.concatenate([np.arange(16, 32), np.arange(0, 16),
                                np.arange(48, 64), np.arange(32, 48)])
_QUARTER = MLA_ROPE // 4


def _partner_pieces(base):
    return [(base + int(_ROPE_PARTNER[i]), _QUARTER) for i in range(0, MLA_ROPE, _QUARTER)]


def _w_in_pieces():
    off = np.cumsum([0, MLA_KV_LORA, MLA_ROPE, RET_HEADS * RET_DK, RET_WIDTH, MLA_Q_LORA,
                     RET_HEADS * RET_DK, RET_WIDTH])
    o_kv, o_kr, o_rk, o_rv, o_q, o_rq, o_gate, o_hy = [int(v) for v in off]
    pieces = [(o_hy, 3 * HY_WIDTH), (o_rv, RET_WIDTH), (o_q, MLA_Q_LORA), (o_gate, RET_WIDTH),
              (o_rq, RET_HEADS * RET_DK), (o_rk, RET_HEADS * RET_DK),
              (o_kv, MLA_KV_LORA), (o_kr, MLA_ROPE)] + _partner_pieces(o_kr)
    pieces.append((None, Z_COLS - sum(w for _, w in pieces)))
    return pieces


def _wuq_pieces():
    per = MLA_NOPE + MLA_ROPE
    pieces = []
    for h in range(MLA_HEADS):
        pieces += [(h * per, per)] + _partner_pieces(h * per + MLA_NOPE)
    return pieces


def _relayout_kernel(w_ref, o_ref, *, pieces):
    col = 0
    for start, width in pieces:
        if start is None:
            o_ref[:, col:col + width] = jnp.zeros((o_ref.shape[0], width), o_ref.dtype)
        else:
            o_ref[:, col:col + width] = w_ref[:, start:start + width].astype(o_ref.dtype)
        col += width


def _relayout_cols(w, pieces):
    nl, k, n_in = w.shape
    n_out = sum(width for _, width in pieces)
    tr = 256
    return pl.pallas_call(
        functools.partial(_relayout_kernel, pieces=tuple(pieces)),
        grid=(nl, k // tr),
        in_specs=[pl.BlockSpec((None, tr, n_in), lambda l, i: (l, i, 0))],
        out_specs=pl.BlockSpec((None, tr, n_out), lambda l, i: (l, i, 0)),
        out_shape=jax.ShapeDtypeStruct((nl, k, n_out), BF16),
        compiler_params=_cparams("arbitrary", "arbitrary"),
        name="relayout_cols",
    )(w)


def _relayout_rows_kernel(w_ref, o_ref, *, pieces):
    row = 0
    for start, width in pieces:
        if start is None:
            o_ref[row:row + width, :] = jnp.zeros((width, o_ref.shape[1]), o_ref.dtype)
        else:
            o_ref[row:row + width, :] = w_ref[start:start + width, :].astype(o_ref.dtype)
        row += width


def _relayout_rows(w_t, pieces):
    nl, n_in, k = w_t.shape
    n_out = sum(width for _, width in pieces)
    tc = 512
    return pl.pallas_call(
        functools.partial(_relayout_rows_kernel, pieces=tuple(pieces)),
        grid=(nl, k // tc),
        in_specs=[pl.BlockSpec((None, n_in, tc), lambda l, i: (l, 0, i))],
        out_specs=pl.BlockSpec((None, n_out, tc), lambda l, i: (l, 0, i)),
        out_shape=jax.ShapeDtypeStruct((nl, n_out, k), BF16),
        compiler_params=_cparams("arbitrary", "arbitrary"),
        name="relayout_rows",
    )(w_t)


def kernel(x, c, ctx, c_ctx, ada_w, ada_b, ffn1_gate, ffn1_up, ffn1_down, w_in, mla_q_norm, mla_wuq, mla_kv_norm, mla_wukv, mla_qn_nope, mla_qn_rope, mla_kn_nope, mla_kn_rope, mla_out_norm, hy_conv_w, hy_conv_b, hy_ffn_w1, hy_ffn_b1, hy_ffn_w2, hy_ffn_b2, hy_ffn_w3, hy_skip, hy_out_norm, ret_decay, ret_gn_w, ret_gn_b, w_out, ffn2_gate, ffn2_up, ffn2_down):
    nl = DEPTH
    d = D_MODEL

    w1g, w1u, w1d = ffn1_gate, ffn1_up, ffn1_down
    w2g, w2u, w2d = ffn2_gate, ffn2_up, ffn2_down
    w_in_p = _relayout_rows(jnp.swapaxes(w_in, 1, 2), _w_in_pieces())
    wuq_p = _relayout_cols(mla_wuq, _wuq_pieces())
    wukv = mla_wukv.astype(BF16)
    w_out_b = w_out.astype(BF16)
    row3 = lambda g: g.reshape(nl, 1, -1)
    partner = jnp.asarray(_ROPE_PARTNER)
    g_qr = row3(jnp.concatenate([mla_qn_rope, mla_qn_rope[:, partner]], axis=-1))
    g_kr = row3(jnp.concatenate([mla_kn_rope, mla_kn_rope[:, partner]], axis=-1))
    g_q, g_kv, g_qn, g_kn = row3(mla_q_norm), row3(mla_kv_norm), row3(mla_qn_nope), row3(mla_kn_nope)
    g_attn, g_hy = row3(mla_out_norm), row3(hy_out_norm)
    gn_w, gn_b = row3(ret_gn_w), row3(ret_gn_b)
    hy_b = row3(hy_conv_b)
    hy_w1 = jnp.pad(hy_ffn_w1, ((0, 0), (0, LANES - HY_EMB), (0, 0)))
    hy_b1, hy_b2 = row3(hy_ffn_b1), row3(hy_ffn_b2)
    hy_skip4 = hy_skip.reshape(nl, HY_ORDER, 1, HY_WIDTH)
    dec = jnp.broadcast_to(ret_decay.astype(F32)[:, :, :, None, None], (nl, 2, RET_HEADS, 8, LANES))

    tab_x = _rope_table()
    tab_c = jnp.concatenate([jnp.ones((CTX_LEN, MLA_ROPE), F32), jnp.zeros((CTX_LEN, MLA_ROPE), F32)], -1)
    deltas = jnp.abs(jnp.linspace(HY_MIN_DECAY, HY_MAX_DECAY, HY_WIDTH, dtype=F32))[None, :]
    mats = {n: _dft_mats(n) for n in (SEQ, CTX_LEN)}
    feats = {n: _hy_features(n) for n in (SEQ, CTX_LEN)}

    cond = jnp.zeros((16, d), F32).at[0].set(c_ctx).at[1:1 + BATCH].set(c)
    mods = _modulation(cond, ada_w, ada_b).reshape(nl, 16, N_MOD, d)

    sx = x.reshape(BATCH * SEQ, d)
    sc = ctx.reshape(BATCH * CTX_LEN, d)

    def mod_tables(l):
        return ([mods[l, 1:1 + BATCH, j][:, None, :] for j in range(N_MOD)],
                [mods[l, 0:1, j][:, None, :] for j in range(N_MOD)])

    all_mods = [mod_tables(l) for l in range(nl)]

    def ffn(s, h, n, gate, wg, wu, wd, l, wd_bf=None):
        u, rounded = _ffn_up(h, wg, wu, l, None if wd_bf is not None else wd)
        wd_bf = rounded if wd_bf is None else wd_bf
        return _ffn_down(u, wd_bf, s, gate, n), wd_bf

    def hyena(z, l, n, spec):
        cv = _short_conv(z, l, n, hy_conv_w, hy_b)
        y1 = _long_conv((cv, cv), 2, 0, mats[n], spec, 0, hy_skip4, g_hy, l, n, False)
        return _long_conv((y1, cv), 0, 1, mats[n], spec, 1, hy_skip4, g_hy, l, n, True)

    for l in range(nl):
        need_ctx_out = l < nl - 1
        mod_x, mod_c = all_mods[l]

        sx, wd_bf = ffn(sx, _normmod(sx, mod_x[0], mod_x[1], SEQ), SEQ, mod_x[2], w1g, w1u, w1d, l)
        sc, _ = ffn(sc, _normmod(sc, mod_c[0], mod_c[1], CTX_LEN), CTX_LEN, mod_c[2], w1g, w1u, w1d,
                    l, wd_bf)

        zc = _norm_matmul(sc, mod_c[3], mod_c[4], CTX_LEN, w_in_p, l)
        zx = _norm_matmul(sx, mod_x[3], mod_x[4], SEQ, w_in_p, l)

        kc, vc = _kv_proj(zc, l, CTX_LEN, wukv, g_kv, g_kn, g_kr, tab_c)
        kx, vx = _kv_proj(zx, l, SEQ, wukv, g_kv, g_kn, g_kr, tab_x)
        zero_state = jnp.zeros((BATCH, RET_HEADS, 2, LANES, RET_DV), F32)
        ret_c, state_c = _retention(zc, l, CTX_LEN, dec, zero_state, gn_w, gn_b)

        filt = {}
        for n in ((SEQ, CTX_LEN) if need_ctx_out else (SEQ,)):
            s_f, d_f = _hy_filter(feats[n], hy_w1, hy_b1, hy_ffn_w2, hy_b2, hy_ffn_w3, deltas, l)
            fc, fs, _, alt = mats[n]
            filt[n] = _hy_spectrum(fc, fs, alt, s_f, d_f)

        qx = _q_proj(zx, l, SEQ, wuq_p, g_q, g_qn, g_qr, tab_x)
        attn_x = _attention(qx, [kc, kx], [vc, vx], [CTX_LEN, SEQ], l, g_attn)
        hy_x = hyena(zx, l, SEQ, filt[SEQ])
        ret_x, _ = _retention(zx, l, SEQ, dec, state_c, gn_w, gn_b)
        sx, hx = _w_out([attn_x, hy_x, ret_x], w_out_b, l, sx, mod_x[5], SEQ, (mod_x[6], mod_x[7]))
        sx, wd_bf = ffn(sx, hx, SEQ, mod_x[8], w2g, w2u, w2d, l)

        if need_ctx_out:
            qc = _q_proj(zc, l, CTX_LEN, wuq_p, g_q, g_qn, g_qr, tab_c)
            attn_c = _attention(qc, [kc], [vc], [CTX_LEN], l, g_attn)
            hy_c = hyena(zc, l, CTX_LEN, filt[CTX_LEN])
            sc, hc = _w_out([attn_c, hy_c, ret_c], w_out_b, l, sc, mod_c[5], CTX_LEN,
                            (mod_c[6], mod_c[7]))
            sc, _ = ffn(sc, hc, CTX_LEN, mod_c[8], w2g, w2u, w2d, l, wd_bf)

    return sx.reshape(BATCH, SEQ, d)
```

```python
import functools
import math

import jax
import jax.numpy as jnp
import numpy as np
from jax import lax
from jax.experimental import pallas as pl
from jax.experimental.pallas import tpu as pltpu

F32 = jnp.float32
BF16 = jnp.bfloat16

D_MODEL = 2048
BATCH = 8
SEQ = 2048
DEPTH = 2
GRID_W = 64
CTX_LEN = 256
EPS = 1e-6
N_MOD = 9
FFN_HIDDEN = 5632

MLA_HEADS = 8
MLA_NOPE = 128
MLA_ROPE = 64
MLA_V = 128
MLA_Q_LORA = 512
MLA_KV_LORA = 256
MLA_WIDTH = MLA_HEADS * MLA_V
ROPE_BASE = 10000.0

HY_WIDTH = 512
HY_ORDER = 2
HY_EMB = 33
HY_FFN = 64
HY_SIN_FREQ = 1.0
HY_MIN_DECAY = math.log(1e-2) / 1.5
HY_MAX_DECAY = math.log(1e-2) / 0.3
HY_WINDOW_SHIFT = 0.05

RET_HEADS = 4
RET_DK = 64
RET_DV = 128
RET_WIDTH = RET_HEADS * RET_DV
RET_CHUNK = 128

VMEM_LIMIT_BYTES = 60 * 1024 * 1024
LANES = 128
HEAD_SLOT = 256

Z_HY = 0
Z_RV = 1536
Z_QLAT = 2048
Z_RGATE = 2560
Z_RQ = 3072
Z_RK = 3328
Z_KVLAT = 3584
Z_KR = 3840
Z_COLS = 4096

TM = 1024
TM_WIDE = 2048
TN = 512


def _cparams(*sem):
    return pltpu.CompilerParams(dimension_semantics=sem, vmem_limit_bytes=VMEM_LIMIT_BYTES)


def _rms(x, width=None):
    width = x.shape[-1] if width is None else width
    return x * lax.rsqrt(jnp.sum(x * x, axis=-1, keepdims=True) / width + EPS)


def _silu(x):
    return x * jax.nn.sigmoid(x)


def _dot_3pass(a, w):
    a_hi, w_hi = a.astype(BF16), w.astype(BF16)
    a_lo = (a - a_hi.astype(F32)).astype(BF16)
    w_lo = (w - w_hi.astype(F32)).astype(BF16)
    rows = a.shape[0]
    acc = jnp.dot(jnp.concatenate([a_hi, a_lo], axis=0), w_hi, preferred_element_type=F32)
    return acc[:rows] + acc[rows:] + jnp.dot(a_hi, w_lo, preferred_element_type=F32)


def _mod_kernel(c_ref, w_ref, b_ref, o_ref):
    o_ref[...] = _dot_3pass(_silu(c_ref[...]), w_ref[...]) + b_ref[...]


def _modulation(cond, ada_w, ada_b):
    nl, d, nout = ada_w.shape
    bn = 1024
    return pl.pallas_call(
        _mod_kernel,
        grid=(nl, nout // bn),
        in_specs=[pl.BlockSpec((16, d), lambda l, j: (0, 0)),
                  pl.BlockSpec((None, d, bn), lambda l, j: (l, 0, j)),
                  pl.BlockSpec((None, 1, bn), lambda l, j: (l, 0, j))],
        out_specs=pl.BlockSpec((None, 16, bn), lambda l, j: (l, 0, j)),
        out_shape=jax.ShapeDtypeStruct((nl, 16, nout), F32),
        compiler_params=_cparams("arbitrary", "arbitrary"),
        name="modulation",
    )(cond, ada_w, ada_b.reshape(nl, 1, nout))


def _normmod_kernel(x_ref, sh_ref, sc_ref, o_ref):
    y = _rms(x_ref[...])
    o_ref[...] = (y * (1.0 + sc_ref[...]) + sh_ref[...]).astype(BF16)


def _normmod(s, shift, scale, n):
    r, d = s.shape
    tm = min(512, n)
    per_batch = shift.shape[0] > 1
    mod_spec = pl.BlockSpec((None, 1, d), lambda i: ((i * tm) // n if per_batch else 0, 0, 0))
    return pl.pallas_call(
        _normmod_kernel,
        grid=(r // tm,),
        in_specs=[pl.BlockSpec((tm, d), lambda i: (i, 0)), mod_spec, mod_spec],
        out_specs=pl.BlockSpec((tm, d), lambda i: (i, 0)),
        out_shape=jax.ShapeDtypeStruct((r, d), BF16),
        compiler_params=_cparams("arbitrary"),
        name="normmod",
    )(s, shift, scale)


def _ffn_up_kernel(h_ref, wg_ref, wu_ref, *refs, round_wd):
    if round_wd:
        wd_ref, o_ref, wd_bf, wg_bf, wu_bf = refs
        wd_bf[...] = wd_ref[...].astype(BF16)
    else:
        o_ref, wg_bf, wu_bf = refs

    @pl.when(pl.program_id(1) == 0)
    def _():
        wg_bf[...] = wg_ref[...].astype(BF16)
        wu_bf[...] = wu_ref[...].astype(BF16)

    h = h_ref[...]
    g = jnp.dot(h, wg_bf[...], preferred_element_type=F32)
    u = jnp.dot(h, wu_bf[...], preferred_element_type=F32)
    o_ref[...] = (_silu(g) * u).astype(BF16)


def _ffn_up(h, wg, wu, l, wd=None):
    r, d = h.shape
    f = wg.shape[-1]
    w_spec = pl.BlockSpec((None, d, TN), lambda j, i: (l, 0, j))
    tm = TM
    round_wd = wd is not None
    u_spec = pl.BlockSpec((tm, TN), lambda j, i: (i, j))
    u_shape = jax.ShapeDtypeStruct((r, f), BF16)
    steps = r // tm
    rows = f // (f // TN * steps)
    slab = pl.BlockSpec((rows, d), lambda j, i: (j * steps + i, 0))
    outs = pl.pallas_call(
        functools.partial(_ffn_up_kernel, round_wd=round_wd),
        grid=(f // TN, steps),
        in_specs=[pl.BlockSpec((tm, d), lambda j, i: (i, 0)), w_spec, w_spec]
        + ([pl.BlockSpec((None, rows, d), lambda j, i: (l, j * steps + i, 0))] if round_wd else []),
        out_specs=[u_spec, slab] if round_wd else u_spec,
        out_shape=[u_shape, jax.ShapeDtypeStruct((f, d), BF16)] if round_wd else u_shape,
        scratch_shapes=[pltpu.VMEM((d, TN), BF16), pltpu.VMEM((d, TN), BF16)],
        compiler_params=_cparams("arbitrary", "arbitrary"),
        name="ffn_up",
    )(h, wg, wu, *([wd] if round_wd else []))
    return outs if round_wd else (outs, None)


def _mod_spec(table, tm, n, d):
    per_batch = table.shape[0] > 1
    return pl.BlockSpec((None, 1, d), lambda i, *_: ((i * tm) // n if per_batch else 0, 0, 0))


def _norm_mm_kernel(s_ref, sh_ref, sc_ref, w_ref, o_ref, h_ref):
    @pl.when(pl.program_id(1) == 0)
    def _():
        h_ref[...] = (_rms(s_ref[...]) * (1.0 + sc_ref[...]) + sh_ref[...]).astype(BF16)

    o_ref[...] = lax.dot_general(h_ref[...], w_ref[...], (((1,), (1,)), ((), ())),
                                 preferred_element_type=F32).astype(o_ref.dtype)


def _norm_matmul(s, shift, scale, n, w_t, l):
    r, d = s.shape
    nout = w_t.shape[1]
    tm = TM_WIDE
    return pl.pallas_call(
        _norm_mm_kernel,
        grid=(r // tm, nout // TN),
        in_specs=[pl.BlockSpec((tm, d), lambda i, j: (i, 0)),
                  _mod_spec(shift, tm, n, d), _mod_spec(scale, tm, n, d),
                  pl.BlockSpec((None, TN, d), lambda i, j: (l, j, 0))],
        out_specs=pl.BlockSpec((tm, TN), lambda i, j: (i, j)),
        out_shape=jax.ShapeDtypeStruct((r, nout), BF16),
        scratch_shapes=[pltpu.VMEM((tm, d), BF16)],
        compiler_params=_cparams("arbitrary", "arbitrary"),
        name="norm_matmul",
    )(s, shift, scale, w_t)


def _ffn_down_kernel(u_ref, w_ref, s_ref, g_ref, o_ref):
    acc = jnp.dot(u_ref[...], w_ref[...], preferred_element_type=F32)
    o_ref[...] = s_ref[...] + (0.5 * g_ref[...]) * acc


def _ffn_down(u, w, s, gate, n):
    r, d = s.shape
    f = u.shape[1]
    per_batch = gate.shape[0] > 1
    return pl.pallas_call(
        _ffn_down_kernel,
        grid=(r // TM, d // TN),
        in_specs=[pl.BlockSpec((TM, f), lambda i, j: (i, 0)),
                  pl.BlockSpec((f, TN), lambda i, j: (0, j)),
                  pl.BlockSpec((TM, TN), lambda i, j: (i, j)),
                  pl.BlockSpec((None, 1, TN), lambda i, j: ((i * TM) // n if per_batch else 0, 0, j))],
        out_specs=pl.BlockSpec((TM, TN), lambda i, j: (i, j)),
        out_shape=jax.ShapeDtypeStruct((r, d), F32),
        compiler_params=_cparams("arbitrary", "arbitrary"),
        name="ffn_down",
    )(u, w, s, gate)


def _w_out_kernel(*refs, n_in):
    a_refs = refs[:n_in]
    w_ref, s_ref, g_ref, sh_ref, sc_ref, o_ref, h_ref = refs[n_in:]
    half = s_ref.shape[0] // 2
    accs = []
    for r0 in (0, half):
        acc, row = None, 0
        for a_ref in a_refs:
            kp = a_ref.shape[1]
            part = jnp.dot(a_ref[r0:r0 + half, :], w_ref[row:row + kp, :],
                           preferred_element_type=F32)
            acc = part if acc is None else acc + part
            row += kp
        accs.append(acc)
    for r0, acc in zip((0, half), accs):
        rows = slice(r0, r0 + half)
        s_new = s_ref[rows, :] + g_ref[...] * acc
        o_ref[rows, :] = s_new
        h_ref[rows, :] = (_rms(s_new) * (1.0 + sc_ref[...]) + sh_ref[...]).astype(BF16)


def _w_out(parts, w, l, s, gate, n, next_mod):
    r, d = s.shape
    tm = TM // 2
    shift, scale = next_mod
    row = pl.BlockSpec((tm, d), lambda i: (i, 0))
    return pl.pallas_call(
        functools.partial(_w_out_kernel, n_in=len(parts)),
        grid=(r // tm,),
        in_specs=[pl.BlockSpec((tm, p.shape[1]), lambda i: (i, 0)) for p in parts]
        + [pl.BlockSpec((None, d, d), lambda i: (l, 0, 0)),
           row, _mod_spec(gate, tm, n, d), _mod_spec(shift, tm, n, d), _mod_spec(scale, tm, n, d)],
        out_specs=[row, row],
        out_shape=[jax.ShapeDtypeStruct((r, d), F32), jax.ShapeDtypeStruct((r, d), BF16)],
        compiler_params=_cparams("arbitrary"),
        name="w_out",
    )(*parts, w, s, gate, shift, scale)


def _rope_lanes(x):
    lane = lax.broadcasted_iota(jnp.int32, x.shape, 1)
    ms = jnp.sum(jnp.where(lane < MLA_ROPE, x * x, 0.0), axis=-1, keepdims=True) / MLA_ROPE
    return x * lax.rsqrt(ms + EPS)


def _kv_kernel(lat_ref, kr_ref, gkv_ref, w_ref, gkn_ref, gkr_ref, tab_ref, k_ref, v_ref):
    a = _rms(lat_ref[...].astype(F32)) * gkv_ref[...]
    kv = jnp.dot(a.astype(BF16), w_ref[...], preferred_element_type=F32)
    t = _rope_lanes(kr_ref[...].astype(F32)) * gkr_ref[...] * tab_ref[...]
    krot = (t + pltpu.roll(t, MLA_ROPE, 1)).astype(BF16)
    for h in range(MLA_HEADS):
        c0 = h * HEAD_SLOT
        kn = _rms(kv[:, c0:c0 + MLA_NOPE]) * gkn_ref[...]
        k_ref[:, c0:c0 + MLA_NOPE] = kn.astype(BF16)
        k_ref[:, c0 + MLA_NOPE:c0 + HEAD_SLOT] = krot
        v_ref[:, h * MLA_V:(h + 1) * MLA_V] = kv[:, c0 + MLA_NOPE:c0 + HEAD_SLOT].astype(BF16)


def _kv_proj(z, l, n, wukv, g_kv, g_kn, g_kr, tab):
    r = z.shape[0]
    tm = min(512, n)
    nt = tab.shape[0] // tm
    vec = lambda w: pl.BlockSpec((None, 1, w), lambda i: (l, 0, 0))
    return pl.pallas_call(
        _kv_kernel,
        grid=(r // tm,),
        in_specs=[pl.BlockSpec((tm, MLA_KV_LORA), lambda i: (i, Z_KVLAT // MLA_KV_LORA)),
                  pl.BlockSpec((tm, LANES), lambda i: (i, Z_KR // LANES)),
                  vec(MLA_KV_LORA),
                  pl.BlockSpec((None, MLA_KV_LORA, MLA_HEADS * HEAD_SLOT), lambda i: (l, 0, 0)),
                  vec(MLA_NOPE), vec(LANES),
                  pl.BlockSpec((tm, LANES), lambda i: (i % nt, 0))],
        out_specs=[pl.BlockSpec((tm, MLA_HEADS * HEAD_SLOT), lambda i: (i, 0)),
                   pl.BlockSpec((tm, MLA_WIDTH), lambda i: (i, 0))],
        out_shape=[jax.ShapeDtypeStruct((r, MLA_HEADS * HEAD_SLOT), BF16),
                   jax.ShapeDtypeStruct((r, MLA_WIDTH), BF16)],
        compiler_params=_cparams("arbitrary"),
        name="kv_proj",
    )(z, z, g_kv, wukv, g_kn, g_kr, tab)


def _q_kernel(lat_ref, gq_ref, w_ref, gn_ref, gr_ref, tab_ref, q_ref, *, scale):
    a = _rms(lat_ref[...].astype(F32)) * gq_ref[...]
    q = jnp.dot(a.astype(BF16), w_ref[...], preferred_element_type=F32)
    rope_scale = gr_ref[...] * tab_ref[...] * scale
    for h in range(MLA_HEADS):
        c0 = h * HEAD_SLOT
        qn = _rms(q[:, c0:c0 + MLA_NOPE]) * (gn_ref[...] * scale)
        q_ref[:, c0:c0 + MLA_NOPE] = qn.astype(BF16)
        qr = _rope_lanes(q[:, c0 + MLA_NOPE:c0 + HEAD_SLOT]) * rope_scale
        q_ref[:, c0 + MLA_NOPE:c0 + HEAD_SLOT] = qr.astype(BF16)


def _q_proj(z, l, n, wuq, g_q, g_qn, g_qr, tab):
    r = z.shape[0]
    tm = min(512, n)
    nt = tab.shape[0] // tm
    vec = lambda w: pl.BlockSpec((None, 1, w), lambda i: (l, 0, 0))
    return pl.pallas_call(
        functools.partial(_q_kernel, scale=(MLA_NOPE + MLA_ROPE) ** -0.5 * math.log2(math.e)),
        grid=(r // tm,),
        in_specs=[pl.BlockSpec((tm, MLA_Q_LORA), lambda i: (i, Z_QLAT // MLA_Q_LORA)),
                  vec(MLA_Q_LORA),
                  pl.BlockSpec((None, MLA_Q_LORA, MLA_HEADS * HEAD_SLOT), lambda i: (l, 0, 0)),
                  vec(MLA_NOPE), vec(LANES),
                  pl.BlockSpec((tm, LANES), lambda i: (i % nt, 0))],
        out_specs=pl.BlockSpec((tm, MLA_HEADS * HEAD_SLOT), lambda i: (i, 0)),
        out_shape=jax.ShapeDtypeStruct((r, MLA_HEADS * HEAD_SLOT), BF16),
        compiler_params=_cparams("arbitrary"),
        name="q_proj",
    )(z, g_q, wuq, g_qn, g_qr, tab)


def _attn_kernel(*refs, n_src):
    q_ref = refs[0]
    k_refs, v_refs = refs[1:1 + n_src], refs[1 + n_src:1 + 2 * n_src]
    g_ref, o_ref, acc_ref = refs[1 + 2 * n_src:]

    def scores(h):
        hs = slice(h * HEAD_SLOT, (h + 1) * HEAD_SLOT)
        return [lax.dot_general(q_ref[:, hs], k_ref[:, hs], (((1,), (1,)), ((), ())),
                                preferred_element_type=F32) for k_ref in k_refs]

    ssq = None
    ss_next = scores(0)
    for h in range(MLA_HEADS):
        hs = slice(h * HEAD_SLOT, (h + 1) * HEAD_SLOT)
        ss, ss_next = ss_next, (scores(h + 1) if h + 1 < MLA_HEADS else None)
        m = functools.reduce(jnp.maximum, [jnp.max(s, axis=-1, keepdims=True) for s in ss])
        vs = [jnp.concatenate([v_ref[:, h * MLA_V:(h + 1) * MLA_V],
                               jnp.ones((v_ref.shape[0], HEAD_SLOT - MLA_V), BF16)], axis=1)
              for v_ref in v_refs]
        o = functools.reduce(jnp.add, [
            jnp.dot(jnp.exp2((s - m).astype(BF16)), v, preferred_element_type=F32)
            for s, v in zip(ss, vs)])
        o = o[:, :MLA_V] / o[:, MLA_V:MLA_V + 1]
        acc_ref[:, h * MLA_V:(h + 1) * MLA_V] = o
        sq = jnp.sum(o * o, axis=-1, keepdims=True)
        ssq = sq if ssq is None else ssq + sq
    o_ref[...] = (acc_ref[...] * lax.rsqrt(ssq / MLA_WIDTH + EPS) * g_ref[...]).astype(BF16)


def _attention(q, ks, vs, ns, l, g_out):
    r = q.shape[0]
    nq = r // BATCH
    tq = min(512, nq)
    per_b = nq // tq
    k_specs = [pl.BlockSpec((nk, MLA_HEADS * HEAD_SLOT), lambda b, i: (b, 0)) for nk in ns]
    v_specs = [pl.BlockSpec((nk, MLA_WIDTH), lambda b, i: (b, 0)) for nk in ns]
    return pl.pallas_call(
        functools.partial(_attn_kernel, n_src=len(ks)),
        grid=(BATCH, per_b),
        in_specs=[pl.BlockSpec((tq, MLA_HEADS * HEAD_SLOT), lambda b, i: (b * per_b + i, 0))]
        + k_specs + v_specs
        + [pl.BlockSpec((None, 1, MLA_WIDTH), lambda b, i: (l, 0, 0))],
        out_specs=pl.BlockSpec((tq, MLA_WIDTH), lambda b, i: (b * per_b + i, 0)),
        out_shape=jax.ShapeDtypeStruct((r, MLA_WIDTH), BF16),
        scratch_shapes=[pltpu.VMEM((tq, MLA_WIDTH), F32)],
        compiler_params=_cparams("arbitrary", "arbitrary"),
        name="attention",
    )(q, *ks, *vs, g_out)


def _log_sigmoid(x):
    return jnp.minimum(x, 0.0) - jnp.log(1.0 + jnp.exp(-jnp.abs(x)))


def _ret_kernel(q_ref, k_ref, v_ref, gate_ref, dec_ref, s0_ref, gnw_ref, gnb_ref,
                o_ref, st_ref, tab_ref, kt_ref, u_ref, sin_ref, *state_refs, n):
    nc = n // RET_CHUNK
    c = RET_CHUNK

    @pl.when(pl.program_id(0) == 0)
    def _():
        ii = lax.broadcasted_iota(jnp.int32, (c, c), 0).astype(F32)
        jj = lax.broadcasted_iota(jnp.int32, (c, c), 1).astype(F32)
        for h in range(RET_HEADS):
            for d in range(2):
                lg = _log_sigmoid(dec_ref[d, h][0:1, :])
                lgm = jnp.broadcast_to(lg, (c, c))
                if d == 0:
                    diff, qpow, kpow = ii - jj, ii + 1.0, (c - 1.0) - ii
                else:
                    diff, qpow, kpow = jj - ii, c - ii, ii
                intra = jnp.where(diff >= 0, jnp.exp(lgm * jnp.maximum(diff, 0.0)), 0.0)
                if d == 0:
                    tab_ref[h, 0, 0] = intra
                else:
                    tab_ref[h, 0, 0] += intra
                tab_ref[h, d, 1] = jnp.exp(lgm * qpow)
                kpow_t = (c - 1.0) - jj if d == 0 else jj
                k_scale = jnp.where(ii.astype(jnp.int32) // RET_DK == h % 2, RET_DK ** -0.5, 0.0)
                tab_ref[h, d, 2] = jnp.exp(lgm * kpow_t) * k_scale
                tab_ref[h, d, 3] = jnp.exp(lgm * float(c))
                tab_ref[h, d, 4] = k_scale

    def head_slices(h):
        return slice((h // 2) * LANES, (h // 2 + 1) * LANES), slice(h * RET_DV, (h + 1) * RET_DV)

    chains = [(h, d) for h in range(RET_HEADS) for d in range(2)]

    def contributions(ti, carry):
        dots = []
        for ci in (2 * ti, 2 * ti + 1):
            r0 = pl.multiple_of(ci * c, c)
            kt_all = k_ref[pl.ds(r0, c), :].astype(F32).T
            kt_ref[ci] = kt_all.astype(BF16)
            for h in range(RET_HEADS):
                grp, hs = head_slices(h)
                v = v_ref[pl.ds(r0, c), hs]
                for d in range(2):
                    kd_t = (kt_all[grp, :] * tab_ref[h, d, 2]).astype(BF16)
                    dots.append((2 * h + d, ci, jnp.dot(kd_t, v, preferred_element_type=F32)))
        for idx, ci, part in dots:
            u_ref[idx, ci] = part
        return carry

    lax.fori_loop(0, nc // 2, contributions, 0)

    for (h, d), s_ref in zip(chains, state_refs):
        s_ref[...] = s0_ref[h, d]

    def scan(ci, carry):
        for idx, (h, d) in enumerate(chains):
            cc = ci if d == 0 else nc - 1 - ci
            state = state_refs[idx][...]
            sin_ref[idx, cc] = state.astype(BF16)
            state_refs[idx][...] = tab_ref[h, d, 3] * state + u_ref[idx, cc]
        return carry

    lax.fori_loop(0, nc, scan, 0)
    for (h, d), s_ref in zip(chains, state_refs):
        st_ref[h, d] = s_ref[...]

    per_trip = min(4, nc)

    def outputs(ti, carry):
        pairs = [(ti * per_trip + j, h) for j in range(per_trip) for h in range(RET_HEADS)]
        rows = [pl.ds(pl.multiple_of(ci * c, c), c) for ci, _ in pairs]
        qs = [q_ref[r, head_slices(h)[0]] for r, (_, h) in zip(rows, pairs)]
        vs = [v_ref[r, head_slices(h)[1]] for r, (_, h) in zip(rows, pairs)]
        raw = [jnp.dot(q, (kt_ref[ci, head_slices(h)[0], :].astype(F32)
                           * tab_ref[h, 0, 4]).astype(BF16), preferred_element_type=F32)
               for q, (ci, h) in zip(qs, pairs)]
        cross = [[jnp.dot(q, sin_ref[2 * h + d, ci], preferred_element_type=F32)
                  for d in range(2)] for q, (ci, h) in zip(qs, pairs)]
        intra = [jnp.dot((s * tab_ref[h, 0, 0]).astype(BF16), v, preferred_element_type=F32)
                 for s, v, (_, h) in zip(raw, vs, pairs)]
        for r, (_, h), o_in, o_cr in zip(rows, pairs, intra, cross):
            hs = head_slices(h)[1]
            o = o_in + o_cr[0] * tab_ref[h, 0, 1] + o_cr[1] * tab_ref[h, 1, 1]
            mu = jnp.mean(o, axis=-1, keepdims=True)
            var = jnp.mean(jnp.square(o - mu), axis=-1, keepdims=True)
            on = (o - mu) * lax.rsqrt(var + EPS) * gnw_ref[:, hs] + gnb_ref[:, hs]
            gate = gate_ref[r, hs].astype(F32)
            o_ref[r, hs] = (_silu(gate) * on).astype(BF16)
        return carry

    lax.fori_loop(0, nc // per_trip, outputs, 0)


def _retention(z, l, n, dec, s0, gn_w, gn_b):
    r = z.shape[0]
    w = RET_WIDTH
    qk = RET_HEADS * RET_DK
    nc = n // RET_CHUNK
    col = lambda base, width: pl.BlockSpec((n, width), lambda b: (b, base // width))
    st_spec = pl.BlockSpec((None, RET_HEADS, 2, LANES, RET_DV), lambda b: (b, 0, 0, 0, 0))
    gn_spec = pl.BlockSpec((None, 1, w), lambda b: (l, 0, 0))
    return pl.pallas_call(
        functools.partial(_ret_kernel, n=n),
        grid=(BATCH,),
        in_specs=[col(Z_RQ, qk), col(Z_RK, qk), col(Z_RV, w), col(Z_RGATE, w),
                  pl.BlockSpec((None, 2, RET_HEADS, 8, LANES), lambda b: (l, 0, 0, 0, 0)),
                  st_spec, gn_spec, gn_spec],
        out_specs=[pl.BlockSpec((n, w), lambda b: (b, 0)), st_spec],
        out_shape=[jax.ShapeDtypeStruct((r, w), BF16),
                   jax.ShapeDtypeStruct((BATCH, RET_HEADS, 2, LANES, RET_DV), F32)],
        scratch_shapes=[pltpu.VMEM((RET_HEADS, 2, 5, RET_CHUNK, RET_CHUNK), F32),
                        pltpu.VMEM((nc, qk, RET_CHUNK), BF16),
                        pltpu.VMEM((2 * RET_HEADS, nc, LANES, RET_DV), F32),
                        pltpu.VMEM((2 * RET_HEADS, nc, LANES, RET_DV), BF16)]
        + [pltpu.VMEM((LANES, RET_DV), F32)] * (2 * RET_HEADS),
        compiler_params=_cparams("arbitrary"),
        name="retention",
    )(z, z, z, z, dec, s0, gn_w, gn_b)


def _hy_filter_kernel(feat_ref, w1_ref, b1_ref, w2_ref, b2_ref, w3_ref, dl_ref, s_ref, d_ref):
    feats = feat_ref[...]
    h = jnp.sin(HY_SIN_FREQ * (_dot_3pass(feats, w1_ref[...]) + b1_ref[...]))
    h = jnp.sin(HY_SIN_FREQ * (_dot_3pass(h, w2_ref[...]) + b2_ref[...]))
    h = _dot_3pass(h, w3_ref[...])
    window = jnp.exp(-feats[:, 0:1] * dl_ref[...]) + HY_WINDOW_SHIFT
    window = jnp.concatenate([window] * HY_ORDER, axis=-1)
    half = HY_ORDER * HY_WIDTH
    h_fwd = h[:, :half] * window
    h_bwd = h[:, half:] * window
    s_ref[...] = (h_fwd + h_bwd).astype(BF16)
    d_ref[...] = (h_fwd - h_bwd).astype(BF16)


def _hy_filter(feats, w1, b1, w2, b2, w3, deltas, l):
    n = feats.shape[0]
    tn = 256
    half = HY_ORDER * HY_WIDTH
    full = lambda a, b: pl.BlockSpec((None, a, b), lambda i: (l, 0, 0))
    return pl.pallas_call(
        _hy_filter_kernel,
        grid=(n // tn,),
        in_specs=[pl.BlockSpec((tn, LANES), lambda i: (i, 0)),
                  full(LANES, HY_FFN), full(1, HY_FFN), full(HY_FFN, HY_FFN), full(1, HY_FFN),
                  full(HY_FFN, 2 * half),
                  pl.BlockSpec((1, HY_WIDTH), lambda i: (0, 0))],
        out_specs=[pl.BlockSpec((tn, half), lambda i: (i, 0))] * 2,
        out_shape=[jax.ShapeDtypeStruct((n, half), BF16)] * 2,
        compiler_params=_cparams("arbitrary"),
        name="hyena_filter",
    )(feats, w1, b1, w2, b2, w3, deltas)


def _hy_spec_kernel(fc_ref, fs_ref, alt_ref, s_ref, d_ref, hc_ref, hs_ref, hn_ref):
    s = s_ref[...]
    tk, n = fc_ref.shape
    row = lax.broadcasted_iota(jnp.int32, (tk, s.shape[1]), 0) + pl.program_id(0) * tk
    wk = jnp.where(row == 0, 1.0, 2.0) * (1.0 / (2 * n))
    hc_ref[...] = jnp.dot(fc_ref[...], s, preferred_element_type=F32) * wk
    hs_ref[...] = jnp.dot(fs_ref[...], d_ref[...], preferred_element_type=F32) * wk
    hn_ref[...] = jnp.dot(alt_ref[...], s, preferred_element_type=F32)[0:8, :] * (1.0 / (2 * n))


def _hy_spectrum(fc, fs, alt, s, d):
    n = fc.shape[0]
    tk = min(512, n)
    half = s.shape[1]
    out = pl.BlockSpec((tk, half), lambda i: (i, 0))
    return pl.pallas_call(
        _hy_spec_kernel,
        grid=(n // tk,),
        in_specs=[pl.BlockSpec((tk, n), lambda i: (i, 0)),
                  pl.BlockSpec((tk, n), lambda i: (i, 0)),
                  pl.BlockSpec((16, n), lambda i: (0, 0)),
                  pl.BlockSpec((n, half), lambda i: (0, 0)),
                  pl.BlockSpec((n, half), lambda i: (0, 0))],
        out_specs=[out, out, pl.BlockSpec((8, half), lambda i: (0, 0))],
        out_shape=[jax.ShapeDtypeStruct((n, half), F32)] * 2
        + [jax.ShapeDtypeStruct((8, half), F32)],
        compiler_params=_cparams("arbitrary"),
        name="hyena_spectrum",
    )(fc, fs, alt, s, d)


def _short_conv_kernel(u_ref, w_ref, b_ref, o_ref, *, n):
    u = u_ref[...].astype(F32)
    t = lax.broadcasted_iota(jnp.int32, u.shape, 0)
    prev = jnp.where(t == 0, 0.0, pltpu.roll(u, 1, 0))
    nxt = jnp.where(t == n - 1, 0.0, pltpu.roll(u, n - 1, 0))
    w = w_ref[...]
    y = prev * w[0:1, :] + u * w[1:2, :] + nxt * w[2:3, :] + b_ref[...]
    o_ref[...] = y.astype(BF16)


def _short_conv(z, l, n, w, b):
    r = z.shape[0]
    wc = HY_WIDTH
    return pl.pallas_call(
        functools.partial(_short_conv_kernel, n=n),
        grid=(BATCH, (HY_ORDER + 1)),
        in_specs=[pl.BlockSpec((n, wc), lambda bi, j: (bi, Z_HY // wc + j)),
                  pl.BlockSpec((None, 3, wc), lambda bi, j: (l, 0, j)),
                  pl.BlockSpec((None, 1, wc), lambda bi, j: (l, 0, j))],
        out_specs=pl.BlockSpec((n, wc), lambda bi, j: (bi, j)),
        out_shape=jax.ShapeDtypeStruct((r, (HY_ORDER + 1) * wc), BF16),
        compiler_params=_cparams("arbitrary", "arbitrary"),
        name="short_conv",
    )(z, w, b)


HY_PAIR = 2


def _long_conv_kernel(u_ref, gate_ref, fc_ref, fs_ref, g_ref, alt_ref, hc_ref, hs_ref, hn_ref,
                      skip_ref, gain_ref, o_ref, y_ref, nyq_ref, *, n, tk, final):
    j = pl.program_id(1)
    kt = n // tk

    @pl.when(j == 0)
    def _():
        for p in range(HY_PAIR):
            x_nyq = jnp.dot(alt_ref[...], u_ref[p], preferred_element_type=F32)
            nyq_ref[p] = x_nyq[0:8, :] * hn_ref[...]

    @pl.when(j < kt)
    def _():
        r0 = pl.multiple_of(j * tk, tk)
        hc, hs = hc_ref[...], hs_ref[...]
        xs_all = [(jnp.dot(fc_ref[...], u_ref[p], preferred_element_type=F32),
                   jnp.dot(fs_ref[...], u_ref[p], preferred_element_type=F32))
                  for p in range(HY_PAIR)]
        for p, (xc, xs) in enumerate(xs_all):
            y_ref[p, pl.ds(r0, tk), :] = (xc * hc - xs * hs).astype(BF16)
            y_ref[p, pl.ds(n + r0, tk), :] = (xc * hs + xs * hc).astype(BF16)

    @pl.when(j >= kt)
    def _():
        t0 = pl.multiple_of((j - kt) * tk, tk)
        t = lax.broadcasted_iota(jnp.int32, (tk, u_ref.shape[-1]), 0)
        alt = (1 - 2 * (t & 1)).astype(F32)
        convs = [jnp.dot(g_ref[...], y_ref[p], preferred_element_type=F32)
                 for p in range(HY_PAIR)]
        for p, conv in enumerate(convs):
            u = u_ref[p, pl.ds(t0, tk), :].astype(F32)
            y = conv + alt * nyq_ref[p][0:1, :] + u * skip_ref[...]
            y = gate_ref[p].astype(F32) * y
            if final:
                y = _rms(y) * gain_ref[...]
            o_ref[p] = y.astype(BF16)


def _long_conv(src, u_col, gate_col, mats, spec, order, skip, gain, l, n, final):
    fc, fs, g, alt = mats
    hc, hs, hn = spec
    r = src[0].shape[0]
    wc = HY_WIDTH
    tk = min(512, n)
    kt = n // tk
    nb = BATCH // HY_PAIR
    u4, gate4 = (a.reshape(nb, HY_PAIR, n, a.shape[-1]) for a in src)
    f_spec = pl.BlockSpec((tk, n), lambda bi, j: (jnp.minimum(j, kt - 1), 0))
    h_spec = pl.BlockSpec((tk, wc), lambda bi, j: (jnp.minimum(j, kt - 1), order))
    tile = lambda col: pl.BlockSpec((None, HY_PAIR, tk, wc),
                                    lambda bi, j: (bi, 0, jnp.maximum(j - kt, 0), col))
    out = pl.pallas_call(
        functools.partial(_long_conv_kernel, n=n, tk=tk, final=final),
        grid=(nb, 2 * kt),
        in_specs=[pl.BlockSpec((None, HY_PAIR, n, wc), lambda bi, j: (bi, 0, 0, u_col)),
                  tile(gate_col),
                  f_spec, f_spec,
                  pl.BlockSpec((tk, 2 * n), lambda bi, j: (jnp.maximum(j - kt, 0), 0)),
                  pl.BlockSpec((16, n), lambda bi, j: (0, 0)),
                  h_spec, h_spec,
                  pl.BlockSpec((8, wc), lambda bi, j: (0, order)),
                  pl.BlockSpec((None, None, 1, wc), lambda bi, j: (l, order, 0, 0)),
                  pl.BlockSpec((None, 1, wc), lambda bi, j: (l, 0, 0))],
        out_specs=tile(0),
        out_shape=jax.ShapeDtypeStruct((nb, HY_PAIR, n, wc), BF16),
        scratch_shapes=[pltpu.VMEM((HY_PAIR, 2 * n, wc), BF16),
                        pltpu.VMEM((HY_PAIR, 8, wc), F32)],
        compiler_params=_cparams("arbitrary", "arbitrary"),
        name="long_conv",
    )(u4, gate4, fc, fs, g, alt, hc, hs, hn, skip, gain)
    return out.reshape(r, wc)


def _dft_mats(n):
    big = 2 * n
    fine = 32
    t = jnp.arange(n, dtype=jnp.int32)

    def trig(k):
        ang = ((k[:, None] * t[None, :]) % big).astype(F32) * (2.0 * math.pi / big)
        return jnp.cos(ang), jnp.sin(ang)

    ca, sa = (m[:, None, :] for m in trig(fine * jnp.arange(n // fine, dtype=jnp.int32)))
    cb, sb = (m[None, :, :] for m in trig(jnp.arange(fine, dtype=jnp.int32)))
    fc = (ca * cb - sa * sb).reshape(n, n).astype(BF16)
    fs = (sa * cb + ca * sb).reshape(n, n).astype(BF16)
    alt = jnp.broadcast_to((1 - 2 * (t % 2)).astype(F32)[None, :], (16, n))
    return fc, fs, jnp.concatenate([fc, fs], axis=1), alt.astype(BF16)


def _hy_features(n):
    t = jnp.linspace(0.0, 1.0, n, dtype=F32)[:, None]
    bands = (HY_EMB - 1) // 2
    w = 2 * math.pi * jnp.arange(n, dtype=F32)[:, None] / n
    f = jnp.linspace(1e-4, bands - 1, bands, dtype=F32)[None, :]
    feats = jnp.concatenate([t, jnp.cos(f * w), -jnp.sin(f * w)], axis=-1)
    return jnp.pad(feats, ((0, 0), (0, LANES - HY_EMB)))


def _rope_table():
    n_freq = MLA_ROPE // 4
    inv_freq = ROPE_BASE ** (-jnp.arange(n_freq, dtype=F32) / n_freq)
    rows = SEQ // GRID_W
    row = jnp.repeat(jnp.arange(rows), GRID_W).astype(F32)
    col = jnp.tile(jnp.arange(GRID_W), rows).astype(F32)
    ar = row[:, None] * inv_freq[None, :]
    ac = col[:, None] * inv_freq[None, :]
    cos_part = jnp.concatenate([jnp.cos(ar), jnp.cos(ar), jnp.cos(ac), jnp.cos(ac)], axis=-1)
    sin_part = jnp.concatenate([-jnp.sin(ar), jnp.sin(ar), -jnp.sin(ac), jnp.sin(ac)], axis=-1)
    return jnp.concatenate([cos_part, sin_part], axis=-1)


_ROPE_PARTNER = np.concatenate([np.arange(16, 32), np.arange(0, 16),
                                np.arange(48, 64), np.arange(32, 48)])
_QUARTER = MLA_ROPE // 4


def _partner_pieces(base):
    return [(base + int(_ROPE_PARTNER[i]), _QUARTER) for i in range(0, MLA_ROPE, _QUARTER)]


def _w_in_pieces():
    off = np.cumsum([0, MLA_KV_LORA, MLA_ROPE, RET_HEADS * RET_DK, RET_WIDTH, MLA_Q_LORA,
                     RET_HEADS * RET_DK, RET_WIDTH])
    o_kv, o_kr, o_rk, o_rv, o_q, o_rq, o_gate, o_hy = [int(v) for v in off]
    pieces = [(o_hy, 3 * HY_WIDTH), (o_rv, RET_WIDTH), (o_q, MLA_Q_LORA), (o_gate, RET_WIDTH),
              (o_rq, RET_HEADS * RET_DK), (o_rk, RET_HEADS * RET_DK),
              (o_kv, MLA_KV_LORA), (o_kr, MLA_ROPE)] + _partner_pieces(o_kr)
    pieces.append((None, Z_COLS - sum(w for _, w in pieces)))
    return pieces


def _wuq_pieces():
    per = MLA_NOPE + MLA_ROPE
    pieces = []
    for h in range(MLA_HEADS):
        pieces += [(h * per, per)] + _partner_pieces(h * per + MLA_NOPE)
    return pieces


def _relayout_kernel(w_ref, o_ref, *, pieces):
    col = 0
    for start, width in pieces:
        if start is None:
            o_ref[:, col:col + width] = jnp.zeros((o_ref.shape[0], width), o_ref.dtype)
        else:
            o_ref[:, col:col + width] = w_ref[:, start:start + width].astype(o_ref.dtype)
        col += width


def _relayout_cols(w, pieces):
    nl, k, n_in = w.shape
    n_out = sum(width for _, width in pieces)
    tr = 256
    return pl.pallas_call(
        functools.partial(_relayout_kernel, pieces=tuple(pieces)),
        grid=(nl, k // tr),
        in_specs=[pl.BlockSpec((None, tr, n_in), lambda l, i: (l, i, 0))],
        out_specs=pl.BlockSpec((None, tr, n_out), lambda l, i: (l, i, 0)),
        out_shape=jax.ShapeDtypeStruct((nl, k, n_out), BF16),
        compiler_params=_cparams("arbitrary", "arbitrary"),
        name="relayout_cols",
    )(w)


def _relayout_rows_kernel(w_ref, o_ref, *, pieces):
    row = 0
    for start, width in pieces:
        if start is None:
            o_ref[row:row + width, :] = jnp.zeros((width, o_ref.shape[1]), o_ref.dtype)
        else:
            o_ref[row:row + width, :] = w_ref[start:start + width, :].astype(o_ref.dtype)
        row += width


def _relayout_rows(w_t, pieces):
    nl, n_in, k = w_t.shape
    n_out = sum(width for _, width in pieces)
    tc = 512
    return pl.pallas_call(
        functools.partial(_relayout_rows_kernel, pieces=tuple(pieces)),
        grid=(nl, k // tc),
        in_specs=[pl.BlockSpec((None, n_in, tc), lambda l, i: (l, 0, i))],
        out_specs=pl.BlockSpec((None, n_out, tc), lambda l, i: (l, 0, i)),
        out_shape=jax.ShapeDtypeStruct((nl, n_out, k), BF16),
        compiler_params=_cparams("arbitrary", "arbitrary"),
        name="relayout_rows",
    )(w_t)


def kernel(x, c, ctx, c_ctx, ada_w, ada_b, ffn1_gate, ffn1_up, ffn1_down, w_in, mla_q_norm, mla_wuq, mla_kv_norm, mla_wukv, mla_qn_nope, mla_qn_rope, mla_kn_nope, mla_kn_rope, mla_out_norm, hy_conv_w, hy_conv_b, hy_ffn_w1, hy_ffn_b1, hy_ffn_w2, hy_ffn_b2, hy_ffn_w3, hy_skip, hy_out_norm, ret_decay, ret_gn_w, ret_gn_b, w_out, ffn2_gate, ffn2_up, ffn2_down):
    nl = DEPTH
    d = D_MODEL

    w1g, w1u, w1d = ffn1_gate, ffn1_up, ffn1_down
    w2g, w2u, w2d = ffn2_gate, ffn2_up, ffn2_down
    w_in_p = _relayout_rows(jnp.swapaxes(w_in, 1, 2), _w_in_pieces())
    wuq_p = _relayout_cols(mla_wuq, _wuq_pieces())
    wukv = mla_wukv.astype(BF16)
    w_out_b = w_out.astype(BF16)
    row3 = lambda g: g.reshape(nl, 1, -1)
    partner = jnp.asarray(_ROPE_PARTNER)
    g_qr = row3(jnp.concatenate([mla_qn_rope, mla_qn_rope[:, partner]], axis=-1))
    g_kr = row3(jnp.concatenate([mla_kn_rope, mla_kn_rope[:, partner]], axis=-1))
    g_q, g_kv, g_qn, g_kn = row3(mla_q_norm), row3(mla_kv_norm), row3(mla_qn_nope), row3(mla_kn_nope)
    g_attn, g_hy = row3(mla_out_norm), row3(hy_out_norm)
    gn_w, gn_b = row3(ret_gn_w), row3(ret_gn_b)
    hy_b = row3(hy_conv_b)
    hy_w1 = jnp.pad(hy_ffn_w1, ((0, 0), (0, LANES - HY_EMB), (0, 0)))
    hy_b1, hy_b2 = row3(hy_ffn_b1), row3(hy_ffn_b2)
    hy_skip4 = hy_skip.reshape(nl, HY_ORDER, 1, HY_WIDTH)
    dec = jnp.broadcast_to(ret_decay.astype(F32)[:, :, :, None, None], (nl, 2, RET_HEADS, 8, LANES))

    tab_x = _rope_table()
    tab_c = jnp.concatenate([jnp.ones((CTX_LEN, MLA_ROPE), F32), jnp.zeros((CTX_LEN, MLA_ROPE), F32)], -1)
    deltas = jnp.abs(jnp.linspace(HY_MIN_DECAY, HY_MAX_DECAY, HY_WIDTH, dtype=F32))[None, :]
    mats = {n: _dft_mats(n) for n in (SEQ, CTX_LEN)}
    feats = {n: _hy_features(n) for n in (SEQ, CTX_LEN)}

    cond = jnp.zeros((16, d), F32).at[0].set(c_ctx).at[1:1 + BATCH].set(c)
    mods = _modulation(cond, ada_w, ada_b).reshape(nl, 16, N_MOD, d)

    sx = x.reshape(BATCH * SEQ, d)
    sc = ctx.reshape(BATCH * CTX_LEN, d)

    def mod_tables(l):
        return ([mods[l, 1:1 + BATCH, j][:, None, :] for j in range(N_MOD)],
                [mods[l, 0:1, j][:, None, :] for j in range(N_MOD)])

    all_mods = [mod_tables(l) for l in range(nl)]

    def ffn(s, h, n, gate, wg, wu, wd, l, wd_bf=None):
        u, rounded = _ffn_up(h, wg, wu, l, None if wd_bf is not None else wd)
        wd_bf = rounded if wd_bf is None else wd_bf
        return _ffn_down(u, wd_bf, s, gate, n), wd_bf

    def hyena(z, l, n, spec):
        cv = _short_conv(z, l, n, hy_conv_w, hy_b)
        y1 = _long_conv((cv, cv), 2, 0, mats[n], spec, 0, hy_skip4, g_hy, l, n, False)
        return _long_conv((y1, cv), 0, 1, mats[n], spec, 1, hy_skip4, g_hy, l, n, True)

    for l in range(nl):
        need_ctx_out = l < nl - 1
        mod_x, mod_c = all_mods[l]

        sx, wd_bf = ffn(sx, _normmod(sx, mod_x[0], mod_x[1], SEQ), SEQ, mod_x[2], w1g, w1u, w1d, l)
        sc, _ = ffn(sc, _normmod(sc, mod_c[0], mod_c[1], CTX_LEN), CTX_LEN, mod_c[2], w1g, w1u, w1d,
                    l, wd_bf)

        zc = _norm_matmul(sc, mod_c[3], mod_c[4], CTX_LEN, w_in_p, l)
        zx = _norm_matmul(sx, mod_x[3], mod_x[4], SEQ, w_in_p, l)

        kc, vc = _kv_proj(zc, l, CTX_LEN, wukv, g_kv, g_kn, g_kr, tab_c)
        kx, vx = _kv_proj(zx, l, SEQ, wukv, g_kv, g_kn, g_kr, tab_x)
        zero_state = jnp.zeros((BATCH, RET_HEADS, 2, LANES, RET_DV), F32)
        ret_c, state_c = _retention(zc, l, CTX_LEN, dec, zero_state, gn_w, gn_b)

        filt = {}
        for n in ((SEQ, CTX_LEN) if need_ctx_out else (SEQ,)):
            s_f, d_f = _hy_filter(feats[n], hy_w1, hy_b1, hy_ffn_w2, hy_b2, hy_ffn_w3, deltas, l)
            fc, fs, _, alt = mats[n]
            filt[n] = _hy_spectrum(fc, fs, alt, s_f, d_f)

        qx = _q_proj(zx, l, SEQ, wuq_p, g_q, g_qn, g_qr, tab_x)
        attn_x = _attention(qx, [kc, kx], [vc, vx], [CTX_LEN, SEQ], l, g_attn)
        hy_x = hyena(zx, l, SEQ, filt[SEQ])
        ret_x, _ = _retention(zx, l, SEQ, dec, state_c, gn_w, gn_b)
        sx, hx = _w_out([attn_x, hy_x, ret_x], w_out_b, l, sx, mod_x[5], SEQ, (mod_x[6], mod_x[7]))
        sx, wd_bf = ffn(sx, hx, SEQ, mod_x[8], w2g, w2u, w2d, l)

        if need_ctx_out:
            qc = _q_proj(zc, l, CTX_LEN, wuq_p, g_q, g_qn, g_qr, tab_c)
            attn_c = _attention(qc, [kc], [vc], [CTX_LEN], l, g_attn)
            hy_c = hyena(zc, l, CTX_LEN, filt[CTX_LEN])
            sc, hc = _w_out([attn_c, hy_c, ret_c], w_out_b, l, sc, mod_c[5], CTX_LEN,
                            (mod_c[6], mod_c[7]))
            sc, _ = ffn(sc, hc, CTX_LEN, mod_c[8], w2g, w2u, w2d, l, wd_bf)

    return sx.reshape(BATCH, SEQ, d)
```
